```python
import math
import jax, jax.numpy as jnp
from jax import lax
import numpy as np

D_MODEL = 4096
BATCH = 1
SEQ = 8192
DEPTH = 2

HEAD_DIM = 128
N_HEADS_TOTAL = D_MODEL // HEAD_DIM
N_HEADS_A = N_HEADS_TOTAL // 2
N_KV_A = N_HEADS_A // 4
N_HEADS_B = N_HEADS_TOTAL - N_HEADS_A
WIDTH_A = N_HEADS_A * HEAD_DIM
WIDTH_B = N_HEADS_B * HEAD_DIM
KV_WIDTH_A = N_KV_A * HEAD_DIM
MIX_WIDTH = WIDTH_A + WIDTH_B
IN_COLS = WIDTH_A + 2 * KV_WIDTH_A + 3 * WIDTH_B
GRID_W = 64
WIN_H_MAX = 8
WIN_W = 16
MEM_LEN = 256
N_HEADS_MEM = 4
MEM_WIDTH = N_HEADS_MEM * HEAD_DIM
D_FF = 256 * ((8 * D_MODEL // 3 + 255) // 256)
ROPE_THETA = 10000.0
EPS = 1e-6
Q_BLOCK = 128

kernel_name = "hybrid_gqa_natten_macaron_encoder"


def rms_norm(x, g):
    xf = x.astype(jnp.float32)
    y = xf * lax.rsqrt(jnp.mean(xf * xf, axis=-1, keepdims=True) + EPS)
    return (y * g.astype(jnp.float32)).astype(x.dtype)


def swiglu(h, w_gate, w_up, w_down):
    return (jax.nn.silu(h @ w_gate) * (h @ w_up)) @ w_down


def axial_rope_tables(seq_len):
    t = jnp.arange(seq_len)
    row = (t // GRID_W).astype(jnp.float32)
    col = (t % GRID_W).astype(jnp.float32)
    axis_dim = HEAD_DIM // 2
    inv_freq = ROPE_THETA ** (-jnp.arange(0, axis_dim, 2, dtype=jnp.float32) / axis_dim)
    ang = jnp.concatenate([row[:, None] * inv_freq, col[:, None] * inv_freq], axis=-1)
    return jnp.cos(ang), jnp.sin(ang)


def apply_axial_rope(x, cos, sin):
    S = x.shape[1]
    nf = HEAD_DIM // 4
    xs = x.astype(jnp.float32).reshape(*x.shape[:-1], 2, 2, nf)
    x1, x2 = xs[..., 0, :], xs[..., 1, :]
    c = cos.reshape(S, 1, 2, nf)
    s = sin.reshape(S, 1, 2, nf)
    out = jnp.stack([x1 * c - x2 * s, x2 * c + x1 * s], axis=-2)
    return out.reshape(x.shape).astype(x.dtype)


def global_gqa(q, k, v):
    B, S, _, D = q.shape
    G = N_HEADS_A // N_KV_A
    nblk = S // Q_BLOCK
    qb = q.reshape(B, nblk, Q_BLOCK, N_KV_A, G, D).transpose(1, 0, 2, 3, 4, 5)
    scale = D ** -0.5

    def block(qblk):
        s = jnp.einsum('bqkgd,bskd->bkgqs', qblk, k, preferred_element_type=jnp.float32) * scale
        p = jax.nn.softmax(s, axis=-1).astype(v.dtype)
        return jnp.einsum('bkgqs,bskd->bqkgd', p, v)

    o = lax.map(block, qb)
    return o.transpose(1, 0, 2, 3, 4, 5).reshape(B, S, N_HEADS_A * D)


def neighbourhood_attn(q, k, v, rpb):
    B, S, H, D = q.shape
    rows = S // GRID_W
    kh = min(WIN_H_MAX, rows)
    qg = q.reshape(B, rows, GRID_W, H, D).transpose(1, 0, 2, 3, 4)
    kg = k.reshape(B, rows, GRID_W, H, D)
    vg = v.reshape(B, rows, GRID_W, H, D)
    cols = jnp.arange(GRID_W)
    col_start = jnp.clip(cols - WIN_W // 2, 0, GRID_W - WIN_W)
    col_idx = col_start[:, None] + jnp.arange(WIN_W)
    col_bias_idx = col_idx - cols[:, None] + (WIN_W - 1)
    rpb_cols = rpb[:, :, col_bias_idx]
    scale = D ** -0.5

    def row_block(args):
        i, qi = args
        rs = jnp.clip(i - kh // 2, 0, rows - kh)
        kr = lax.dynamic_slice_in_dim(kg, rs, kh, axis=1)
        vr = lax.dynamic_slice_in_dim(vg, rs, kh, axis=1)
        kn = kr[:, :, col_idx]
        vn = vr[:, :, col_idx]
        row_off = rs + jnp.arange(kh) - i + (WIN_H_MAX - 1)
        bias = jnp.take(rpb_cols, row_off, axis=1).transpose(0, 2, 1, 3)
        s = jnp.einsum('bqhd,baqkhd->bhqak', qi, kn, preferred_element_type=jnp.float32) * scale
        s = s + bias[None].astype(jnp.float32)
        p = jax.nn.softmax(s.reshape(B, H, GRID_W, kh * WIN_W), axis=-1)
        p = p.reshape(s.shape).astype(v.dtype)
        return jnp.einsum('bhqak,baqkhd->bqhd', p, vn)

    o = lax.map(row_block, (jnp.arange(rows), qg))
    return o.transpose(1, 0, 2, 3, 4).reshape(B, S, H * D)


def hybrid_mixer(h, w_in, q_norm_a, k_norm_a, rpb_b, out_norm_a, out_norm_b, w_out, cos, sin):
    B, S, _ = h.shape
    proj = h @ w_in
    c0 = WIDTH_A
    c1 = c0 + KV_WIDTH_A
    c2 = c1 + KV_WIDTH_A
    c3 = c2 + WIDTH_B
    c4 = c3 + WIDTH_B
    qa, ka, va, qb, kb, vb = jnp.split(proj, [c0, c1, c2, c3, c4], axis=-1)
    qa = qa.reshape(B, S, N_HEADS_A, HEAD_DIM)
    ka = ka.reshape(B, S, N_KV_A, HEAD_DIM)
    va = va.reshape(B, S, N_KV_A, HEAD_DIM)
    qa = apply_axial_rope(rms_norm(qa, q_norm_a), cos, sin)
    ka = apply_axial_rope(rms_norm(ka, k_norm_a), cos, sin)
    oa = global_gqa(qa, ka, va)
    ob = neighbourhood_attn(qb.reshape(B, S, N_HEADS_B, HEAD_DIM),
                            kb.reshape(B, S, N_HEADS_B, HEAD_DIM),
                            vb.reshape(B, S, N_HEADS_B, HEAD_DIM), rpb_b)
    o = jnp.concatenate([rms_norm(oa, out_norm_a), rms_norm(ob, out_norm_b)], axis=-1)
    return o @ w_out


def memory_xattn(h, m, wq, wkv, wo):
    B, S, _ = h.shape
    M = m.shape[1]
    q = (h @ wq).reshape(B, S, N_HEADS_MEM, HEAD_DIM)
    k, v = jnp.split(m @ wkv, 2, axis=-1)
    k = k.reshape(B, M, N_HEADS_MEM, HEAD_DIM)
    v = v.reshape(B, M, N_HEADS_MEM, HEAD_DIM)
    s = jnp.einsum('bqhd,bmhd->bhqm', q, k, preferred_element_type=jnp.float32) * (HEAD_DIM ** -0.5)
    p = jax.nn.softmax(s, axis=-1).astype(v.dtype)
    o = jnp.einsum('bhqm,bmhd->bqhd', p, v).reshape(B, S, MEM_WIDTH)
    return o @ wo


def setup_inputs(seed: int = 0) -> dict:
    key = jax.random.key(seed)
    ks = jax.random.split(key, 24)

    def w(k, shape, fan_in):
        return jax.random.normal(k, shape, jnp.float32) * (fan_in ** -0.5)

    def gain(k, shape):
        return 1.0 + 0.05 * jax.random.normal(k, shape, jnp.float32)

    L = DEPTH
    return {
        "x": jax.random.normal(ks[0], (BATCH, SEQ, D_MODEL), jnp.float32),
        "mem": jax.random.normal(ks[1], (BATCH, MEM_LEN, D_MODEL), jnp.float32),
        "ffn1_norm": gain(ks[2], (L, D_MODEL)),
        "ffn1_w_gate": w(ks[3], (L, D_MODEL, D_FF), D_MODEL),
        "ffn1_w_up": w(ks[4], (L, D_MODEL, D_FF), D_MODEL),
        "ffn1_w_down": w(ks[5], (L, D_FF, D_MODEL), D_FF),
        "mix_norm": gain(ks[6], (L, D_MODEL)),
        "w_in": w(ks[7], (L, D_MODEL, IN_COLS), D_MODEL),
        "q_norm_a": gain(ks[8], (L, HEAD_DIM)),
        "k_norm_a": gain(ks[9], (L, HEAD_DIM)),
        "rpb_b": 0.1 * jax.random.normal(ks[10], (L, N_HEADS_B, 2 * WIN_H_MAX - 1, 2 * WIN_W - 1), jnp.float32),
        "out_norm_a": gain(ks[11], (L, WIDTH_A)),
        "out_norm_b": gain(ks[12], (L, WIDTH_B)),
        "w_out": w(ks[13], (L, MIX_WIDTH, D_MODEL), MIX_WIDTH),
        "xattn_norm": gain(ks[14], (L, D_MODEL)),
        "mem_norm": gain(ks[15], (L, D_MODEL)),
        "xattn_wq": w(ks[16], (L, D_MODEL, MEM_WIDTH), D_MODEL),
        "xattn_wkv": w(ks[17], (L, D_MODEL, 2 * MEM_WIDTH), D_MODEL),
        "xattn_wo": w(ks[18], (L, MEM_WIDTH, D_MODEL), MEM_WIDTH),
        "ffn2_norm": gain(ks[19], (L, D_MODEL)),
        "ffn2_w_gate": w(ks[20], (L, D_MODEL, D_FF), D_MODEL),
        "ffn2_w_up": w(ks[21], (L, D_MODEL, D_FF), D_MODEL),
        "ffn2_w_down": w(ks[22], (L, D_FF, D_MODEL), D_FF),
        "final_norm": gain(ks[23], (D_MODEL,)),
    }


def reference(x, mem, ffn1_norm, ffn1_w_gate, ffn1_w_up, ffn1_w_down, mix_norm, w_in,
              q_norm_a, k_norm_a, rpb_b, out_norm_a, out_norm_b, w_out,
              xattn_norm, mem_norm, xattn_wq, xattn_wkv, xattn_wo,
              ffn2_norm, ffn2_w_gate, ffn2_w_up, ffn2_w_down, final_norm):
    cos, sin = axial_rope_tables(x.shape[1])
    for l in range(DEPTH):
        h = rms_norm(x, ffn1_norm[l])
        x = x + 0.5 * swiglu(h, ffn1_w_gate[l], ffn1_w_up[l], ffn1_w_down[l])
        h = rms_norm(x, mix_norm[l])
        x = x + hybrid_mixer(h, w_in[l], q_norm_a[l], k_norm_a[l], rpb_b[l],
                             out_norm_a[l], out_norm_b[l], w_out[l], cos, sin)
        h = rms_norm(x, xattn_norm[l])
        m = rms_norm(mem, mem_norm[l])
        x = x + memory_xattn(h, m, xattn_wq[l], xattn_wkv[l], xattn_wo[l])
        h = rms_norm(x, ffn2_norm[l])
        x = x + 0.5 * swiglu(h, ffn2_w_gate[l], ffn2_w_up[l], ffn2_w_down[l])
    return rms_norm(x, final_norm)
```

```python
import functools

import jax
import jax.numpy as jnp
from jax import lax
from jax.experimental import pallas as pl
from jax.experimental.pallas import tpu as pltpu

F32 = jnp.float32
BF16 = jnp.bfloat16

HEAD_DIM = 128
N_HEADS_A = 16
N_KV_A = 4
GQA_GROUP = N_HEADS_A // N_KV_A
N_HEADS_B = 16
WIDTH_A = N_HEADS_A * HEAD_DIM
WIDTH_B = N_HEADS_B * HEAD_DIM
KV_WIDTH_A = N_KV_A * HEAD_DIM
GRID_W = 64
WIN_H_MAX = 8
WIN_W = 16
N_HEADS_MEM = 4
MEM_WIDTH = N_HEADS_MEM * HEAD_DIM
ROPE_THETA = 10000.0
EPS = 1e-6
ATTN_SCALE = HEAD_DIM ** -0.5
MASK_VALUE = -1e30

V7X_VMEM_BYTES = 64 * 1024 * 1024
VMEM_LIMIT_BYTES = V7X_VMEM_BYTES - 6 * 1024 * 1024

TM = 1024
TN_UP = 256
TN_DOWN = 256
TN_PROJ = 512
NORM_CHUNK = 128
TQ = 256
TK = 512
NA_ROWS = 4
NA_KROWS = 12
TM_X = 256
TM_FINAL = 512


def _params(*sem):
    return pltpu.CompilerParams(dimension_semantics=sem,
                                vmem_limit_bytes=VMEM_LIMIT_BYTES)


def _single(shape, index_map):
    return pl.BlockSpec(shape, index_map, pipeline_mode=pl.Buffered(1))


def _rms_norm_rows(x_ref, g_ref, a_ref, col_off=0):
    tm, d = x_ref.shape
    chunk = min(NORM_CHUNK, tm)

    def body(c, carry):
        r = pl.multiple_of(c * chunk, chunk)
        x = x_ref[pl.ds(r, chunk), :].astype(F32)
        ms = jnp.mean(x * x, axis=-1, keepdims=True)
        y = (x * lax.rsqrt(ms + EPS)) * g_ref[...]
        a_ref[pl.ds(r, chunk), col_off:col_off + d] = y.astype(a_ref.dtype)
        return carry

    lax.fori_loop(0, tm // chunk, body, 0)


def _ffn_up_kernel(x_ref, g_ref, wg_ref, wu_ref, o_ref, a_ref):
    @pl.when(pl.program_id(1) == 0)
    def _():
        _rms_norm_rows(x_ref, g_ref, a_ref)

    a = a_ref[...]
    gate = jnp.dot(a, wg_ref[...], preferred_element_type=F32)
    up = jnp.dot(a, wu_ref[...], preferred_element_type=F32)
    o_ref[...] = ((gate * jax.nn.sigmoid(gate)) * up).astype(o_ref.dtype)


def _ffn_up(x, g, wg, wu):
    s, d = x.shape
    dff = wg.shape[1]
    return pl.pallas_call(
        _ffn_up_kernel,
        grid=(s // TM, dff // TN_UP),
        in_specs=[
            _single((TM, d), lambda i, j: (i, 0)),
            pl.BlockSpec((1, d), lambda i, j: (0, 0)),
            pl.BlockSpec((d, TN_UP), lambda i, j: (0, j)),
            pl.BlockSpec((d, TN_UP), lambda i, j: (0, j)),
        ],
        out_specs=pl.BlockSpec((TM, TN_UP), lambda i, j: (i, j)),
        out_shape=jax.ShapeDtypeStruct((s, dff), BF16),
        scratch_shapes=[pltpu.VMEM((TM, d), BF16)],
        compiler_params=_params("parallel", "arbitrary"),
        name="ffn_up",
    )(x, g, wg, wu)


def _ffn_down_kernel(a_ref, w_ref, r_ref, o_ref):
    acc = jnp.dot(a_ref[...], w_ref[...], preferred_element_type=F32)
    o_ref[...] = r_ref[...] + 0.5 * acc


def _ffn_down(act, wd, x):
    s, dff = act.shape
    d = wd.shape[1]
    return pl.pallas_call(
        _ffn_down_kernel,
        grid=(s // TM, d // TN_DOWN),
        in_specs=[
            _single((TM, dff), lambda i, j: (i, 0)),
            pl.BlockSpec((dff, TN_DOWN), lambda i, j: (0, j)),
            pl.BlockSpec((TM, TN_DOWN), lambda i, j: (i, j)),
        ],
        out_specs=pl.BlockSpec((TM, TN_DOWN), lambda i, j: (i, j)),
        out_shape=jax.ShapeDtypeStruct((s, d), F32),
        compiler_params=_params("parallel", "arbitrary"),
        name="ffn_down",
    )(act, wd, x)


def _in_proj_kernel(x_ref, g_ref, w_ref, qg_ref, kg_ref, cos_ref, sin_ref,
                    o_ref, a_ref):
    j = pl.program_id(1)
    n_q_tiles = WIDTH_A // TN_PROJ
    n_qk_tiles = (WIDTH_A + KV_WIDTH_A) // TN_PROJ

    @pl.when(j == 0)
    def _():
        _rms_norm_rows(x_ref, g_ref, a_ref)

    acc = jnp.dot(a_ref[...], w_ref[...], preferred_element_type=F32)

    @pl.when(j < n_qk_tiles)
    def _():
        is_q = j < n_q_tiles
        gain = jnp.where(is_q, qg_ref[...], kg_ref[...])
        scale = jnp.where(is_q, ATTN_SCALE, 1.0).astype(F32)
        cos = cos_ref[...]
        sin = sin_ref[...]
        lane = lax.broadcasted_iota(jnp.int32, (1, HEAD_DIM), 1)
        first_half = (lane % (HEAD_DIM // 2)) < (HEAD_DIM // 4)
        for h in range(TN_PROJ // HEAD_DIM):
            xh = acc[:, h * HEAD_DIM:(h + 1) * HEAD_DIM]
            ms = jnp.mean(xh * xh, axis=-1, keepdims=True)
            y = (xh * lax.rsqrt(ms + EPS)) * gain
            partner = jnp.where(first_half,
                                pltpu.roll(y, HEAD_DIM - HEAD_DIM // 4, 1),
                                pltpu.roll(y, HEAD_DIM // 4, 1))
            out = (y * cos + partner * sin) * scale
            o_ref[:, h * HEAD_DIM:(h + 1) * HEAD_DIM] = out.astype(o_ref.dtype)

    @pl.when(j >= n_qk_tiles)
    def _():
        o_ref[...] = acc.astype(o_ref.dtype)


def _in_proj(x, g, w, qg, kg, cos, sin):
    s, d = x.shape
    n = w.shape[1]
    return pl.pallas_call(
        _in_proj_kernel,
        grid=(s // TM, n // TN_PROJ),
        in_specs=[
            _single((TM, d), lambda i, j: (i, 0)),
            pl.BlockSpec((1, d), lambda i, j: (0, 0)),
            pl.BlockSpec((d, TN_PROJ), lambda i, j: (0, j)),
            pl.BlockSpec((1, HEAD_DIM), lambda i, j: (0, 0)),
            pl.BlockSpec((1, HEAD_DIM), lambda i, j: (0, 0)),
            pl.BlockSpec((TM, HEAD_DIM), lambda i, j: (i, 0)),
            pl.BlockSpec((TM, HEAD_DIM), lambda i, j: (i, 0)),
        ],
        out_specs=pl.BlockSpec((TM, TN_PROJ), lambda i, j: (i, j)),
        out_shape=jax.ShapeDtypeStruct((s, n), BF16),
        scratch_shapes=[pltpu.VMEM((TM, d), BF16)],
        compiler_params=_params("parallel", "arbitrary"),
        name="in_proj",
    )(x, g, w, qg, kg, cos, sin)


def _gqa_kernel(q_ref, k_ref, v_ref, o_ref, qs_ref):
    tq = q_ref.shape[0]
    s_len = k_ref.shape[0]
    rows = GQA_GROUP * tq
    for g in range(GQA_GROUP):
        qs_ref[g * tq:(g + 1) * tq, :] = q_ref[:, g * HEAD_DIM:(g + 1) * HEAD_DIM]
    q = qs_ref[...]

    def body(c, carry):
        m, l, acc = carry
        r = pl.multiple_of(c * TK, TK)
        kc = k_ref[pl.ds(r, TK), :]
        vc = v_ref[pl.ds(r, TK), :]
        s = lax.dot_general(q, kc, (((1,), (1,)), ((), ())),
                            preferred_element_type=F32)
        m_new = jnp.maximum(m, jnp.max(s, axis=-1, keepdims=True))
        alpha = jnp.exp(m - m_new)
        p = jnp.exp(s - m_new)
        l = alpha * l + jnp.sum(p, axis=-1, keepdims=True)
        acc = alpha * acc + jnp.dot(p.astype(BF16), vc,
                                    preferred_element_type=F32)
        return m_new, l, acc

    init = (jnp.full((rows, 1), MASK_VALUE, F32), jnp.zeros((rows, 1), F32),
            jnp.zeros((rows, HEAD_DIM), F32))
    _, l, acc = lax.fori_loop(0, s_len // TK, body, init)
    out = acc / l
    for g in range(GQA_GROUP):
        o_ref[:, g * HEAD_DIM:(g + 1) * HEAD_DIM] = out[g * tq:(g + 1) * tq]


def _gqa(proj):
    s = proj.shape[0]
    k_blk = WIDTH_A // HEAD_DIM
    v_blk = (WIDTH_A + KV_WIDTH_A) // HEAD_DIM
    gw = GQA_GROUP * HEAD_DIM
    return pl.pallas_call(
        _gqa_kernel,
        grid=(N_KV_A, s // TQ),
        in_specs=[
            pl.BlockSpec((TQ, gw), lambda h, i: (i, h)),
            pl.BlockSpec((s, HEAD_DIM), lambda h, i: (0, k_blk + h)),
            pl.BlockSpec((s, HEAD_DIM), lambda h, i: (0, v_blk + h)),
        ],
        out_specs=pl.BlockSpec((TQ, gw), lambda h, i: (i, h)),
        out_shape=jax.ShapeDtypeStruct((s, WIDTH_A), F32),
        scratch_shapes=[pltpu.VMEM((GQA_GROUP * TQ, HEAD_DIM), BF16)],
        compiler_params=_params("parallel", "arbitrary"),
        name="gqa",
    )(proj, proj, proj)


def _na_classes(grid_rows):
    kh = min(WIN_H_MAX, grid_rows)
    n_blocks = grid_rows // NA_ROWS
    classes = []
    for rb in (0, 1, n_blocks - 1):
        start = min(max(NA_ROWS * rb - NA_ROWS, 0), grid_rows - NA_KROWS)
        table = []
        for qi in range(NA_ROWS):
            i = NA_ROWS * rb + qi
            rs = min(max(i - kh // 2, 0), grid_rows - kh)
            row = []
            for a in range(NA_KROWS):
                r = start + a
                row.append(r - i + (WIN_H_MAX - 1) if rs <= r < rs + kh else None)
            table.append(row)
        classes.append(table)
    return classes


def _na_kernel(rpb_ref, q_ref, k_ref, v_ref, o_ref, strip_ref, bias_ref, *,
               classes, n_blocks):
    h = pl.program_id(0)
    n_rpb_rows = 2 * WIN_H_MAX - 1
    n_rpb_cols = 2 * WIN_W - 1
    qblk = NA_ROWS * GRID_W
    kblk = NA_KROWS * GRID_W

    jj = lax.broadcasted_iota(jnp.int32, (GRID_W, 2 * GRID_W), 0)
    cc = lax.broadcasted_iota(jnp.int32, (GRID_W, 2 * GRID_W), 1) % GRID_W
    rel = cc - jj + (WIN_W - 1)
    cs = jnp.clip(jj - WIN_W // 2, 0, GRID_W - WIN_W)
    col_ok = (cc >= cs) & (cc < cs + WIN_W)
    for dr in range(n_rpb_rows):
        base = (h * n_rpb_rows + dr) * n_rpb_cols

        def pick(d, t, base=base):
            return jnp.where(rel == d, rpb_ref[base + d], t)

        strip = lax.fori_loop(0, n_rpb_cols, pick,
                              jnp.zeros((GRID_W, 2 * GRID_W), F32))
        strip_ref[dr] = jnp.where(col_ok, strip, MASK_VALUE)

    left = lax.broadcasted_iota(jnp.int32, (GRID_W, 2 * GRID_W), 1) < GRID_W
    masked = jnp.full((GRID_W, 2 * GRID_W), MASK_VALUE, F32)
    for cls, table in enumerate(classes):
        for qi in range(NA_ROWS):
            for ap in range(NA_KROWS // 2):
                dl, dr_ = table[qi][2 * ap], table[qi][2 * ap + 1]
                lhs = masked if dl is None else strip_ref[dl]
                rhs = masked if dr_ is None else strip_ref[dr_]
                bias_ref[cls, qi * GRID_W:(qi + 1) * GRID_W,
                         ap * 2 * GRID_W:(ap + 1) * 2 * GRID_W] = (
                             jnp.where(left, lhs, rhs))

    def body(rb, carry):
        cls = jnp.where(rb == 0, 0, jnp.where(rb == n_blocks - 1, 2, 1))
        sb = jnp.clip(rb - 1, 0, n_blocks - NA_KROWS // NA_ROWS)
        q0 = pl.multiple_of(rb * qblk, qblk)
        k0 = pl.multiple_of(sb * qblk, qblk)
        q = q_ref[pl.ds(q0, qblk), :]
        kw = k_ref[pl.ds(k0, kblk), :]
        vw = v_ref[pl.ds(k0, kblk), :]
        s = lax.dot_general(q, kw, (((1,), (1,)), ((), ())),
                            preferred_element_type=F32)
        s = s * ATTN_SCALE + bias_ref[cls]
        m = jnp.max(s, axis=-1, keepdims=True)
        p = jnp.exp(s - m)
        l = jnp.sum(p, axis=-1, keepdims=True)
        o = jnp.dot(p.astype(BF16), vw, preferred_element_type=F32)
        o_ref[pl.ds(q0, qblk), :] = o / l
        return carry

    lax.fori_loop(0, n_blocks, body, 0)


def _na(proj, rpb_flat):
    s = proj.shape[0]
    grid_rows = s // GRID_W
    n_blocks = grid_rows // NA_ROWS
    q_blk = (WIDTH_A + 2 * KV_WIDTH_A) // HEAD_DIM
    k_blk = q_blk + N_HEADS_B
    v_blk = k_blk + N_HEADS_B
    kern = functools.partial(_na_kernel, classes=_na_classes(grid_rows),
                             n_blocks=n_blocks)
    return pl.pallas_call(
        kern,
        grid=(N_HEADS_B,),
        in_specs=[
            pl.BlockSpec(memory_space=pltpu.SMEM),
            pl.BlockSpec((s, HEAD_DIM), lambda h: (0, q_blk + h)),
            pl.BlockSpec((s, HEAD_DIM), lambda h: (0, k_blk + h)),
            pl.BlockSpec((s, HEAD_DIM), lambda h: (0, v_blk + h)),
        ],
        out_specs=pl.BlockSpec((s, HEAD_DIM), lambda h: (0, h)),
        out_shape=jax.ShapeDtypeStruct((s, WIDTH_B), F32),
        scratch_shapes=[
            pltpu.VMEM((2 * WIN_H_MAX - 1, GRID_W, 2 * GRID_W), F32),
            pltpu.VMEM((3, NA_ROWS * GRID_W, NA_KROWS * GRID_W), F32),
        ],
        compiler_params=_params("arbitrary"),
        name="na",
    )(rpb_flat, proj, proj, proj)


def _out_proj_kernel(oa_ref, ob_ref, ga_ref, gb_ref, w_ref, r_ref, o_ref, a_ref):
    @pl.when(pl.program_id(1) == 0)
    def _():
        _rms_norm_rows(oa_ref, ga_ref, a_ref, 0)
        _rms_norm_rows(ob_ref, gb_ref, a_ref, oa_ref.shape[1])

    acc = jnp.dot(a_ref[...], w_ref[...], preferred_element_type=F32)
    o_ref[...] = r_ref[...] + acc


def _out_proj(oa, ob, ga, gb, w, x):
    s, wa = oa.shape
    wb = ob.shape[1]
    d = w.shape[1]
    return pl.pallas_call(
        _out_proj_kernel,
        grid=(s // TM, d // TN_PROJ),
        in_specs=[
            _single((TM, wa), lambda i, j: (i, 0)),
            _single((TM, wb), lambda i, j: (i, 0)),
            pl.BlockSpec((1, wa), lambda i, j: (0, 0)),
            pl.BlockSpec((1, wb), lambda i, j: (0, 0)),
            pl.BlockSpec((wa + wb, TN_PROJ), lambda i, j: (0, j)),
            pl.BlockSpec((TM, TN_PROJ), lambda i, j: (i, j)),
        ],
        out_specs=pl.BlockSpec((TM, TN_PROJ), lambda i, j: (i, j)),
        out_shape=jax.ShapeDtypeStruct((s, d), F32),
        scratch_shapes=[pltpu.VMEM((TM, wa + wb), BF16)],
        compiler_params=_params("parallel", "arbitrary"),
        name="out_proj",
    )(oa, ob, ga, gb, w, x)


def _mem_kv_kernel(m_ref, g_ref, w_ref, o_ref, a_ref):
    @pl.when(pl.program_id(0) == 0)
    def _():
        _rms_norm_rows(m_ref, g_ref, a_ref)

    o_ref[...] = jnp.dot(a_ref[...], w_ref[...],
                         preferred_element_type=F32).astype(o_ref.dtype)


def _mem_kv(mem, g, wkv):
    m, d = mem.shape
    n = wkv.shape[1]
    return pl.pallas_call(
        _mem_kv_kernel,
        grid=(n // TN_PROJ,),
        in_specs=[
            pl.BlockSpec((m, d), lambda j: (0, 0)),
            pl.BlockSpec((1, d), lambda j: (0, 0)),
            pl.BlockSpec((d, TN_PROJ), lambda j: (0, j)),
        ],
        out_specs=pl.BlockSpec((m, TN_PROJ), lambda j: (0, j)),
        out_shape=jax.ShapeDtypeStruct((m, n), BF16),
        scratch_shapes=[pltpu.VMEM((m, d), BF16)],
        compiler_params=_params("arbitrary"),
        name="mem_kv",
    )(mem, g, wkv)


def _xattn_kernel(x_ref, g_ref, wq_ref, kv_ref, wo_ref, o_ref, h_ref, oc_ref):
    _rms_norm_rows(x_ref, g_ref, h_ref)
    q = jnp.dot(h_ref[...], wq_ref[...], preferred_element_type=F32)
    for hd in range(N_HEADS_MEM):
        lo, hi = hd * HEAD_DIM, (hd + 1) * HEAD_DIM
        qh = q[:, lo:hi].astype(BF16)
        kh = kv_ref[:, lo:hi]
        vh = kv_ref[:, MEM_WIDTH + lo:MEM_WIDTH + hi]
        s = lax.dot_general(qh, kh, (((1,), (1,)), ((), ())),
                            preferred_element_type=F32) * ATTN_SCALE
        m = jnp.max(s, axis=-1, keepdims=True)
        p = jnp.exp(s - m)
        l = jnp.sum(p, axis=-1, keepdims=True)
        o = jnp.dot(p.astype(BF16), vh, preferred_element_type=F32) / l
        oc_ref[:, lo:hi] = o.astype(oc_ref.dtype)
    o_ref[...] = x_ref[...] + jnp.dot(oc_ref[...], wo_ref[...],
                                      preferred_element_type=F32)


def _xattn(x, g, wq, kv, wo):
    s, d = x.shape
    m = kv.shape[0]
    return pl.pallas_call(
        _xattn_kernel,
        grid=(s // TM_X,),
        in_specs=[
            pl.BlockSpec((TM_X, d), lambda i: (i, 0)),
            pl.BlockSpec((1, d), lambda i: (0, 0)),
            pl.BlockSpec((d, MEM_WIDTH), lambda i: (0, 0)),
            pl.BlockSpec((m, 2 * MEM_WIDTH), lambda i: (0, 0)),
            pl.BlockSpec((MEM_WIDTH, d), lambda i: (0, 0)),
        ],
        out_specs=pl.BlockSpec((TM_X, d), lambda i: (i, 0)),
        out_shape=jax.ShapeDtypeStruct((s, d), F32),
        scratch_shapes=[pltpu.VMEM((TM_X, d), BF16),
                        pltpu.VMEM((TM_X, MEM_WIDTH), BF16)],
        compiler_params=_params("parallel"),
        name="xattn",
    )(x, g, wq, kv, wo)


def _final_norm_kernel(x_ref, g_ref, o_ref):
    _rms_norm_rows(x_ref, g_ref, o_ref)


def _final_norm(x, g):
    s, d = x.shape
    return pl.pallas_call(
        _final_norm_kernel,
        grid=(s // TM_FINAL,),
        in_specs=[pl.BlockSpec((TM_FINAL, d), lambda i: (i, 0)),
                  pl.BlockSpec((1, d), lambda i: (0, 0))],
        out_specs=pl.BlockSpec((TM_FINAL, d), lambda i: (i, 0)),
        out_shape=jax.ShapeDtypeStruct((s, d), F32),
        compiler_params=_params("parallel"),
        name="final_norm",
    )(x, g)


def _rope_tables(seq_len):
    t = jnp.arange(seq_len)
    row = (t // GRID_W).astype(F32)
    col = (t % GRID_W).astype(F32)
    axis_dim = HEAD_DIM // 2
    inv_freq = ROPE_THETA ** (-jnp.arange(0, axis_dim, 2, dtype=F32) / axis_dim)
    ang_r = row[:, None] * inv_freq
    ang_c = col[:, None] * inv_freq
    cos = jnp.concatenate([jnp.cos(ang_r), jnp.cos(ang_r),
                           jnp.cos(ang_c), jnp.cos(ang_c)], axis=-1)
    sin = jnp.concatenate([-jnp.sin(ang_r), jnp.sin(ang_r),
                           -jnp.sin(ang_c), jnp.sin(ang_c)], axis=-1)
    return cos, sin


def kernel(x, mem, ffn1_norm, ffn1_w_gate, ffn1_w_up, ffn1_w_down, mix_norm, w_in, q_norm_a, k_norm_a, rpb_b, out_norm_a, out_norm_b, w_out, xattn_norm, mem_norm, xattn_wq, xattn_wkv, xattn_wo, ffn2_norm, ffn2_w_gate, ffn2_w_up, ffn2_w_down, final_norm):
    batch, seq_len, d_model = x.shape
    depth = w_in.shape[0]
    assert batch == 1 and mem.shape[0] == 1
    assert seq_len % GRID_W == 0 and (seq_len // GRID_W) % NA_ROWS == 0

    cos, sin = _rope_tables(seq_len)
    row = lambda v: v.reshape(1, -1)
    xs = x[0]
    mem2 = mem[0]
    for l in range(depth):
        bf = lambda w: w[l].astype(BF16)

        act = _ffn_up(xs, row(ffn1_norm[l]), bf(ffn1_w_gate), bf(ffn1_w_up))
        xs = _ffn_down(act, bf(ffn1_w_down), xs)

        proj = _in_proj(xs, row(mix_norm[l]), bf(w_in), row(q_norm_a[l]),
                        row(k_norm_a[l]), cos, sin)
        oa = _gqa(proj)
        ob = _na(proj, rpb_b[l].reshape(-1))
        xs = _out_proj(oa, ob, row(out_norm_a[l]), row(out_norm_b[l]),
                       bf(w_out), xs)

        kv = _mem_kv(mem2, row(mem_norm[l]), bf(xattn_wkv))
        xs = _xattn(xs, row(xattn_norm[l]), bf(xattn_wq), kv, bf(xattn_wo))

        act = _ffn_up(xs, row(ffn2_norm[l]), bf(ffn2_w_gate), bf(ffn2_w_up))
        xs = _ffn_down(act, bf(ffn2_w_down), xs)

    return _final_norm(xs, row(final_norm))[None]
```

```python
import functools

import jax
import jax.numpy as jnp
from jax import lax
from jax.experimental import pallas as pl
from jax.experimental.pallas import tpu as pltpu

F32 = jnp.float32
BF16 = jnp.bfloat16

HEAD_DIM = 128
N_HEADS_A = 16
N_KV_A = 4
GQA_GROUP = N_HEADS_A // N_KV_A
N_HEADS_B = 16
WIDTH_A = N_HEADS_A * HEAD_DIM
WIDTH_B = N_HEADS_B * HEAD_DIM
KV_WIDTH_A = N_KV_A * HEAD_DIM
GRID_W = 64
WIN_H_MAX = 8
WIN_W = 16
N_HEADS_MEM = 4
MEM_WIDTH = N_HEADS_MEM * HEAD_DIM
ROPE_THETA = 10000.0
EPS = 1e-6
ATTN_SCALE = HEAD_DIM ** -0.5
LOG2_E = 1.4426950408889634
MASK_VALUE = -1e30

V7X_VMEM_BYTES = 64 * 1024 * 1024
VMEM_LIMIT_BYTES = V7X_VMEM_BYTES - 6 * 1024 * 1024

TM = 1024
TN_UP = 256
TN_DOWN = 256
TN_PROJ = 512
NORM_CHUNK = 128
TQ = 256
TK = 512
NA_ROWS = 4
NA_KROWS = 12
TM_X = 256
TM_FINAL = 512


def _params(*sem):
    return pltpu.CompilerParams(dimension_semantics=sem,
                                vmem_limit_bytes=VMEM_LIMIT_BYTES)


def _single(shape, index_map):
    return pl.BlockSpec(shape, index_map, pipeline_mode=pl.Buffered(1))


def _rms_norm_rows(x_ref, g_ref, a_ref, col_off=0):
    tm, d = x_ref.shape
    chunk = min(NORM_CHUNK, tm)

    def body(c, carry):
        r = pl.multiple_of(c * chunk, chunk)
        x = x_ref[pl.ds(r, chunk), :].astype(F32)
        ms = jnp.mean(x * x, axis=-1, keepdims=True)
        y = (x * lax.rsqrt(ms + EPS)) * g_ref[...]
        a_ref[pl.ds(r, chunk), col_off:col_off + d] = y.astype(a_ref.dtype)
        return carry

    lax.fori_loop(0, tm // chunk, body, 0)


def _ffn_up_kernel(x_ref, g_ref, wg_ref, wu_ref, o_ref, a_ref):
    @pl.when(pl.program_id(1) == 0)
    def _():
        _rms_norm_rows(x_ref, g_ref, a_ref)

    a = a_ref[...]
    gate = jnp.dot(a, wg_ref[...].astype(BF16), preferred_element_type=F32)
    up = jnp.dot(a, wu_ref[...].astype(BF16), preferred_element_type=F32)
    o_ref[...] = ((gate * jax.nn.sigmoid(gate)) * up).astype(o_ref.dtype)


def _ffn_up(x, g, wg, wu, layer):
    s, d = x.shape
    dff = wg.shape[2]
    return pl.pallas_call(
        _ffn_up_kernel,
        grid=(s // TM, dff // TN_UP),
        in_specs=[
            _single((TM, d), lambda i, j: (i, 0)),
            pl.BlockSpec((1, d), lambda i, j: (0, 0)),
            pl.BlockSpec((None, d, TN_UP), lambda i, j: (layer, 0, j)),
            pl.BlockSpec((None, d, TN_UP), lambda i, j: (layer, 0, j)),
        ],
        out_specs=pl.BlockSpec((TM, TN_UP), lambda i, j: (i, j)),
        out_shape=jax.ShapeDtypeStruct((s, dff), BF16),
        scratch_shapes=[pltpu.VMEM((TM, d), BF16)],
        compiler_params=_params("parallel", "arbitrary"),
        name="ffn_up",
    )(x, g, wg, wu)


def _ffn_down_kernel(a_ref, w_ref, r_ref, o_ref):
    acc = jnp.dot(a_ref[...], w_ref[...], preferred_element_type=F32)
    o_ref[...] = r_ref[...] + 0.5 * acc


def _ffn_down(act, wd, x):
    s, dff = act.shape
    d = wd.shape[1]
    return pl.pallas_call(
        _ffn_down_kernel,
        grid=(s // TM, d // TN_DOWN),
        in_specs=[
            _single((TM, dff), lambda i, j: (i, 0)),
            pl.BlockSpec((dff, TN_DOWN), lambda i, j: (0, j)),
            pl.BlockSpec((TM, TN_DOWN), lambda i, j: (i, j)),
        ],
        out_specs=pl.BlockSpec((TM, TN_DOWN), lambda i, j: (i, j)),
        out_shape=jax.ShapeDtypeStruct((s, d), F32),
        compiler_params=_params("parallel", "arbitrary"),
        name="ffn_down",
    )(act, wd, x)


def _in_proj_kernel(x_ref, g_ref, w_ref, qg_ref, kg_ref, cos_ref, sin_ref,
                    o_ref, a_ref):
    j = pl.program_id(1)
    n_q_tiles = WIDTH_A // TN_PROJ
    n_qk_tiles = (WIDTH_A + KV_WIDTH_A) // TN_PROJ

    @pl.when(j == 0)
    def _():
        _rms_norm_rows(x_ref, g_ref, a_ref)

    def project():
        return jnp.dot(a_ref[...], w_ref[...].astype(BF16),
                       preferred_element_type=F32)

    @pl.when(j < n_qk_tiles)
    def _():
        acc = project()
        is_q = j < n_q_tiles
        gain = jnp.where(is_q, qg_ref[...], kg_ref[...])
        scale = jnp.where(is_q, ATTN_SCALE * LOG2_E, 1.0).astype(F32)
        cos = cos_ref[...]
        sin = sin_ref[...]
        lane = lax.broadcasted_iota(jnp.int32, (1, HEAD_DIM), 1)
        first_half = (lane % (HEAD_DIM // 2)) < (HEAD_DIM // 4)
        for h in range(TN_PROJ // HEAD_DIM):
            xh = acc[:, h * HEAD_DIM:(h + 1) * HEAD_DIM]
            ms = jnp.mean(xh * xh, axis=-1, keepdims=True)
            y = (xh * lax.rsqrt(ms + EPS)) * gain
            partner = jnp.where(first_half,
                                pltpu.roll(y, HEAD_DIM - HEAD_DIM // 4, 1),
                                pltpu.roll(y, HEAD_DIM // 4, 1))
            out = (y * cos + partner * sin) * scale
            o_ref[:, h * HEAD_DIM:(h + 1) * HEAD_DIM] = out.astype(o_ref.dtype)

    @pl.when(j >= n_qk_tiles)
    def _():
        o_ref[...] = project().astype(o_ref.dtype)


def _in_proj(x, g, w, layer, qg, kg, cos, sin):
    s, d = x.shape
    n = w.shape[2]
    return pl.pallas_call(
        _in_proj_kernel,
        grid=(s // TM, n // TN_PROJ),
        in_specs=[
            _single((TM, d), lambda i, j: (i, 0)),
            pl.BlockSpec((1, d), lambda i, j: (0, 0)),
            pl.BlockSpec((None, d, TN_PROJ), lambda i, j: (layer, 0, j)),
            pl.BlockSpec((1, HEAD_DIM), lambda i, j: (0, 0)),
            pl.BlockSpec((1, HEAD_DIM), lambda i, j: (0, 0)),
            pl.BlockSpec((TM, HEAD_DIM), lambda i, j: (i, 0)),
            pl.BlockSpec((TM, HEAD_DIM), lambda i, j: (i, 0)),
        ],
        out_specs=pl.BlockSpec((TM, TN_PROJ), lambda i, j: (i, j)),
        out_shape=jax.ShapeDtypeStruct((s, n), BF16),
        scratch_shapes=[pltpu.VMEM((TM, d), BF16)],
        compiler_params=_params("parallel", "arbitrary"),
        name="in_proj",
    )(x, g, w, qg, kg, cos, sin)


def _gqa_kernel(q_ref, k_ref, v_ref, o_ref, qs_ref, vt_ref, m_ref, l_ref,
                acc_ref, st_ref):
    tq = q_ref.shape[0]
    s_len = k_ref.shape[0]
    n_chunks = s_len // TK

    @pl.when(pl.program_id(1) == 0)
    def _():
        def transpose_chunk(c, carry):
            r = pl.multiple_of(c * TK, TK)
            vc = v_ref[pl.ds(r, TK), :].astype(F32)
            vt_ref[:, pl.ds(r, TK)] = vc.T.astype(vt_ref.dtype)
            return carry

        lax.fori_loop(0, n_chunks, transpose_chunk, 0)

    for g in range(GQA_GROUP):
        qs_ref[g * tq:(g + 1) * tq, :] = q_ref[:, g * HEAD_DIM:(g + 1) * HEAD_DIM]
    m_ref[...] = jnp.full(m_ref.shape, MASK_VALUE, F32)
    l_ref[...] = jnp.zeros(l_ref.shape, F32)
    acc_ref[...] = jnp.zeros(acc_ref.shape, F32)

    def scores(c, slot):
        r = pl.multiple_of(c * TK, TK)
        st_ref[slot] = lax.dot_general(k_ref[pl.ds(r, TK), :], qs_ref[...],
                                       (((1,), (1,)), ((), ())),
                                       preferred_element_type=F32)

    def softmax_pv(c, slot):
        r = pl.multiple_of(c * TK, TK)
        for g in range(GQA_GROUP):
            cols = slice(g * tq, (g + 1) * tq)
            st = st_ref[slot, :, cols]
            m_old = m_ref[:, cols]
            m_new = jnp.maximum(m_old, jnp.max(st, axis=0, keepdims=True))
            alpha = jnp.exp2(m_old - m_new)
            p = jnp.exp2(st - m_new)
            l_ref[:, cols] = alpha * l_ref[:, cols] + jnp.sum(p, axis=0,
                                                              keepdims=True)
            pv = jnp.dot(vt_ref[:, pl.ds(r, TK)], p.astype(BF16),
                         preferred_element_type=F32)
            acc_ref[:, cols] = alpha * acc_ref[:, cols] + pv
            m_ref[:, cols] = m_new

    scores(0, 0)

    def body(i, carry):
        c = 2 * i
        scores(c + 1, 1)
        softmax_pv(c, 0)
        scores(jnp.minimum(c + 2, n_chunks - 1), 0)
        softmax_pv(c + 1, 1)
        return carry

    lax.fori_loop(0, n_chunks // 2, body, 0)
    out = (acc_ref[...] / l_ref[...]).T
    for g in range(GQA_GROUP):
        o_ref[:, g * HEAD_DIM:(g + 1) * HEAD_DIM] = out[g * tq:(g + 1) * tq]


def _gqa(proj):
    s = proj.shape[0]
    k_blk = WIDTH_A // HEAD_DIM
    v_blk = (WIDTH_A + KV_WIDTH_A) // HEAD_DIM
    gw = GQA_GROUP * HEAD_DIM
    return pl.pallas_call(
        _gqa_kernel,
        grid=(N_KV_A, s // TQ),
        in_specs=[
            pl.BlockSpec((TQ, gw), lambda h, i: (i, h)),
            pl.BlockSpec((s, HEAD_DIM), lambda h, i: (0, k_blk + h)),
            pl.BlockSpec((s, HEAD_DIM), lambda h, i: (0, v_blk + h)),
        ],
        out_specs=pl.BlockSpec((TQ, gw), lambda h, i: (i, h)),
        out_shape=jax.ShapeDtypeStruct((s, WIDTH_A), F32),
        scratch_shapes=[
            pltpu.VMEM((GQA_GROUP * TQ, HEAD_DIM), BF16),
            pltpu.VMEM((HEAD_DIM, s), BF16),
            pltpu.VMEM((1, GQA_GROUP * TQ), F32),
            pltpu.VMEM((1, GQA_GROUP * TQ), F32),
            pltpu.VMEM((HEAD_DIM, GQA_GROUP * TQ), F32),
            pltpu.VMEM((2, TK, GQA_GROUP * TQ), F32),
        ],
        compiler_params=_params("arbitrary", "arbitrary"),
        name="gqa",
    )(proj, proj, proj)


def _na_classes(grid_rows):
    kh = min(WIN_H_MAX, grid_rows)
    n_blocks = grid_rows // NA_ROWS
    classes = []
    for rb in (0, 1, n_blocks - 1):
        start = min(max(NA_ROWS * rb - NA_ROWS, 0), grid_rows - NA_KROWS)
        table = []
        for qi in range(NA_ROWS):
            i = NA_ROWS * rb + qi
            rs = min(max(i - kh // 2, 0), grid_rows - kh)
            row = []
            for a in range(NA_KROWS):
                r = start + a
                row.append(r - i + (WIN_H_MAX - 1) if rs <= r < rs + kh else None)
            table.append(row)
        classes.append(table)
    return classes


def _na_kernel(rpb_ref, q_ref, k_ref, v_ref, o_ref, strip_ref, bias_ref, s_ref,
               *, classes, n_blocks):
    h = pl.program_id(0)
    n_rpb_rows = 2 * WIN_H_MAX - 1
    n_rpb_cols = 2 * WIN_W - 1
    qblk = NA_ROWS * GRID_W
    kblk = NA_KROWS * GRID_W

    jj = lax.broadcasted_iota(jnp.int32, (GRID_W, 2 * GRID_W), 0)
    cc = lax.broadcasted_iota(jnp.int32, (GRID_W, 2 * GRID_W), 1) % GRID_W
    rel = cc - jj + (WIN_W - 1)
    cs = jnp.clip(jj - WIN_W // 2, 0, GRID_W - WIN_W)
    col_ok = (cc >= cs) & (cc < cs + WIN_W)
    for dr in range(n_rpb_rows):
        base = (h * n_rpb_rows + dr) * n_rpb_cols

        def pick(d, t, base=base):
            return jnp.where(rel == d, rpb_ref[base + d], t)

        strip = lax.fori_loop(0, n_rpb_cols, pick,
                              jnp.zeros((GRID_W, 2 * GRID_W), F32))
        strip_ref[dr] = jnp.where(col_ok, strip, MASK_VALUE)

    left = lax.broadcasted_iota(jnp.int32, (GRID_W, 2 * GRID_W), 1) < GRID_W
    masked = jnp.full((GRID_W, 2 * GRID_W), MASK_VALUE, F32)
    for cls, table in enumerate(classes):
        for qi in range(NA_ROWS):
            for ap in range(NA_KROWS // 2):
                dl, dr_ = table[qi][2 * ap], table[qi][2 * ap + 1]
                lhs = masked if dl is None else strip_ref[dl]
                rhs = masked if dr_ is None else strip_ref[dr_]
                bias_ref[cls, qi * GRID_W:(qi + 1) * GRID_W,
                         ap * 2 * GRID_W:(ap + 1) * 2 * GRID_W] = (
                             jnp.where(left, lhs, rhs))

    def key_start(rb):
        sb = jnp.clip(rb - 1, 0, n_blocks - NA_KROWS // NA_ROWS)
        return pl.multiple_of(sb * qblk, qblk)

    def scores(rb, slot):
        q0 = pl.multiple_of(rb * qblk, qblk)
        s_ref[slot] = lax.dot_general(q_ref[pl.ds(q0, qblk), :],
                                      k_ref[pl.ds(key_start(rb), kblk), :],
                                      (((1,), (1,)), ((), ())),
                                      preferred_element_type=F32)

    def softmax_pv(rb, slot):
        cls = jnp.where(rb == 0, 0, jnp.where(rb == n_blocks - 1, 2, 1))
        q0 = pl.multiple_of(rb * qblk, qblk)
        s = s_ref[slot] * ATTN_SCALE + bias_ref[cls]
        m = jnp.max(s, axis=-1, keepdims=True)
        p = jnp.exp(s - m)
        l = jnp.sum(p, axis=-1, keepdims=True)
        o = jnp.dot(p.astype(BF16), v_ref[pl.ds(key_start(rb), kblk), :],
                    preferred_element_type=F32)
        o_ref[pl.ds(q0, qblk), :] = o / l

    scores(0, 0)

    def body(i, carry):
        rb = 2 * i
        scores(rb + 1, 1)
        softmax_pv(rb, 0)
        scores(jnp.minimum(rb + 2, n_blocks - 1), 0)
        softmax_pv(rb + 1, 1)
        return carry

    lax.fori_loop(0, n_blocks // 2, body, 0)


def _na(proj, rpb_flat):
    s = proj.shape[0]
    grid_rows = s // GRID_W
    n_blocks = grid_rows // NA_ROWS
    q_blk = (WIDTH_A + 2 * KV_WIDTH_A) // HEAD_DIM
    k_blk = q_blk + N_HEADS_B
    v_blk = k_blk + N_HEADS_B
    kern = functools.partial(_na_kernel, classes=_na_classes(grid_rows),
                             n_blocks=n_blocks)
    return pl.pallas_call(
        kern,
        grid=(N_HEADS_B,),
        in_specs=[
            pl.BlockSpec(memory_space=pltpu.SMEM),
            pl.BlockSpec((s, HEAD_DIM), lambda h: (0, q_blk + h)),
            pl.BlockSpec((s, HEAD_DIM), lambda h: (0, k_blk + h)),
            pl.BlockSpec((s, HEAD_DIM), lambda h: (0, v_blk + h)),
        ],
        out_specs=pl.BlockSpec((s, HEAD_DIM), lambda h: (0, h)),
        out_shape=jax.ShapeDtypeStruct((s, WIDTH_B), F32),
        scratch_shapes=[
            pltpu.VMEM((2 * WIN_H_MAX - 1, GRID_W, 2 * GRID_W), F32),
            pltpu.VMEM((3, NA_ROWS * GRID_W, NA_KROWS * GRID_W), F32),
            pltpu.VMEM((2, NA_ROWS * GRID_W, NA_KROWS * GRID_W), F32),
        ],
        compiler_params=_params("arbitrary"),
        name="na",
    )(rpb_flat, proj, proj, proj)


def _out_proj_kernel(oa_ref, ob_ref, ga_ref, gb_ref, w_ref, r_ref, o_ref, a_ref):
    @pl.when(pl.program_id(1) == 0)
    def _():
        _rms_norm_rows(oa_ref, ga_ref, a_ref, 0)
        _rms_norm_rows(ob_ref, gb_ref, a_ref, oa_ref.shape[1])

    acc = jnp.dot(a_ref[...], w_ref[...].astype(BF16),
                  preferred_element_type=F32)
    o_ref[...] = r_ref[...] + acc


def _out_proj(oa, ob, ga, gb, w, layer, x):
    s, wa = oa.shape
    wb = ob.shape[1]
    d = w.shape[2]
    return pl.pallas_call(
        _out_proj_kernel,
        grid=(s // TM, d // TN_PROJ),
        in_specs=[
            _single((TM, wa), lambda i, j: (i, 0)),
            _single((TM, wb), lambda i, j: (i, 0)),
            pl.BlockSpec((1, wa), lambda i, j: (0, 0)),
            pl.BlockSpec((1, wb), lambda i, j: (0, 0)),
            pl.BlockSpec((None, wa + wb, TN_PROJ), lambda i, j: (layer, 0, j)),
            pl.BlockSpec((TM, TN_PROJ), lambda i, j: (i, j)),
        ],
        out_specs=pl.BlockSpec((TM, TN_PROJ), lambda i, j: (i, j)),
        out_shape=jax.ShapeDtypeStruct((s, d), F32),
        scratch_shapes=[pltpu.VMEM((TM, wa + wb), BF16)],
        compiler_params=_params("parallel", "arbitrary"),
        name="out_proj",
    )(oa, ob, ga, gb, w, x)


def _mem_kv_kernel(m_ref, g_ref, w_ref, o_ref, a_ref):
    @pl.when(pl.program_id(0) == 0)
    def _():
        _rms_norm_rows(m_ref, g_ref, a_ref)

    o_ref[...] = jnp.dot(a_ref[...], w_ref[...],
                         preferred_element_type=F32).astype(o_ref.dtype)


def _mem_kv(mem, g, wkv):
    m, d = mem.shape
    n = wkv.shape[1]
    return pl.pallas_call(
        _mem_kv_kernel,
        grid=(n // TN_PROJ,),
        in_specs=[
            pl.BlockSpec((m, d), lambda j: (0, 0)),
            pl.BlockSpec((1, d), lambda j: (0, 0)),
            pl.BlockSpec((d, TN_PROJ), lambda j: (0, j)),
        ],
        out_specs=pl.BlockSpec((m, TN_PROJ), lambda j: (0, j)),
        out_shape=jax.ShapeDtypeStruct((m, n), BF16),
        scratch_shapes=[pltpu.VMEM((m, d), BF16)],
        compiler_params=_params("arbitrary"),
        name="mem_kv",
    )(mem, g, wkv)


def _xattn_kernel(x_ref, g_ref, wq_ref, kv_ref, wo_ref, o_ref, h_ref, oc_ref):
    _rms_norm_rows(x_ref, g_ref, h_ref)
    q = jnp.dot(h_ref[...], wq_ref[...], preferred_element_type=F32)
    for hd in range(N_HEADS_MEM):
        lo, hi = hd * HEAD_DIM, (hd + 1) * HEAD_DIM
        qh = q[:, lo:hi].astype(BF16)
        kh = kv_ref[:, lo:hi]
        vh = kv_ref[:, MEM_WIDTH + lo:MEM_WIDTH + hi]
        s = lax.dot_general(qh, kh, (((1,), (1,)), ((), ())),
                            preferred_element_type=F32) * ATTN_SCALE
        m = jnp.max(s, axis=-1, keepdims=True)
        p = jnp.exp(s - m)
        l = jnp.sum(p, axis=-1, keepdims=True)
        o = jnp.dot(p.astype(BF16), vh, preferred_element_type=F32) / l
        oc_ref[:, lo:hi] = o.astype(oc_ref.dtype)
    o_ref[...] = x_ref[...] + jnp.dot(oc_ref[...], wo_ref[...],
                                      preferred_element_type=F32)


def _xattn(x, g, wq, kv, wo):
    s, d = x.shape
    m = kv.shape[0]
    return pl.pallas_call(
        _xattn_kernel,
        grid=(s // TM_X,),
        in_specs=[
            pl.BlockSpec((TM_X, d), lambda i: (i, 0)),
            pl.BlockSpec((1, d), lambda i: (0, 0)),
            pl.BlockSpec((d, MEM_WIDTH), lambda i: (0, 0)),
            pl.BlockSpec((m, 2 * MEM_WIDTH), lambda i: (0, 0)),
            pl.BlockSpec((MEM_WIDTH, d), lambda i: (0, 0)),
        ],
        out_specs=pl.BlockSpec((TM_X, d), lambda i: (i, 0)),
        out_shape=jax.ShapeDtypeStruct((s, d), F32),
        scratch_shapes=[pltpu.VMEM((TM_X, d), BF16),
                        pltpu.VMEM((TM_X, MEM_WIDTH), BF16)],
        compiler_params=_params("parallel"),
        name="xattn",
    )(x, g, wq, kv, wo)


def _final_norm_kernel(x_ref, g_ref, o_ref):
    _rms_norm_rows(x_ref, g_ref, o_ref)


def _final_norm(x, g):
    s, d = x.shape
    return pl.pallas_call(
        _final_norm_kernel,
        grid=(s // TM_FINAL,),
        in_specs=[pl.BlockSpec((TM_FINAL, d), lambda i: (i, 0)),
                  pl.BlockSpec((1, d), lambda i: (0, 0))],
        out_specs=pl.BlockSpec((TM_FINAL, d), lambda i: (i, 0)),
        out_shape=jax.ShapeDtypeStruct((s, d), F32),
        compiler_params=_params("parallel"),
        name="final_norm",
    )(x, g)


def _rope_tables(seq_len):
    t = jnp.arange(seq_len)
    row = (t // GRID_W).astype(F32)
    col = (t % GRID_W).astype(F32)
    axis_dim = HEAD_DIM // 2
    inv_freq = ROPE_THETA ** (-jnp.arange(0, axis_dim, 2, dtype=F32) / axis_dim)
    ang_r = row[:, None] * inv_freq
    ang_c = col[:, None] * inv_freq
    cos = jnp.concatenate([jnp.cos(ang_r), jnp.cos(ang_r),
                           jnp.cos(ang_c), jnp.cos(ang_c)], axis=-1)
    sin = jnp.concatenate([-jnp.sin(ang_r), jnp.sin(ang_r),
                           -jnp.sin(ang_c), jnp.sin(ang_c)], axis=-1)
    return cos, sin


def kernel(x, mem, ffn1_norm, ffn1_w_gate, ffn1_w_up, ffn1_w_down, mix_norm, w_in, q_norm_a, k_norm_a, rpb_b, out_norm_a, out_norm_b, w_out, xattn_norm, mem_norm, xattn_wq, xattn_wkv, xattn_wo, ffn2_norm, ffn2_w_gate, ffn2_w_up, ffn2_w_down, final_norm):
    batch, seq_len, d_model = x.shape
    depth = w_in.shape[0]
    assert batch == 1 and mem.shape[0] == 1
    assert seq_len % GRID_W == 0 and (seq_len // GRID_W) % (2 * NA_ROWS) == 0
    assert seq_len % (2 * TK) == 0 and seq_len % TM == 0

    cos, sin = _rope_tables(seq_len)
    row = lambda v: v.reshape(1, -1)
    xs = x[0]
    mem2 = mem[0]
    for l in range(depth):
        bf = lambda w: w[l].astype(BF16)

        act = _ffn_up(xs, row(ffn1_norm[l]), ffn1_w_gate, ffn1_w_up, l)
        xs = _ffn_down(act, bf(ffn1_w_down), xs)

        proj = _in_proj(xs, row(mix_norm[l]), w_in, l, row(q_norm_a[l]),
                        row(k_norm_a[l]), cos, sin)
        oa = _gqa(proj)
        ob = _na(proj, rpb_b[l].reshape(-1))
        xs = _out_proj(oa, ob, row(out_norm_a[l]), row(out_norm_b[l]),
                       w_out, l, xs)

        kv = _mem_kv(mem2, row(mem_norm[l]), bf(xattn_wkv))
        xs = _xattn(xs, row(xattn_norm[l]), bf(xattn_wq), kv, bf(xattn_wo))

        act = _ffn_up(xs, row(ffn2_norm[l]), ffn2_w_gate, ffn2_w_up, l)
        xs = _ffn_down(act, bf(ffn2_w_down), xs)

    return _final_norm(xs, row(final_norm))[None]
```

```python
import functools

import jax
import jax.numpy as jnp
import numpy as np
from jax import lax
from jax.experimental import pallas as pl
from jax.experimental.pallas import tpu as pltpu

F32 = jnp.float32
BF16 = jnp.bfloat16

HEAD_DIM = 128
N_HEADS_A = 16
N_KV_A = 4
GQA_GROUP = N_HEADS_A // N_KV_A
N_HEADS_B = 16
WIDTH_A = N_HEADS_A * HEAD_DIM
WIDTH_B = N_HEADS_B * HEAD_DIM
KV_WIDTH_A = N_KV_A * HEAD_DIM
GRID_W = 64
WIN_H_MAX = 8
WIN_W = 16
N_HEADS_MEM = 4
MEM_WIDTH = N_HEADS_MEM * HEAD_DIM
ROPE_THETA = 10000.0
EPS = 1e-6
ATTN_SCALE = HEAD_DIM ** -0.5
LOG2_E = 1.4426950408889634
MASK_VALUE = -1e30

V7X_VMEM_BYTES = 64 * 1024 * 1024
VMEM_LIMIT_BYTES = V7X_VMEM_BYTES - 6 * 1024 * 1024

TM = 1024
TN_UP = 256
TN_DOWN = 256
KSPLIT_DOWN = 2
TN_PROJ = 512
NORM_CHUNK = 128
TQ = 256
TK = 512
GQA_UNROLL = 4
NA_ROWS = 4
NA_KROWS = 12
TM_X = 256
TM_FINAL = 512


def _params(*sem):
    return pltpu.CompilerParams(dimension_semantics=sem,
                                vmem_limit_bytes=VMEM_LIMIT_BYTES)


def _single(shape, index_map):
    return pl.BlockSpec(shape, index_map, pipeline_mode=pl.Buffered(1))


def _rms_norm_rows(x_ref, g_ref, a_ref, col_off=0):
    tm, d = x_ref.shape
    chunk = min(NORM_CHUNK, tm)

    def body(c, carry):
        r = pl.multiple_of(c * chunk, chunk)
        x = x_ref[pl.ds(r, chunk), :].astype(F32)
        ms = jnp.mean(x * x, axis=-1, keepdims=True)
        y = (x * lax.rsqrt(ms + EPS)) * g_ref[...]
        a_ref[pl.ds(r, chunk), col_off:col_off + d] = y.astype(a_ref.dtype)
        return carry

    lax.fori_loop(0, tm // chunk, body, 0)


def _ffn_up_kernel(x_ref, g_ref, wg_ref, wu_ref, o_ref, a_ref):
    @pl.when(pl.program_id(1) == 0)
    def _():
        _rms_norm_rows(x_ref, g_ref, a_ref)

    a = a_ref[...]
    gate = jnp.dot(a, wg_ref[...].astype(BF16), preferred_element_type=F32)
    up = jnp.dot(a, wu_ref[...].astype(BF16), preferred_element_type=F32)
    o_ref[...] = ((gate * jax.nn.sigmoid(gate)) * up).astype(o_ref.dtype)


def _ffn_up(x, g, wg, wu, layer):
    s, d = x.shape
    dff = wg.shape[2]
    return pl.pallas_call(
        _ffn_up_kernel,
        grid=(s // TM, dff // TN_UP),
        in_specs=[
            _single((TM, d), lambda i, j: (i, 0)),
            pl.BlockSpec((1, d), lambda i, j: (0, 0)),
            pl.BlockSpec((None, d, TN_UP), lambda i, j: (layer, 0, j)),
            pl.BlockSpec((None, d, TN_UP), lambda i, j: (layer, 0, j)),
        ],
        out_specs=pl.BlockSpec((TM, TN_UP), lambda i, j: (i, j)),
        out_shape=jax.ShapeDtypeStruct((s, dff), BF16),
        scratch_shapes=[pltpu.VMEM((TM, d), BF16)],
        compiler_params=_params("parallel", "arbitrary"),
        name="ffn_up",
    )(x, g, wg, wu)


def _ffn_down_kernel(a_ref, w_ref, r_ref, o_ref):
    tk = w_ref.shape[0]
    for k in range(KSPLIT_DOWN):
        @pl.when(pl.program_id(2) == k)
        def _(k=k):
            part = 0.5 * jnp.dot(a_ref[:, k * tk:(k + 1) * tk],
                                 w_ref[...].astype(BF16),
                                 preferred_element_type=F32)
            if k == 0:
                o_ref[...] = r_ref[...] + part
            else:
                o_ref[...] += part


def _ffn_down(act, wd, layer, x):
    s, dff = act.shape
    d = wd.shape[2]
    tk = dff // KSPLIT_DOWN
    return pl.pallas_call(
        _ffn_down_kernel,
        grid=(s // TM, d // TN_DOWN, KSPLIT_DOWN),
        in_specs=[
            _single((TM, dff), lambda i, j, k: (i, 0)),
            pl.BlockSpec((None, tk, TN_DOWN), lambda i, j, k: (layer, k, j)),
            pl.BlockSpec((TM, TN_DOWN), lambda i, j, k: (i, j)),
        ],
        out_specs=pl.BlockSpec((TM, TN_DOWN), lambda i, j, k: (i, j)),
        out_shape=jax.ShapeDtypeStruct((s, d), F32),
        compiler_params=_params("parallel", "arbitrary", "arbitrary"),
        name="ffn_down",
    )(act, wd, x)


def _head_pair_matrices():
    lane = np.arange(2 * HEAD_DIM)
    head = lane // HEAD_DIM
    ones = (head[:, None] == head[None, :]).astype(np.float32)
    first_half = (lane % (HEAD_DIM // 2)) < (HEAD_DIM // 4)
    partner = np.where(first_half, lane + HEAD_DIM // 4, lane - HEAD_DIM // 4)
    perm = np.zeros((2 * HEAD_DIM, 2 * HEAD_DIM), np.float32)
    perm[partner, lane] = 1.0
    return jnp.asarray(ones, BF16), jnp.asarray(perm, BF16)


def _in_proj_kernel(x_ref, g_ref, w_ref, qg_ref, kg_ref, cos_ref, sin_ref,
                    ones_ref, perm_ref, o_ref, a_ref):
    j = pl.program_id(1)
    n_q_tiles = WIDTH_A // TN_PROJ
    n_qk_tiles = (WIDTH_A + KV_WIDTH_A) // TN_PROJ

    @pl.when(j == 0)
    def _():
        _rms_norm_rows(x_ref, g_ref, a_ref)

    def project():
        return jnp.dot(a_ref[...], w_ref[...].astype(BF16),
                       preferred_element_type=F32)

    @pl.when(j < n_qk_tiles)
    def _():
        acc = project()
        is_q = j < n_q_tiles
        gain = jnp.where(is_q, qg_ref[...], kg_ref[...])
        gain = jnp.concatenate([gain, gain], axis=1)
        scale = jnp.where(is_q, ATTN_SCALE * LOG2_E, 1.0).astype(F32)
        cos = jnp.concatenate([cos_ref[...], cos_ref[...]], axis=1)
        sin = jnp.concatenate([sin_ref[...], sin_ref[...]], axis=1)
        pair = 2 * HEAD_DIM
        for hp in range(TN_PROJ // pair):
            xh = acc[:, hp * pair:(hp + 1) * pair]
            ss = jnp.dot((xh * xh).astype(BF16), ones_ref[...],
                         preferred_element_type=F32)
            y = (xh * lax.rsqrt(ss * (1.0 / HEAD_DIM) + EPS)) * gain
            partner = jnp.dot(y.astype(BF16), perm_ref[...],
                              preferred_element_type=F32)
            out = (y * cos + partner * sin) * scale
            o_ref[:, hp * pair:(hp + 1) * pair] = out.astype(o_ref.dtype)

    @pl.when(j >= n_qk_tiles)
    def _():
        o_ref[...] = project().astype(o_ref.dtype)


def _in_proj(x, g, w, layer, qg, kg, cos, sin):
    s, d = x.shape
    n = w.shape[2]
    ones, perm = _head_pair_matrices()
    pair = 2 * HEAD_DIM
    return pl.pallas_call(
        _in_proj_kernel,
        grid=(s // TM, n // TN_PROJ),
        in_specs=[
            _single((TM, d), lambda i, j: (i, 0)),
            pl.BlockSpec((1, d), lambda i, j: (0, 0)),
            pl.BlockSpec((None, d, TN_PROJ), lambda i, j: (layer, 0, j)),
            pl.BlockSpec((1, HEAD_DIM), lambda i, j: (0, 0)),
            pl.BlockSpec((1, HEAD_DIM), lambda i, j: (0, 0)),
            pl.BlockSpec((TM, HEAD_DIM), lambda i, j: (i, 0)),
            pl.BlockSpec((TM, HEAD_DIM), lambda i, j: (i, 0)),
            pl.BlockSpec((pair, pair), lambda i, j: (0, 0)),
            pl.BlockSpec((pair, pair), lambda i, j: (0, 0)),
        ],
        out_specs=pl.BlockSpec((TM, TN_PROJ), lambda i, j: (i, j)),
        out_shape=jax.ShapeDtypeStruct((s, n), BF16),
        scratch_shapes=[pltpu.VMEM((TM, d), BF16)],
        compiler_params=_params("parallel", "arbitrary"),
        name="in_proj",
    )(x, g, w, qg, kg, cos, sin, ones, perm)


def _gqa_kernel(q_ref, k_ref, v_ref, o_ref, qs_ref, vt_ref, m_ref, l_ref,
                acc_ref, st_ref):
    tq = q_ref.shape[0]
    s_len = k_ref.shape[0]
    n_chunks = s_len // TK

    @pl.when(pl.program_id(1) == 0)
    def _():
        def transpose_chunk(c, carry):
            r = pl.multiple_of(c * TK, TK)
            vc = v_ref[pl.ds(r, TK), :].astype(F32)
            vt_ref[:, pl.ds(r, TK)] = vc.T.astype(vt_ref.dtype)
            return carry

        lax.fori_loop(0, n_chunks, transpose_chunk, 0)

    for g in range(GQA_GROUP):
        qs_ref[g * tq:(g + 1) * tq, :] = q_ref[:, g * HEAD_DIM:(g + 1) * HEAD_DIM]
    m_ref[...] = jnp.full(m_ref.shape, MASK_VALUE, F32)
    l_ref[...] = jnp.zeros(l_ref.shape, F32)
    acc_ref[...] = jnp.zeros(acc_ref.shape, F32)

    def scores(c, slot):
        r = pl.multiple_of(c * TK, TK)
        st_ref[slot] = lax.dot_general(k_ref[pl.ds(r, TK), :], qs_ref[...],
                                       (((1,), (1,)), ((), ())),
                                       preferred_element_type=F32)

    def softmax_pv(c, slot):
        r = pl.multiple_of(c * TK, TK)
        for g in range(GQA_GROUP):
            cols = slice(g * tq, (g + 1) * tq)
            st = st_ref[slot, :, cols]
            m_old = m_ref[:, cols]
            m_new = jnp.maximum(m_old, jnp.max(st, axis=0, keepdims=True))
            alpha = jnp.exp2(m_old - m_new)
            p = jnp.exp2(st - m_new)
            l_ref[:, cols] = alpha * l_ref[:, cols] + jnp.sum(p, axis=0,
                                                              keepdims=True)
            pv = jnp.dot(vt_ref[:, pl.ds(r, TK)], p.astype(BF16),
                         preferred_element_type=F32)
            acc_ref[:, cols] = alpha * acc_ref[:, cols] + pv
            m_ref[:, cols] = m_new

    scores(0, 0)

    def body(i, carry):
        c = GQA_UNROLL * i
        for u in range(GQA_UNROLL):
            scores(jnp.minimum(c + u + 1, n_chunks - 1), (u + 1) % 2)
            softmax_pv(c + u, u % 2)
        return carry

    lax.fori_loop(0, n_chunks // GQA_UNROLL, body, 0)
    out = (acc_ref[...] / l_ref[...]).T
    for g in range(GQA_GROUP):
        o_ref[:, g * HEAD_DIM:(g + 1) * HEAD_DIM] = out[g * tq:(g + 1) * tq]


def _gqa(proj):
    s = proj.shape[0]
    k_blk = WIDTH_A // HEAD_DIM
    v_blk = (WIDTH_A + KV_WIDTH_A) // HEAD_DIM
    gw = GQA_GROUP * HEAD_DIM
    return pl.pallas_call(
        _gqa_kernel,
        grid=(N_KV_A, s // TQ),
        in_specs=[
            pl.BlockSpec((TQ, gw), lambda h, i: (i, h)),
            pl.BlockSpec((s, HEAD_DIM), lambda h, i: (0, k_blk + h)),
            pl.BlockSpec((s, HEAD_DIM), lambda h, i: (0, v_blk + h)),
        ],
        out_specs=pl.BlockSpec((TQ, gw), lambda h, i: (i, h)),
        out_shape=jax.ShapeDtypeStruct((s, WIDTH_A), F32),
        scratch_shapes=[
            pltpu.VMEM((GQA_GROUP * TQ, HEAD_DIM), BF16),
            pltpu.VMEM((HEAD_DIM, s), BF16),
            pltpu.VMEM((1, GQA_GROUP * TQ), F32),
            pltpu.VMEM((1, GQA_GROUP * TQ), F32),
            pltpu.VMEM((HEAD_DIM, GQA_GROUP * TQ), F32),
            pltpu.VMEM((2, TK, GQA_GROUP * TQ), F32),
        ],
        compiler_params=_params("arbitrary", "arbitrary"),
        name="gqa",
    )(proj, proj, proj)


def _na_classes(grid_rows):
    kh = min(WIN_H_MAX, grid_rows)
    n_blocks = grid_rows // NA_ROWS
    classes = []
    for rb in (0, 1, n_blocks - 1):
        start = min(max(NA_ROWS * rb - NA_ROWS, 0), grid_rows - NA_KROWS)
        table = []
        for qi in range(NA_ROWS):
            i = NA_ROWS * rb + qi
            rs = min(max(i - kh // 2, 0), grid_rows - kh)
            row = []
            for a in range(NA_KROWS):
                r = start + a
                row.append(r - i + (WIN_H_MAX - 1) if rs <= r < rs + kh else None)
            table.append(row)
        classes.append(table)
    return classes


def _na_kernel(rpb_ref, q_ref, k_ref, v_ref, o_ref, strip_ref, bias_ref, s_ref,
               *, classes, n_blocks):
    h = pl.program_id(0)
    n_rpb_rows = 2 * WIN_H_MAX - 1
    n_rpb_cols = 2 * WIN_W - 1
    qblk = NA_ROWS * GRID_W
    kblk = NA_KROWS * GRID_W

    jj = lax.broadcasted_iota(jnp.int32, (GRID_W, 2 * GRID_W), 0)
    cc = lax.broadcasted_iota(jnp.int32, (GRID_W, 2 * GRID_W), 1) % GRID_W
    rel = cc - jj + (WIN_W - 1)
    cs = jnp.clip(jj - WIN_W // 2, 0, GRID_W - WIN_W)
    col_ok = (cc >= cs) & (cc < cs + WIN_W)
    for dr in range(n_rpb_rows):
        base = (h * n_rpb_rows + dr) * n_rpb_cols

        def pick(d, t, base=base):
            return jnp.where(rel == d, rpb_ref[base + d], t)

        strip = lax.fori_loop(0, n_rpb_cols, pick,
                              jnp.zeros((GRID_W, 2 * GRID_W), F32))
        strip_ref[dr] = jnp.where(col_ok, strip, MASK_VALUE)

    left = lax.broadcasted_iota(jnp.int32, (GRID_W, 2 * GRID_W), 1) < GRID_W
    masked = jnp.full((GRID_W, 2 * GRID_W), MASK_VALUE, F32)
    for cls, table in enumerate(classes):
        for qi in range(NA_ROWS):
            for ap in range(NA_KROWS // 2):
                dl, dr_ = table[qi][2 * ap], table[qi][2 * ap + 1]
                lhs = masked if dl is None else strip_ref[dl]
                rhs = masked if dr_ is None else strip_ref[dr_]
                bias_ref[cls, qi * GRID_W:(qi + 1) * GRID_W,
                         ap * 2 * GRID_W:(ap + 1) * 2 * GRID_W] = (
                             jnp.where(left, lhs, rhs))

    def key_start(rb):
        sb = jnp.clip(rb - 1, 0, n_blocks - NA_KROWS // NA_ROWS)
        return pl.multiple_of(sb * qblk, qblk)

    def scores(rb, slot):
        q0 = pl.multiple_of(rb * qblk, qblk)
        s_ref[slot] = lax.dot_general(q_ref[pl.ds(q0, qblk), :],
                                      k_ref[pl.ds(key_start(rb), kblk), :],
                                      (((1,), (1,)), ((), ())),
                                      preferred_element_type=F32)

    def softmax_pv(rb, slot):
        cls = jnp.where(rb == 0, 0, jnp.where(rb == n_blocks - 1, 2, 1))
        q0 = pl.multiple_of(rb * qblk, qblk)
        s = s_ref[slot] * ATTN_SCALE + bias_ref[cls]
        m = jnp.max(s, axis=-1, keepdims=True)
        p = jnp.exp(s - m)
        l = jnp.sum(p, axis=-1, keepdims=True)
        o = jnp.dot(p.astype(BF16), v_ref[pl.ds(key_start(rb), kblk), :],
                    preferred_element_type=F32)
        o_ref[pl.ds(q0, qblk), :] = o / l

    scores(0, 0)

    def body(i, carry):
        rb = 2 * i
        scores(rb + 1, 1)
        softmax_pv(rb, 0)
        scores(jnp.minimum(rb + 2, n_blocks - 1), 0)
        softmax_pv(rb + 1, 1)
        return carry

    lax.fori_loop(0, n_blocks // 2, body, 0)


def _na(proj, rpb_flat):
    s = proj.shape[0]
    grid_rows = s // GRID_W
    n_blocks = grid_rows // NA_ROWS
    q_blk = (WIDTH_A + 2 * KV_WIDTH_A) // HEAD_DIM
    k_blk = q_blk + N_HEADS_B
    v_blk = k_blk + N_HEADS_B
    kern = functools.partial(_na_kernel, classes=_na_classes(grid_rows),
                             n_blocks=n_blocks)
    return pl.pallas_call(
        kern,
        grid=(N_HEADS_B,),
        in_specs=[
            pl.BlockSpec(memory_space=pltpu.SMEM),
            pl.BlockSpec((s, HEAD_DIM), lambda h: (0, q_blk + h)),
            pl.BlockSpec((s, HEAD_DIM), lambda h: (0, k_blk + h)),
            pl.BlockSpec((s, HEAD_DIM), lambda h: (0, v_blk + h)),
        ],
        out_specs=pl.BlockSpec((s, HEAD_DIM), lambda h: (0, h)),
        out_shape=jax.ShapeDtypeStruct((s, WIDTH_B), F32),
        scratch_shapes=[
            pltpu.VMEM((2 * WIN_H_MAX - 1, GRID_W, 2 * GRID_W), F32),
            pltpu.VMEM((3, NA_ROWS * GRID_W, NA_KROWS * GRID_W), F32),
            pltpu.VMEM((2, NA_ROWS * GRID_W, NA_KROWS * GRID_W), F32),
        ],
        compiler_params=_params("arbitrary"),
        name="na",
    )(rpb_flat, proj, proj, proj)


def _out_proj_kernel(oa_ref, ob_ref, ga_ref, gb_ref, w_ref, r_ref, o_ref, a_ref):
    @pl.when(pl.program_id(1) == 0)
    def _():
        _rms_norm_rows(oa_ref, ga_ref, a_ref, 0)
        _rms_norm_rows(ob_ref, gb_ref, a_ref, oa_ref.shape[1])

    acc = jnp.dot(a_ref[...], w_ref[...].astype(BF16),
                  preferred_element_type=F32)
    o_ref[...] = r_ref[...] + acc


def _out_proj(oa, ob, ga, gb, w, layer, x):
    s, wa = oa.shape
    wb = ob.shape[1]
    d = w.shape[2]
    return pl.pallas_call(
        _out_proj_kernel,
        grid=(s // TM, d // TN_PROJ),
        in_specs=[
            _single((TM, wa), lambda i, j: (i, 0)),
            _single((TM, wb), lambda i, j: (i, 0)),
            pl.BlockSpec((1, wa), lambda i, j: (0, 0)),
            pl.BlockSpec((1, wb), lambda i, j: (0, 0)),
            pl.BlockSpec((None, wa + wb, TN_PROJ), lambda i, j: (layer, 0, j)),
            pl.BlockSpec((TM, TN_PROJ), lambda i, j: (i, j)),
        ],
        out_specs=pl.BlockSpec((TM, TN_PROJ), lambda i, j: (i, j)),
        out_shape=jax.ShapeDtypeStruct((s, d), F32),
        scratch_shapes=[pltpu.VMEM((TM, wa + wb), BF16)],
        compiler_params=_params("parallel", "arbitrary"),
        name="out_proj",
    )(oa, ob, ga, gb, w, x)


def _mem_kv_kernel(m_ref, g_ref, w_ref, o_ref, a_ref):
    @pl.when(pl.program_id(0) == 0)
    def _():
        _rms_norm_rows(m_ref, g_ref, a_ref)

    o_ref[...] = jnp.dot(a_ref[...], w_ref[...],
                         preferred_element_type=F32).astype(o_ref.dtype)


def _mem_kv(mem, g, wkv):
    m, d = mem.shape
    n = wkv.shape[1]
    return pl.pallas_call(
        _mem_kv_kernel,
        grid=(n // TN_PROJ,),
        in_specs=[
            pl.BlockSpec((m, d), lambda j: (0, 0)),
            pl.BlockSpec((1, d), lambda j: (0, 0)),
            pl.BlockSpec((d, TN_PROJ), lambda j: (0, j)),
        ],
        out_specs=pl.BlockSpec((m, TN_PROJ), lambda j: (0, j)),
        out_shape=jax.ShapeDtypeStruct((m, n), BF16),
        scratch_shapes=[pltpu.VMEM((m, d), BF16)],
        compiler_params=_params("arbitrary"),
        name="mem_kv",
    )(mem, g, wkv)


def _xattn_kernel(x_ref, g_ref, wq_ref, kv_ref, wo_ref, o_ref, h_ref, oc_ref):
    _rms_norm_rows(x_ref, g_ref, h_ref)
    q = jnp.dot(h_ref[...], wq_ref[...], preferred_element_type=F32)
    for hd in range(N_HEADS_MEM):
        lo, hi = hd * HEAD_DIM, (hd + 1) * HEAD_DIM
        qh = q[:, lo:hi].astype(BF16)
        kh = kv_ref[:, lo:hi]
        vh = kv_ref[:, MEM_WIDTH + lo:MEM_WIDTH + hi]
        s = lax.dot_general(qh, kh, (((1,), (1,)), ((), ())),
                            preferred_element_type=F32) * ATTN_SCALE
        m = jnp.max(s, axis=-1, keepdims=True)
        p = jnp.exp(s - m)
        l = jnp.sum(p, axis=-1, keepdims=True)
        o = jnp.dot(p.astype(BF16), vh, preferred_element_type=F32) / l
        oc_ref[:, lo:hi] = o.astype(oc_ref.dtype)
    o_ref[...] = x_ref[...] + jnp.dot(oc_ref[...], wo_ref[...],
                                      preferred_element_type=F32)


def _xattn(x, g, wq, kv, wo):
    s, d = x.shape
    m = kv.shape[0]
    return pl.pallas_call(
        _xattn_kernel,
        grid=(s // TM_X,),
        in_specs=[
            pl.BlockSpec((TM_X, d), lambda i: (i, 0)),
            pl.BlockSpec((1, d), lambda i: (0, 0)),
            pl.BlockSpec((d, MEM_WIDTH), lambda i: (0, 0)),
            pl.BlockSpec((m, 2 * MEM_WIDTH), lambda i: (0, 0)),
            pl.BlockSpec((MEM_WIDTH, d), lambda i: (0, 0)),
        ],
        out_specs=pl.BlockSpec((TM_X, d), lambda i: (i, 0)),
        out_shape=jax.ShapeDtypeStruct((s, d), F32),
        scratch_shapes=[pltpu.VMEM((TM_X, d), BF16),
                        pltpu.VMEM((TM_X, MEM_WIDTH), BF16)],
        compiler_params=_params("parallel"),
        name="xattn",
    )(x, g, wq, kv, wo)


def _final_norm_kernel(x_ref, g_ref, o_ref):
    _rms_norm_rows(x_ref, g_ref, o_ref)


def _final_norm(x, g):
    s, d = x.shape
    return pl.pallas_call(
        _final_norm_kernel,
        grid=(s // TM_FINAL,),
        in_specs=[pl.BlockSpec((TM_FINAL, d), lambda i: (i, 0)),
                  pl.BlockSpec((1, d), lambda i: (0, 0))],
        out_specs=pl.BlockSpec((TM_FINAL, d), lambda i: (i, 0)),
        out_shape=jax.ShapeDtypeStruct((s, d), F32),
        compiler_params=_params("parallel"),
        name="final_norm",
    )(x, g)


def _rope_tables(seq_len):
    t = jnp.arange(seq_len)
    row = (t // GRID_W).astype(F32)
    col = (t % GRID_W).astype(F32)
    axis_dim = HEAD_DIM // 2
    inv_freq = ROPE_THETA ** (-jnp.arange(0, axis_dim, 2, dtype=F32) / axis_dim)
    ang_r = row[:, None] * inv_freq
    ang_c = col[:, None] * inv_freq
    cos = jnp.concatenate([jnp.cos(ang_r), jnp.cos(ang_r),
                           jnp.cos(ang_c), jnp.cos(ang_c)], axis=-1)
    sin = jnp.concatenate([-jnp.sin(ang_r), jnp.sin(ang_r),
                           -jnp.sin(ang_c), jnp.sin(ang_c)], axis=-1)
    return cos, sin


def kernel(x, mem, ffn1_norm, ffn1_w_gate, ffn1_w_up, ffn1_w_down, mix_norm, w_in, q_norm_a, k_norm_a, rpb_b, out_norm_a, out_norm_b, w_out, xattn_norm, mem_norm, xattn_wq, xattn_wkv, xattn_wo, ffn2_norm, ffn2_w_gate, ffn2_w_up, ffn2_w_down, final_norm):
    batch, seq_len, d_model = x.shape
    depth = w_in.shape[0]
    assert batch == 1 and mem.shape[0] == 1
    assert seq_len % GRID_W == 0 and (seq_len // GRID_W) % (2 * NA_ROWS) == 0
    assert seq_len % (GQA_UNROLL * TK) == 0 and seq_len % TM == 0

    cos, sin = _rope_tables(seq_len)
    row = lambda v: v.reshape(1, -1)
    xs = x[0]
    mem2 = mem[0]
    for l in range(depth):
        bf = lambda w: w[l].astype(BF16)

        act = _ffn_up(xs, row(ffn1_norm[l]), ffn1_w_gate, ffn1_w_up, l)
        xs = _ffn_down(act, ffn1_w_down, l, xs)

        proj = _in_proj(xs, row(mix_norm[l]), w_in, l, row(q_norm_a[l]),
                        row(k_norm_a[l]), cos, sin)
        oa = _gqa(proj)
        ob = _na(proj, rpb_b[l].reshape(-1))
        xs = _out_proj(oa, ob, row(out_norm_a[l]), row(out_norm_b[l]),
                       w_out, l, xs)

        kv = _mem_kv(mem2, row(mem_norm[l]), bf(xattn_wkv))
        xs = _xattn(xs, row(xattn_norm[l]), bf(xattn_wq), kv, bf(xattn_wo))

        act = _ffn_up(xs, row(ffn2_norm[l]), ffn2_w_gate, ffn2_w_up, l)
        xs = _ffn_down(act, ffn2_w_down, l, xs)

    return _final_norm(xs, row(final_norm))[None]
```

```python
import functools

import jax
import jax.numpy as jnp
import numpy as np
from jax import lax
from jax.experimental import pallas as pl
from jax.experimental.pallas import tpu as pltpu

F32 = jnp.float32
BF16 = jnp.bfloat16

HEAD_DIM = 128
N_HEADS_A = 16
N_KV_A = 4
GQA_GROUP = N_HEADS_A // N_KV_A
N_HEADS_B = 16
WIDTH_A = N_HEADS_A * HEAD_DIM
WIDTH_B = N_HEADS_B * HEAD_DIM
KV_WIDTH_A = N_KV_A * HEAD_DIM
GRID_W = 64
WIN_H_MAX = 8
WIN_W = 16
N_HEADS_MEM = 4
MEM_WIDTH = N_HEADS_MEM * HEAD_DIM
ROPE_THETA = 10000.0
EPS = 1e-6
ATTN_SCALE = HEAD_DIM ** -0.5
LOG2_E = 1.4426950408889634
MASK_VALUE = -1e30

V7X_VMEM_BYTES = 64 * 1024 * 1024
VMEM_LIMIT_BYTES = V7X_VMEM_BYTES - 6 * 1024 * 1024

TM = 1024
TN_UP = 256
TN_DOWN = 256
KSPLIT_DOWN = 2
TN_PROJ = 512
NORM_COLS = 1024
NORM_CHUNK = 256
TQ = 256
TK = 512
GQA_UNROLL = 4
NA_ROWS = 4
NA_KROWS = 12
NA_UNROLL = 4
TM_X = 256
TM_FINAL = 512


def _params(*sem):
    return pltpu.CompilerParams(dimension_semantics=sem,
                                vmem_limit_bytes=VMEM_LIMIT_BYTES)


def _single(shape, index_map):
    return pl.BlockSpec(shape, index_map, pipeline_mode=pl.Buffered(1))


def _rms_norm_rows(x_ref, g_ref, a_ref, col_off=0):
    tm, d = x_ref.shape
    chunk = min(NORM_CHUNK, tm)

    cb = min(NORM_COLS, d)

    def body(c, carry):
        r = pl.multiple_of(c * chunk, chunk)
        ss = jnp.zeros((chunk, 1), F32)
        for c0 in range(0, d, cb):
            xb = x_ref[pl.ds(r, chunk), c0:c0 + cb].astype(F32)
            ss = ss + jnp.sum(xb * xb, axis=-1, keepdims=True)
        inv = lax.rsqrt(ss * (1.0 / d) + EPS)
        for c0 in range(0, d, cb):
            xb = x_ref[pl.ds(r, chunk), c0:c0 + cb].astype(F32)
            y = (xb * inv) * g_ref[:, c0:c0 + cb]
            a_ref[pl.ds(r, chunk),
                  col_off + c0:col_off + c0 + cb] = y.astype(a_ref.dtype)
        return carry

    lax.fori_loop(0, tm // chunk, body, 0)


def _row_tile_copy(x_hbm, x_buf, sem, tile):
    rows = x_buf.shape[0]
    r = pl.multiple_of(tile * rows, rows)
    return pltpu.make_async_copy(x_hbm.at[pl.ds(r, rows), :], x_buf, sem)


def _prefetched_norm_prologue(sources, a_ref):
    i = pl.program_id(0)
    j = pl.program_id(1)

    @pl.when(j == 0)
    def _():
        @pl.when(i == 0)
        def _():
            for x_hbm, x_buf, sem, _, _ in sources:
                _row_tile_copy(x_hbm, x_buf, sem, 0).start()

        for x_hbm, x_buf, sem, g_ref, col_off in sources:
            _row_tile_copy(x_hbm, x_buf, sem, i).wait()
            _rms_norm_rows(x_buf, g_ref, a_ref, col_off)

    @pl.when((j == 1) & (i + 1 < pl.num_programs(0)))
    def _():
        for x_hbm, x_buf, sem, _, _ in sources:
            _row_tile_copy(x_hbm, x_buf, sem, i + 1).start()


def _ffn_up_kernel(x_hbm, g_ref, wg_ref, wu_ref, o_ref, x_buf, a_ref, sem):
    _prefetched_norm_prologue([(x_hbm, x_buf, sem.at[0], g_ref, 0)], a_ref)
    a = a_ref[...]
    gate = jnp.dot(a, wg_ref[...].astype(BF16), preferred_element_type=F32)
    up = jnp.dot(a, wu_ref[...].astype(BF16), preferred_element_type=F32)
    o_ref[...] = ((gate * jax.nn.sigmoid(gate)) * up).astype(o_ref.dtype)


def _ffn_up(x, g, wg, wu, layer):
    s, d = x.shape
    dff = wg.shape[2]
    return pl.pallas_call(
        _ffn_up_kernel,
        grid=(s // TM, dff // TN_UP),
        in_specs=[
            pl.BlockSpec(memory_space=pl.ANY),
            pl.BlockSpec((1, d), lambda i, j: (0, 0)),
            pl.BlockSpec((None, d, TN_UP), lambda i, j: (layer, 0, j)),
            pl.BlockSpec((None, d, TN_UP), lambda i, j: (layer, 0, j)),
        ],
        out_specs=pl.BlockSpec((TM, TN_UP), lambda i, j: (i, j)),
        out_shape=jax.ShapeDtypeStruct((s, dff), BF16),
        scratch_shapes=[pltpu.VMEM((TM, d), F32), pltpu.VMEM((TM, d), BF16),
                        pltpu.SemaphoreType.DMA((1,))],
        compiler_params=_params("arbitrary", "arbitrary"),
        name="ffn_up",
    )(x, g, wg, wu)


def _ffn_down_kernel(a_ref, w_ref, r_ref, o_ref):
    tk = w_ref.shape[0]
    for k in range(KSPLIT_DOWN):
        @pl.when(pl.program_id(2) == k)
        def _(k=k):
            part = 0.5 * jnp.dot(a_ref[:, k * tk:(k + 1) * tk],
                                 w_ref[...].astype(BF16),
                                 preferred_element_type=F32)
            if k == 0:
                o_ref[...] = r_ref[...] + part
            else:
                o_ref[...] += part


def _ffn_down(act, wd, layer, x):
    s, dff = act.shape
    d = wd.shape[2]
    tk = dff // KSPLIT_DOWN
    return pl.pallas_call(
        _ffn_down_kernel,
        grid=(s // TM, d // TN_DOWN, KSPLIT_DOWN),
        in_specs=[
            _single((TM, dff), lambda i, j, k: (i, 0)),
            pl.BlockSpec((None, tk, TN_DOWN), lambda i, j, k: (layer, k, j)),
            pl.BlockSpec((TM, TN_DOWN), lambda i, j, k: (i, j)),
        ],
        out_specs=pl.BlockSpec((TM, TN_DOWN), lambda i, j, k: (i, j)),
        out_shape=jax.ShapeDtypeStruct((s, d), F32),
        compiler_params=_params("parallel", "arbitrary", "arbitrary"),
        name="ffn_down",
    )(act, wd, x)


def _head_pair_matrices():
    lane = np.arange(2 * HEAD_DIM)
    head = lane // HEAD_DIM
    ones = (head[:, None] == head[None, :]).astype(np.float32)
    first_half = (lane % (HEAD_DIM // 2)) < (HEAD_DIM // 4)
    partner = np.where(first_half, lane + HEAD_DIM // 4, lane - HEAD_DIM // 4)
    perm = np.zeros((2 * HEAD_DIM, 2 * HEAD_DIM), np.float32)
    perm[partner, lane] = 1.0
    return jnp.asarray(ones, BF16), jnp.asarray(perm, BF16)


def _in_proj_kernel(x_hbm, g_ref, w_ref, qg_ref, kg_ref, cos_ref, sin_ref,
                    ones_ref, perm_ref, o_ref, x_buf, a_ref, sem):
    j = pl.program_id(1)
    n_q_tiles = WIDTH_A // TN_PROJ
    n_qk_tiles = (WIDTH_A + KV_WIDTH_A) // TN_PROJ
    _prefetched_norm_prologue([(x_hbm, x_buf, sem.at[0], g_ref, 0)], a_ref)

    def project():
        return jnp.dot(a_ref[...], w_ref[...].astype(BF16),
                       preferred_element_type=F32)

    @pl.when(j < n_qk_tiles)
    def _():
        acc = project()
        is_q = j < n_q_tiles
        gain = jnp.where(is_q, qg_ref[...], kg_ref[...])
        gain = jnp.concatenate([gain, gain], axis=1)
        scale = jnp.where(is_q, ATTN_SCALE * LOG2_E, 1.0).astype(F32)
        cos = jnp.concatenate([cos_ref[...], cos_ref[...]], axis=1)
        sin = jnp.concatenate([sin_ref[...], sin_ref[...]], axis=1)
        pair = 2 * HEAD_DIM
        for hp in range(TN_PROJ // pair):
            xh = acc[:, hp * pair:(hp + 1) * pair]
            ss = jnp.dot((xh * xh).astype(BF16), ones_ref[...],
                         preferred_element_type=F32)
            y = (xh * lax.rsqrt(ss * (1.0 / HEAD_DIM) + EPS)) * gain
            partner = jnp.dot(y.astype(BF16), perm_ref[...],
                              preferred_element_type=F32)
            out = (y * cos + partner * sin) * scale
            o_ref[:, hp * pair:(hp + 1) * pair] = out.astype(o_ref.dtype)

    @pl.when(j >= n_qk_tiles)
    def _():
        o_ref[...] = project().astype(o_ref.dtype)


def _in_proj(x, g, w, layer, qg, kg, cos, sin):
    s, d = x.shape
    n = w.shape[2]
    ones, perm = _head_pair_matrices()
    pair = 2 * HEAD_DIM
    return pl.pallas_call(
        _in_proj_kernel,
        grid=(s // TM, n // TN_PROJ),
        in_specs=[
            pl.BlockSpec(memory_space=pl.ANY),
            pl.BlockSpec((1, d), lambda i, j: (0, 0)),
            pl.BlockSpec((None, d, TN_PROJ), lambda i, j: (layer, 0, j)),
            pl.BlockSpec((1, HEAD_DIM), lambda i, j: (0, 0)),
            pl.BlockSpec((1, HEAD_DIM), lambda i, j: (0, 0)),
            pl.BlockSpec((TM, HEAD_DIM), lambda i, j: (i, 0)),
            pl.BlockSpec((TM, HEAD_DIM), lambda i, j: (i, 0)),
            pl.BlockSpec((pair, pair), lambda i, j: (0, 0)),
            pl.BlockSpec((pair, pair), lambda i, j: (0, 0)),
        ],
        out_specs=pl.BlockSpec((TM, TN_PROJ), lambda i, j: (i, j)),
        out_shape=jax.ShapeDtypeStruct((s, n), BF16),
        scratch_shapes=[pltpu.VMEM((TM, d), F32), pltpu.VMEM((TM, d), BF16),
                        pltpu.SemaphoreType.DMA((1,))],
        compiler_params=_params("arbitrary", "arbitrary"),
        name="in_proj",
    )(x, g, w, qg, kg, cos, sin, ones, perm)


def _gqa_kernel(q_ref, k_ref, v_ref, o_ref, qs_ref, vt_ref, m_ref, l_ref,
                acc_ref, st_ref):
    tq = q_ref.shape[0]
    s_len = k_ref.shape[0]
    n_chunks = s_len // TK

    @pl.when(pl.program_id(1) == 0)
    def _():
        def transpose_chunk(c, carry):
            r = pl.multiple_of(c * TK, TK)
            vc = v_ref[pl.ds(r, TK), :].astype(F32)
            vt_ref[:, pl.ds(r, TK)] = vc.T.astype(vt_ref.dtype)
            return carry

        lax.fori_loop(0, n_chunks, transpose_chunk, 0)

    for g in range(GQA_GROUP):
        qs_ref[g * tq:(g + 1) * tq, :] = q_ref[:, g * HEAD_DIM:(g + 1) * HEAD_DIM]
    m_ref[...] = jnp.full(m_ref.shape, MASK_VALUE, F32)
    l_ref[...] = jnp.zeros(l_ref.shape, F32)
    acc_ref[...] = jnp.zeros(acc_ref.shape, F32)

    def scores(c, slot):
        r = pl.multiple_of(c * TK, TK)
        st_ref[slot] = lax.dot_general(k_ref[pl.ds(r, TK), :], qs_ref[...],
                                       (((1,), (1,)), ((), ())),
                                       preferred_element_type=F32)

    def softmax_pv(c, slot):
        r = pl.multiple_of(c * TK, TK)
        for g in range(GQA_GROUP):
            cols = slice(g * tq, (g + 1) * tq)
            st = st_ref[slot, :, cols]
            m_old = m_ref[:, cols]
            m_new = jnp.maximum(m_old, jnp.max(st, axis=0, keepdims=True))
            alpha = jnp.exp2(m_old - m_new)
            p = jnp.exp2(st - m_new)
            l_ref[:, cols] = alpha * l_ref[:, cols] + jnp.sum(p, axis=0,
                                                              keepdims=True)
            pv = jnp.dot(vt_ref[:, pl.ds(r, TK)], p.astype(BF16),
                         preferred_element_type=F32)
            acc_ref[:, cols] = alpha * acc_ref[:, cols] + pv
            m_ref[:, cols] = m_new

    scores(0, 0)

    def body(i, carry):
        c = GQA_UNROLL * i
        for u in range(GQA_UNROLL):
            scores(jnp.minimum(c + u + 1, n_chunks - 1), (u + 1) % 2)
            softmax_pv(c + u, u % 2)
        return carry

    lax.fori_loop(0, n_chunks // GQA_UNROLL, body, 0)
    out = (acc_ref[...] / l_ref[...]).T
    for g in range(GQA_GROUP):
        o_ref[:, g * HEAD_DIM:(g + 1) * HEAD_DIM] = out[g * tq:(g + 1) * tq]


def _gqa(proj):
    s = proj.shape[0]
    k_blk = WIDTH_A // HEAD_DIM
    v_blk = (WIDTH_A + KV_WIDTH_A) // HEAD_DIM
    gw = GQA_GROUP * HEAD_DIM
    return pl.pallas_call(
        _gqa_kernel,
        grid=(N_KV_A, s // TQ),
        in_specs=[
            pl.BlockSpec((TQ, gw), lambda h, i: (i, h)),
            pl.BlockSpec((s, HEAD_DIM), lambda h, i: (0, k_blk + h)),
            pl.BlockSpec((s, HEAD_DIM), lambda h, i: (0, v_blk + h)),
        ],
        out_specs=pl.BlockSpec((TQ, gw), lambda h, i: (i, h)),
        out_shape=jax.ShapeDtypeStruct((s, WIDTH_A), F32),
        scratch_shapes=[
            pltpu.VMEM((GQA_GROUP * TQ, HEAD_DIM), BF16),
            pltpu.VMEM((HEAD_DIM, s), BF16),
            pltpu.VMEM((1, GQA_GROUP * TQ), F32),
            pltpu.VMEM((1, GQA_GROUP * TQ), F32),
            pltpu.VMEM((HEAD_DIM, GQA_GROUP * TQ), F32),
            pltpu.VMEM((2, TK, GQA_GROUP * TQ), F32),
        ],
        compiler_params=_params("arbitrary", "arbitrary"),
        name="gqa",
    )(proj, proj, proj)


def _na_classes(grid_rows):
    kh = min(WIN_H_MAX, grid_rows)
    n_blocks = grid_rows // NA_ROWS
    classes = []
    for rb in (0, 1, n_blocks - 1):
        start = min(max(NA_ROWS * rb - NA_ROWS, 0), grid_rows - NA_KROWS)
        table = []
        for qi in range(NA_ROWS):
            i = NA_ROWS * rb + qi
            rs = min(max(i - kh // 2, 0), grid_rows - kh)
            row = []
            for a in range(NA_KROWS):
                r = start + a
                row.append(r - i + (WIN_H_MAX - 1) if rs <= r < rs + kh else None)
            table.append(row)
        classes.append(table)
    return classes


def _na_kernel(rpb_ref, q_ref, k_ref, v_ref, o_ref, strip_ref, bias_ref, s_ref,
               *, classes, n_blocks):
    h = pl.program_id(0)
    n_rpb_rows = 2 * WIN_H_MAX - 1
    n_rpb_cols = 2 * WIN_W - 1
    qblk = NA_ROWS * GRID_W
    kblk = NA_KROWS * GRID_W

    jj = lax.broadcasted_iota(jnp.int32, (GRID_W, 2 * GRID_W), 0)
    cc = lax.broadcasted_iota(jnp.int32, (GRID_W, 2 * GRID_W), 1) % GRID_W
    rel = cc - jj + (WIN_W - 1)
    cs = jnp.clip(jj - WIN_W // 2, 0, GRID_W - WIN_W)
    col_ok = (cc >= cs) & (cc < cs + WIN_W)
    for dr in range(n_rpb_rows):
        base = (h * n_rpb_rows + dr) * n_rpb_cols

        def pick(d, t, base=base):
            return jnp.where(rel == d, rpb_ref[base + d], t)

        strip = lax.fori_loop(0, n_rpb_cols, pick,
                              jnp.zeros((GRID_W, 2 * GRID_W), F32))
        strip_ref[dr] = jnp.where(col_ok, strip, MASK_VALUE)

    left = lax.broadcasted_iota(jnp.int32, (GRID_W, 2 * GRID_W), 1) < GRID_W
    masked = jnp.full((GRID_W, 2 * GRID_W), MASK_VALUE, F32)
    for cls, table in enumerate(classes):
        for qi in range(NA_ROWS):
            for ap in range(NA_KROWS // 2):
                dl, dr_ = table[qi][2 * ap], table[qi][2 * ap + 1]
                lhs = masked if dl is None else strip_ref[dl]
                rhs = masked if dr_ is None else strip_ref[dr_]
                bias_ref[cls, qi * GRID_W:(qi + 1) * GRID_W,
                         ap * 2 * GRID_W:(ap + 1) * 2 * GRID_W] = (
                             jnp.where(left, lhs, rhs))

    def key_start(rb):
        sb = jnp.clip(rb - 1, 0, n_blocks - NA_KROWS // NA_ROWS)
        return pl.multiple_of(sb * qblk, qblk)

    def scores(rb, slot):
        q0 = pl.multiple_of(rb * qblk, qblk)
        s_ref[slot] = lax.dot_general(q_ref[pl.ds(q0, qblk), :],
                                      k_ref[pl.ds(key_start(rb), kblk), :],
                                      (((1,), (1,)), ((), ())),
                                      preferred_element_type=F32)

    def softmax_pv(rb, slot):
        cls = jnp.where(rb == 0, 0, jnp.where(rb == n_blocks - 1, 2, 1))
        q0 = pl.multiple_of(rb * qblk, qblk)
        s = s_ref[slot] * ATTN_SCALE + bias_ref[cls]
        m = jnp.max(s, axis=-1, keepdims=True)
        p = jnp.exp(s - m)
        l = jnp.sum(p, axis=-1, keepdims=True)
        o = jnp.dot(p.astype(BF16), v_ref[pl.ds(key_start(rb), kblk), :],
                    preferred_element_type=F32)
        o_ref[pl.ds(q0, qblk), :] = o / l

    scores(0, 0)

    def body(i, carry):
        rb = NA_UNROLL * i
        for u in range(NA_UNROLL):
            scores(jnp.minimum(rb + u + 1, n_blocks - 1), (u + 1) % 2)
            softmax_pv(rb + u, u % 2)
        return carry

    lax.fori_loop(0, n_blocks // NA_UNROLL, body, 0)


def _na(proj, rpb_flat):
    s = proj.shape[0]
    grid_rows = s // GRID_W
    n_blocks = grid_rows // NA_ROWS
    q_blk = (WIDTH_A + 2 * KV_WIDTH_A) // HEAD_DIM
    k_blk = q_blk + N_HEADS_B
    v_blk = k_blk + N_HEADS_B
    kern = functools.partial(_na_kernel, classes=_na_classes(grid_rows),
                             n_blocks=n_blocks)
    return pl.pallas_call(
        kern,
        grid=(N_HEADS_B,),
        in_specs=[
            pl.BlockSpec(memory_space=pltpu.SMEM),
            pl.BlockSpec((s, HEAD_DIM), lambda h: (0, q_blk + h)),
            pl.BlockSpec((s, HEAD_DIM), lambda h: (0, k_blk + h)),
            pl.BlockSpec((s, HEAD_DIM), lambda h: (0, v_blk + h)),
        ],
        out_specs=pl.BlockSpec((s, HEAD_DIM), lambda h: (0, h)),
        out_shape=jax.ShapeDtypeStruct((s, WIDTH_B), F32),
        scratch_shapes=[
            pltpu.VMEM((2 * WIN_H_MAX - 1, GRID_W, 2 * GRID_W), F32),
            pltpu.VMEM((3, NA_ROWS * GRID_W, NA_KROWS * GRID_W), F32),
            pltpu.VMEM((2, NA_ROWS * GRID_W, NA_KROWS * GRID_W), F32),
        ],
        compiler_params=_params("arbitrary"),
        name="na",
    )(rpb_flat, proj, proj, proj)


def _out_proj_kernel(oa_hbm, ob_hbm, ga_ref, gb_ref, w_ref, r_ref, o_ref,
                     oa_buf, ob_buf, a_ref, sem):
    _prefetched_norm_prologue(
        [(oa_hbm, oa_buf, sem.at[0], ga_ref, 0),
         (ob_hbm, ob_buf, sem.at[1], gb_ref, oa_buf.shape[1])], a_ref)

    acc = jnp.dot(a_ref[...], w_ref[...].astype(BF16),
                  preferred_element_type=F32)
    o_ref[...] = r_ref[...] + acc


def _out_proj(oa, ob, ga, gb, w, layer, x):
    s, wa = oa.shape
    wb = ob.shape[1]
    d = w.shape[2]
    return pl.pallas_call(
        _out_proj_kernel,
        grid=(s // TM, d // TN_PROJ),
        in_specs=[
            pl.BlockSpec(memory_space=pl.ANY),
            pl.BlockSpec(memory_space=pl.ANY),
            pl.BlockSpec((1, wa), lambda i, j: (0, 0)),
            pl.BlockSpec((1, wb), lambda i, j: (0, 0)),
            pl.BlockSpec((None, wa + wb, TN_PROJ), lambda i, j: (layer, 0, j)),
            pl.BlockSpec((TM, TN_PROJ), lambda i, j: (i, j)),
        ],
        out_specs=pl.BlockSpec((TM, TN_PROJ), lambda i, j: (i, j)),
        out_shape=jax.ShapeDtypeStruct((s, d), F32),
        scratch_shapes=[pltpu.VMEM((TM, wa), F32), pltpu.VMEM((TM, wb), F32),
                        pltpu.VMEM((TM, wa + wb), BF16),
                        pltpu.SemaphoreType.DMA((2,))],
        compiler_params=_params("arbitrary", "arbitrary"),
        name="out_proj",
    )(oa, ob, ga, gb, w, x)


def _mem_kv_kernel(m_ref, g_ref, w_ref, o_ref, a_ref):
    @pl.when(pl.program_id(0) == 0)
    def _():
        _rms_norm_rows(m_ref, g_ref, a_ref)

    o_ref[...] = jnp.dot(a_ref[...], w_ref[...],
                         preferred_element_type=F32).astype(o_ref.dtype)


def _mem_kv(mem, g, wkv):
    m, d = mem.shape
    n = wkv.shape[1]
    return pl.pallas_call(
        _mem_kv_kernel,
        grid=(n // TN_PROJ,),
        in_specs=[
            pl.BlockSpec((m, d), lambda j: (0, 0)),
            pl.BlockSpec((1, d), lambda j: (0, 0)),
            pl.BlockSpec((d, TN_PROJ), lambda j: (0, j)),
        ],
        out_specs=pl.BlockSpec((m, TN_PROJ), lambda j: (0, j)),
        out_shape=jax.ShapeDtypeStruct((m, n), BF16),
        scratch_shapes=[pltpu.VMEM((m, d), BF16)],
        compiler_params=_params("arbitrary"),
        name="mem_kv",
    )(mem, g, wkv)


def _xattn_kernel(x_ref, g_ref, wq_ref, kv_ref, wo_ref, o_ref, h_ref, oc_ref):
    _rms_norm_rows(x_ref, g_ref, h_ref)
    q = jnp.dot(h_ref[...], wq_ref[...], preferred_element_type=F32)
    for hd in range(N_HEADS_MEM):
        lo, hi = hd * HEAD_DIM, (hd + 1) * HEAD_DIM
        qh = q[:, lo:hi].astype(BF16)
        kh = kv_ref[:, lo:hi]
        vh = kv_ref[:, MEM_WIDTH + lo:MEM_WIDTH + hi]
        s = lax.dot_general(qh, kh, (((1,), (1,)), ((), ())),
                            preferred_element_type=F32) * ATTN_SCALE
        m = jnp.max(s, axis=-1, keepdims=True)
        p = jnp.exp(s - m)
        l = jnp.sum(p, axis=-1, keepdims=True)
        o = jnp.dot(p.astype(BF16), vh, preferred_element_type=F32) / l
        oc_ref[:, lo:hi] = o.astype(oc_ref.dtype)
    o_ref[...] = x_ref[...] + jnp.dot(oc_ref[...], wo_ref[...],
                                      preferred_element_type=F32)


def _xattn(x, g, wq, kv, wo):
    s, d = x.shape
    m = kv.shape[0]
    return pl.pallas_call(
        _xattn_kernel,
        grid=(s // TM_X,),
        in_specs=[
            pl.BlockSpec((TM_X, d), lambda i: (i, 0)),
            pl.BlockSpec((1, d), lambda i: (0, 0)),
            pl.BlockSpec((d, MEM_WIDTH), lambda i: (0, 0)),
            pl.BlockSpec((m, 2 * MEM_WIDTH), lambda i: (0, 0)),
            pl.BlockSpec((MEM_WIDTH, d), lambda i: (0, 0)),
        ],
        out_specs=pl.BlockSpec((TM_X, d), lambda i: (i, 0)),
        out_shape=jax.ShapeDtypeStruct((s, d), F32),
        scratch_shapes=[pltpu.VMEM((TM_X, d), BF16),
                        pltpu.VMEM((TM_X, MEM_WIDTH), BF16)],
        compiler_params=_params("parallel"),
        name="xattn",
    )(x, g, wq, kv, wo)


def _final_norm_kernel(x_ref, g_ref, o_ref):
    _rms_norm_rows(x_ref, g_ref, o_ref)


def _final_norm(x, g):
    s, d = x.shape
    return pl.pallas_call(
        _final_norm_kernel,
        grid=(s // TM_FINAL,),
        in_specs=[pl.BlockSpec((TM_FINAL, d), lambda i: (i, 0)),
                  pl.BlockSpec((1, d), lambda i: (0, 0))],
        out_specs=pl.BlockSpec((TM_FINAL, d), lambda i: (i, 0)),
        out_shape=jax.ShapeDtypeStruct((s, d), F32),
        compiler_params=_params("parallel"),
        name="final_norm",
    )(x, g)


def _rope_tables(seq_len):
    t = jnp.arange(seq_len)
    row = (t // GRID_W).astype(F32)
    col = (t % GRID_W).astype(F32)
    axis_dim = HEAD_DIM // 2
    inv_freq = ROPE_THETA ** (-jnp.arange(0, axis_dim, 2, dtype=F32) / axis_dim)
    ang_r = row[:, None] * inv_freq
    ang_c = col[:, None] * inv_freq
    cos = jnp.concatenate([jnp.cos(ang_r), jnp.cos(ang_r),
                           jnp.cos(ang_c), jnp.cos(ang_c)], axis=-1)
    sin = jnp.concatenate([-jnp.sin(ang_r), jnp.sin(ang_r),
                           -jnp.sin(ang_c), jnp.sin(ang_c)], axis=-1)
    return cos, sin


def kernel(x, mem, ffn1_norm, ffn1_w_gate, ffn1_w_up, ffn1_w_down, mix_norm, w_in, q_norm_a, k_norm_a, rpb_b, out_norm_a, out_norm_b, w_out, xattn_norm, mem_norm, xattn_wq, xattn_wkv, xattn_wo, ffn2_norm, ffn2_w_gate, ffn2_w_up, ffn2_w_down, final_norm):
    batch, seq_len, d_model = x.shape
    depth = w_in.shape[0]
    assert batch == 1 and mem.shape[0] == 1
    assert seq_len % GRID_W == 0
    assert (seq_len // GRID_W) % (NA_UNROLL * NA_ROWS) == 0
    assert seq_len % (GQA_UNROLL * TK) == 0 and seq_len % TM == 0

    cos, sin = _rope_tables(seq_len)
    row = lambda v: v.reshape(1, -1)
    xs = x[0]
    mem2 = mem[0]
    for l in range(depth):
        bf = lambda w: w[l].astype(BF16)

        act = _ffn_up(xs, row(ffn1_norm[l]), ffn1_w_gate, ffn1_w_up, l)
        xs = _ffn_down(act, ffn1_w_down, l, xs)

        proj = _in_proj(xs, row(mix_norm[l]), w_in, l, row(q_norm_a[l]),
                        row(k_norm_a[l]), cos, sin)
        oa = _gqa(proj)
        ob = _na(proj, rpb_b[l].reshape(-1))
        xs = _out_proj(oa, ob, row(out_norm_a[l]), row(out_norm_b[l]),
                       w_out, l, xs)

        kv = _mem_kv(mem2, row(mem_norm[l]), bf(xattn_wkv))
        xs = _xattn(xs, row(xattn_norm[l]), bf(xattn_wq), kv, bf(xattn_wo))

        act = _ffn_up(xs, row(ffn2_norm[l]), ffn2_w_gate, ffn2_w_up, l)
        xs = _ffn_down(act, ffn2_w_down, l, xs)

    return _final_norm(xs, row(final_norm))[None]
```

```python
import functools

import jax
import jax.numpy as jnp
import numpy as np
from jax import lax
from jax.experimental import pallas as pl
from jax.experimental.pallas import tpu as pltpu

F32 = jnp.float32
BF16 = jnp.bfloat16

HEAD_DIM = 128
N_HEADS_A = 16
N_KV_A = 4
GQA_GROUP = N_HEADS_A // N_KV_A
N_HEADS_B = 16
WIDTH_A = N_HEADS_A * HEAD_DIM
WIDTH_B = N_HEADS_B * HEAD_DIM
KV_WIDTH_A = N_KV_A * HEAD_DIM
GRID_W = 64
WIN_H_MAX = 8
WIN_W = 16
N_HEADS_MEM = 4
MEM_WIDTH = N_HEADS_MEM * HEAD_DIM
ROPE_THETA = 10000.0
EPS = 1e-6
ATTN_SCALE = HEAD_DIM ** -0.5
LOG2_E = 1.4426950408889634
MASK_VALUE = -1e30

V7X_VMEM_BYTES = 64 * 1024 * 1024
VMEM_LIMIT_BYTES = V7X_VMEM_BYTES - 6 * 1024 * 1024

TM = 1024
TN_UP = 256
TN_DOWN = 512
KSPLIT_DOWN = 2
TN_PROJ = 512
NORM_COLS = 1024
NORM_CHUNK = 256
TQ = 256
TK = 512
GQA_UNROLL = 4
NA_ROWS = 4
NA_KROWS = 12
NA_UNROLL = 4
TM_X = 256
TM_FINAL = 512


def _params(*sem):
    return pltpu.CompilerParams(dimension_semantics=sem,
                                vmem_limit_bytes=VMEM_LIMIT_BYTES)


def _rms_norm_rows(x_ref, g_ref, a_ref, col_off=0):
    tm, d = x_ref.shape
    chunk = min(NORM_CHUNK, tm)

    cb = min(NORM_COLS, d)

    def body(c, carry):
        r = pl.multiple_of(c * chunk, chunk)
        ss = jnp.zeros((chunk, 1), F32)
        for c0 in range(0, d, cb):
            xb = x_ref[pl.ds(r, chunk), c0:c0 + cb].astype(F32)
            ss = ss + jnp.sum(xb * xb, axis=-1, keepdims=True)
        inv = lax.rsqrt(ss * (1.0 / d) + EPS)
        for c0 in range(0, d, cb):
            xb = x_ref[pl.ds(r, chunk), c0:c0 + cb].astype(F32)
            y = (xb * inv) * g_ref[:, c0:c0 + cb]
            a_ref[pl.ds(r, chunk),
                  col_off + c0:col_off + c0 + cb] = y.astype(a_ref.dtype)
        return carry

    lax.fori_loop(0, tm // chunk, body, 0)


def _row_tile_copy(x_hbm, x_buf, sem, tile):
    rows = x_buf.shape[0]
    r = pl.multiple_of(tile * rows, rows)
    return pltpu.make_async_copy(x_hbm.at[pl.ds(r, rows), :], x_buf, sem)


def _prefetched_norm_prologue(sources, a_ref):
    i = pl.program_id(0)
    j = pl.program_id(1)

    @pl.when(j == 0)
    def _():
        @pl.when(i == 0)
        def _():
            for x_hbm, x_buf, sem, _, _ in sources:
                _row_tile_copy(x_hbm, x_buf, sem, 0).start()

        for x_hbm, x_buf, sem, g_ref, col_off in sources:
            _row_tile_copy(x_hbm, x_buf, sem, i).wait()
            _rms_norm_rows(x_buf, g_ref, a_ref, col_off)

    @pl.when((j == 1) & (i + 1 < pl.num_programs(0)))
    def _():
        for x_hbm, x_buf, sem, _, _ in sources:
            _row_tile_copy(x_hbm, x_buf, sem, i + 1).start()


def _ffn_up_kernel(x_hbm, g_ref, wg_ref, wu_ref, o_ref, x_buf, a_ref, sem):
    _prefetched_norm_prologue([(x_hbm, x_buf, sem.at[0], g_ref, 0)], a_ref)
    a = a_ref[...]
    gate = jnp.dot(a, wg_ref[...].astype(BF16), preferred_element_type=F32)
    up = jnp.dot(a, wu_ref[...].astype(BF16), preferred_element_type=F32)
    o_ref[...] = ((gate * jax.nn.sigmoid(gate)) * up).astype(o_ref.dtype)


def _ffn_up(x, g, wg, wu, layer):
    s, d = x.shape
    dff = wg.shape[2]
    return pl.pallas_call(
        _ffn_up_kernel,
        grid=(s // TM, dff // TN_UP),
        in_specs=[
            pl.BlockSpec(memory_space=pl.ANY),
            pl.BlockSpec((1, d), lambda i, j: (0, 0)),
            pl.BlockSpec((None, d, TN_UP), lambda i, j: (layer, 0, j)),
            pl.BlockSpec((None, d, TN_UP), lambda i, j: (layer, 0, j)),
        ],
        out_specs=pl.BlockSpec((TM, TN_UP), lambda i, j: (i, j)),
        out_shape=jax.ShapeDtypeStruct((s, dff), BF16),
        scratch_shapes=[pltpu.VMEM((TM, d), F32), pltpu.VMEM((TM, d), BF16),
                        pltpu.SemaphoreType.DMA((1,))],
        compiler_params=_params("arbitrary", "arbitrary"),
        name="ffn_up",
    )(x, g, wg, wu)


def _ffn_down_kernel(act_hbm, w_ref, r_ref, o_ref, a0_buf, a1_buf, sem):
    i = pl.program_id(0)
    j = pl.program_id(1)
    k = pl.program_id(2)
    n_i = pl.num_programs(0)
    n_j = pl.num_programs(1)
    tm, tk = a0_buf.shape
    bufs = (a0_buf, a1_buf)

    def half_copy(tile, half):
        r = pl.multiple_of(tile * tm, tm)
        return pltpu.make_async_copy(
            act_hbm.at[pl.ds(r, tm), pl.ds(half * tk, tk)], bufs[half],
            sem.at[half])

    @pl.when((j == 0) & (k == 0))
    def _():
        @pl.when(i == 0)
        def _():
            half_copy(0, 0).start()

        half_copy(i, 1).start()
        half_copy(i, 0).wait()

    @pl.when((j == 0) & (k == 1))
    def _():
        half_copy(i, 1).wait()

    @pl.when((j == n_j - 1) & (k == 1) & (i + 1 < n_i))
    def _():
        half_copy(i + 1, 0).start()

    for half in range(KSPLIT_DOWN):
        @pl.when(k == half)
        def _(half=half):
            part = 0.5 * jnp.dot(bufs[half][...], w_ref[...].astype(BF16),
                                 preferred_element_type=F32)
            if half == 0:
                o_ref[...] = r_ref[...] + part
            else:
                o_ref[...] += part


def _ffn_down(act, wd, layer, x):
    s, dff = act.shape
    d = wd.shape[2]
    assert KSPLIT_DOWN == 2 and d // TN_DOWN >= 2
    tk = dff // KSPLIT_DOWN
    return pl.pallas_call(
        _ffn_down_kernel,
        grid=(s // TM, d // TN_DOWN, KSPLIT_DOWN),
        in_specs=[
            pl.BlockSpec(memory_space=pl.ANY),
            pl.BlockSpec((None, tk, TN_DOWN), lambda i, j, k: (layer, k, j)),
            pl.BlockSpec((TM, TN_DOWN), lambda i, j, k: (i, j)),
        ],
        out_specs=pl.BlockSpec((TM, TN_DOWN), lambda i, j, k: (i, j)),
        out_shape=jax.ShapeDtypeStruct((s, d), F32),
        scratch_shapes=[pltpu.VMEM((TM, tk), BF16), pltpu.VMEM((TM, tk), BF16),
                        pltpu.SemaphoreType.DMA((KSPLIT_DOWN,))],
        compiler_params=_params("arbitrary", "arbitrary", "arbitrary"),
        name="ffn_down",
    )(act, wd, x)


def _head_pair_matrices():
    lane = np.arange(2 * HEAD_DIM)
    head = lane // HEAD_DIM
    ones = (head[:, None] == head[None, :]).astype(np.float32)
    first_half = (lane % (HEAD_DIM // 2)) < (HEAD_DIM // 4)
    partner = np.where(first_half, lane + HEAD_DIM // 4, lane - HEAD_DIM // 4)
    perm = np.zeros((2 * HEAD_DIM, 2 * HEAD_DIM), np.float32)
    perm[partner, lane] = 1.0
    return jnp.asarray(ones, BF16), jnp.asarray(perm, BF16)


def _in_proj_kernel(x_hbm, g_ref, w_ref, qg_ref, kg_ref, cos_ref, sin_ref,
                    ones_ref, perm_ref, o_ref, x_buf, a_ref, sem):
    j = pl.program_id(1)
    n_q_tiles = WIDTH_A // TN_PROJ
    n_qk_tiles = (WIDTH_A + KV_WIDTH_A) // TN_PROJ
    _prefetched_norm_prologue([(x_hbm, x_buf, sem.at[0], g_ref, 0)], a_ref)

    def project():
        return jnp.dot(a_ref[...], w_ref[...].astype(BF16),
                       preferred_element_type=F32)

    @pl.when(j < n_qk_tiles)
    def _():
        acc = project()
        is_q = j < n_q_tiles
        gain = jnp.where(is_q, qg_ref[...], kg_ref[...])
        gain = jnp.concatenate([gain, gain], axis=1)
        scale = jnp.where(is_q, ATTN_SCALE * LOG2_E, 1.0).astype(F32)
        cos = jnp.concatenate([cos_ref[...], cos_ref[...]], axis=1)
        sin = jnp.concatenate([sin_ref[...], sin_ref[...]], axis=1)
        pair = 2 * HEAD_DIM
        for hp in range(TN_PROJ // pair):
            xh = acc[:, hp * pair:(hp + 1) * pair]
            ss = jnp.dot((xh * xh).astype(BF16), ones_ref[...],
                         preferred_element_type=F32)
            y = (xh * lax.rsqrt(ss * (1.0 / HEAD_DIM) + EPS)) * gain
            partner = jnp.dot(y.astype(BF16), perm_ref[...],
                              preferred_element_type=F32)
            out = (y * cos + partner * sin) * scale
            o_ref[:, hp * pair:(hp + 1) * pair] = out.astype(o_ref.dtype)

    @pl.when(j >= n_qk_tiles)
    def _():
        o_ref[...] = project().astype(o_ref.dtype)


def _in_proj(x, g, w, layer, qg, kg, cos, sin):
    s, d = x.shape
    n = w.shape[2]
    ones, perm = _head_pair_matrices()
    pair = 2 * HEAD_DIM
    return pl.pallas_call(
        _in_proj_kernel,
        grid=(s // TM, n // TN_PROJ),
        in_specs=[
            pl.BlockSpec(memory_space=pl.ANY),
            pl.BlockSpec((1, d), lambda i, j: (0, 0)),
            pl.BlockSpec((None, d, TN_PROJ), lambda i, j: (layer, 0, j)),
            pl.BlockSpec((1, HEAD_DIM), lambda i, j: (0, 0)),
            pl.BlockSpec((1, HEAD_DIM), lambda i, j: (0, 0)),
            pl.BlockSpec((TM, HEAD_DIM), lambda i, j: (i, 0)),
            pl.BlockSpec((TM, HEAD_DIM), lambda i, j: (i, 0)),
            pl.BlockSpec((pair, pair), lambda i, j: (0, 0)),
            pl.BlockSpec((pair, pair), lambda i, j: (0, 0)),
        ],
        out_specs=pl.BlockSpec((TM, TN_PROJ), lambda i, j: (i, j)),
        out_shape=jax.ShapeDtypeStruct((s, n), BF16),
        scratch_shapes=[pltpu.VMEM((TM, d), F32), pltpu.VMEM((TM, d), BF16),
                        pltpu.SemaphoreType.DMA((1,))],
        compiler_params=_params("arbitrary", "arbitrary"),
        name="in_proj",
    )(x, g, w, qg, kg, cos, sin, ones, perm)


def _gqa_kernel(q_ref, k_ref, v_ref, o_ref, qs_ref, vt_ref, m_ref, l_ref,
                acc_ref, st_ref):
    tq = q_ref.shape[0]
    s_len = k_ref.shape[0]
    n_chunks = s_len // TK

    @pl.when(pl.program_id(1) == 0)
    def _():
        def transpose_chunk(c, carry):
            r = pl.multiple_of(c * TK, TK)
            vc = v_ref[pl.ds(r, TK), :].astype(F32)
            vt_ref[:, pl.ds(r, TK)] = vc.T.astype(vt_ref.dtype)
            return carry

        lax.fori_loop(0, n_chunks, transpose_chunk, 0)

    for g in range(GQA_GROUP):
        qs_ref[g * tq:(g + 1) * tq, :] = q_ref[:, g * HEAD_DIM:(g + 1) * HEAD_DIM]
    m_ref[...] = jnp.full(m_ref.shape, MASK_VALUE, F32)
    l_ref[...] = jnp.zeros(l_ref.shape, F32)
    acc_ref[...] = jnp.zeros(acc_ref.shape, F32)

    def scores(c, slot):
        r = pl.multiple_of(c * TK, TK)
        st_ref[slot] = lax.dot_general(k_ref[pl.ds(r, TK), :], qs_ref[...],
                                       (((1,), (1,)), ((), ())),
                                       preferred_element_type=F32)

    def softmax_pv(c, slot):
        r = pl.multiple_of(c * TK, TK)
        for g in range(GQA_GROUP):
            cols = slice(g * tq, (g + 1) * tq)
            st = st_ref[slot, :, cols]
            m_old = m_ref[:, cols]
            m_new = jnp.maximum(m_old, jnp.max(st, axis=0, keepdims=True))
            alpha = jnp.exp2(m_old - m_new)
            p = jnp.exp2(st - m_new)
            l_ref[:, cols] = alpha * l_ref[:, cols] + jnp.sum(p, axis=0,
                                                              keepdims=True)
            pv = jnp.dot(vt_ref[:, pl.ds(r, TK)], p.astype(BF16),
                         preferred_element_type=F32)
            acc_ref[:, cols] = alpha * acc_ref[:, cols] + pv
            m_ref[:, cols] = m_new

    scores(0, 0)

    def body(i, carry):
        c = GQA_UNROLL * i
        for u in range(GQA_UNROLL):
            scores(jnp.minimum(c + u + 1, n_chunks - 1), (u + 1) % 2)
            softmax_pv(c + u, u % 2)
        return carry

    lax.fori_loop(0, n_chunks // GQA_UNROLL, body, 0)
    out = (acc_ref[...] / l_ref[...]).T
    for g in range(GQA_GROUP):
        o_ref[:, g * HEAD_DIM:(g + 1) * HEAD_DIM] = out[g * tq:(g + 1) * tq]


def _gqa(proj):
    s = proj.shape[0]
    k_blk = WIDTH_A // HEAD_DIM
    v_blk = (WIDTH_A + KV_WIDTH_A) // HEAD_DIM
    gw = GQA_GROUP * HEAD_DIM
    return pl.pallas_call(
        _gqa_kernel,
        grid=(N_KV_A, s // TQ),
        in_specs=[
            pl.BlockSpec((TQ, gw), lambda h, i: (i, h)),
            pl.BlockSpec((s, HEAD_DIM), lambda h, i: (0, k_blk + h)),
            pl.BlockSpec((s, HEAD_DIM), lambda h, i: (0, v_blk + h)),
        ],
        out_specs=pl.BlockSpec((TQ, gw), lambda h, i: (i, h)),
        out_shape=jax.ShapeDtypeStruct((s, WIDTH_A), F32),
        scratch_shapes=[
            pltpu.VMEM((GQA_GROUP * TQ, HEAD_DIM), BF16),
            pltpu.VMEM((HEAD_DIM, s), BF16),
            pltpu.VMEM((1, GQA_GROUP * TQ), F32),
            pltpu.VMEM((1, GQA_GROUP * TQ), F32),
            pltpu.VMEM((HEAD_DIM, GQA_GROUP * TQ), F32),
            pltpu.VMEM((2, TK, GQA_GROUP * TQ), F32),
        ],
        compiler_params=_params("arbitrary", "arbitrary"),
        name="gqa",
    )(proj, proj, proj)


def _na_classes(grid_rows):
    kh = min(WIN_H_MAX, grid_rows)
    n_blocks = grid_rows // NA_ROWS
    classes = []
    for rb in (0, 1, n_blocks - 1):
        start = min(max(NA_ROWS * rb - NA_ROWS, 0), grid_rows - NA_KROWS)
        table = []
        for qi in range(NA_ROWS):
            i = NA_ROWS * rb + qi
            rs = min(max(i - kh // 2, 0), grid_rows - kh)
            row = []
            for a in range(NA_KROWS):
                r = start + a
                row.append(r - i + (WIN_H_MAX - 1) if rs <= r < rs + kh else None)
            table.append(row)
        classes.append(table)
    return classes


def _na_kernel(rpb_ref, q_ref, k_ref, v_ref, o_ref, strip_ref, bias_ref, s_ref,
               *, classes, n_blocks):
    h = pl.program_id(0)
    n_rpb_rows = 2 * WIN_H_MAX - 1
    n_rpb_cols = 2 * WIN_W - 1
    qblk = NA_ROWS * GRID_W
    kblk = NA_KROWS * GRID_W

    jj = lax.broadcasted_iota(jnp.int32, (GRID_W, 2 * GRID_W), 0)
    cc = lax.broadcasted_iota(jnp.int32, (GRID_W, 2 * GRID_W), 1) % GRID_W
    rel = cc - jj + (WIN_W - 1)
    cs = jnp.clip(jj - WIN_W // 2, 0, GRID_W - WIN_W)
    col_ok = (cc >= cs) & (cc < cs + WIN_W)
    for dr in range(n_rpb_rows):
        base = (h * n_rpb_rows + dr) * n_rpb_cols

        def pick(d, t, base=base):
            return jnp.where(rel == d, rpb_ref[base + d], t)

        strip = lax.fori_loop(0, n_rpb_cols, pick,
                              jnp.zeros((GRID_W, 2 * GRID_W), F32))
        strip_ref[dr] = jnp.where(col_ok, strip, MASK_VALUE)

    left = lax.broadcasted_iota(jnp.int32, (GRID_W, 2 * GRID_W), 1) < GRID_W
    masked = jnp.full((GRID_W, 2 * GRID_W), MASK_VALUE, F32)
    for cls, table in enumerate(classes):
        for qi in range(NA_ROWS):
            for ap in range(NA_KROWS // 2):
                dl, dr_ = table[qi][2 * ap], table[qi][2 * ap + 1]
                lhs = masked if dl is None else strip_ref[dl]
                rhs = masked if dr_ is None else strip_ref[dr_]
                bias_ref[cls, qi * GRID_W:(qi + 1) * GRID_W,
                         ap * 2 * GRID_W:(ap + 1) * 2 * GRID_W] = (
                             jnp.where(left, lhs, rhs))

    def key_start(rb):
        sb = jnp.clip(rb - 1, 0, n_blocks - NA_KROWS // NA_ROWS)
        return pl.multiple_of(sb * qblk, qblk)

    def scores(rb, slot):
        q0 = pl.multiple_of(rb * qblk, qblk)
        s_ref[slot] = lax.dot_general(q_ref[pl.ds(q0, qblk), :],
                                      k_ref[pl.ds(key_start(rb), kblk), :],
                                      (((1,), (1,)), ((), ())),
                                      preferred_element_type=F32)

    def softmax_pv(rb, slot):
        cls = jnp.where(rb == 0, 0, jnp.where(rb == n_blocks - 1, 2, 1))
        q0 = pl.multiple_of(rb * qblk, qblk)
        s = s_ref[slot] * ATTN_SCALE + bias_ref[cls]
        m = jnp.max(s, axis=-1, keepdims=True)
        p = jnp.exp(s - m)
        l = jnp.sum(p, axis=-1, keepdims=True)
        o = jnp.dot(p.astype(BF16), v_ref[pl.ds(key_start(rb), kblk), :],
                    preferred_element_type=F32)
        o_ref[pl.ds(q0, qblk), :] = o / l

    scores(0, 0)

    def body(i, carry):
        rb = NA_UNROLL * i
        for u in range(NA_UNROLL):
            scores(jnp.minimum(rb + u + 1, n_blocks - 1), (u + 1) % 2)
            softmax_pv(rb + u, u % 2)
        return carry

    lax.fori_loop(0, n_blocks // NA_UNROLL, body, 0)


def _na(proj, rpb_flat):
    s = proj.shape[0]
    grid_rows = s // GRID_W
    n_blocks = grid_rows // NA_ROWS
    q_blk = (WIDTH_A + 2 * KV_WIDTH_A) // HEAD_DIM
    k_blk = q_blk + N_HEADS_B
    v_blk = k_blk + N_HEADS_B
    kern = functools.partial(_na_kernel, classes=_na_classes(grid_rows),
                             n_blocks=n_blocks)
    return pl.pallas_call(
        kern,
        grid=(N_HEADS_B,),
        in_specs=[
            pl.BlockSpec(memory_space=pltpu.SMEM),
            pl.BlockSpec((s, HEAD_DIM), lambda h: (0, q_blk + h)),
            pl.BlockSpec((s, HEAD_DIM), lambda h: (0, k_blk + h)),
            pl.BlockSpec((s, HEAD_DIM), lambda h: (0, v_blk + h)),
        ],
        out_specs=pl.BlockSpec((s, HEAD_DIM), lambda h: (0, h)),
        out_shape=jax.ShapeDtypeStruct((s, WIDTH_B), F32),
        scratch_shapes=[
            pltpu.VMEM((2 * WIN_H_MAX - 1, GRID_W, 2 * GRID_W), F32),
            pltpu.VMEM((3, NA_ROWS * GRID_W, NA_KROWS * GRID_W), F32),
            pltpu.VMEM((2, NA_ROWS * GRID_W, NA_KROWS * GRID_W), F32),
        ],
        compiler_params=_params("arbitrary"),
        name="na",
    )(rpb_flat, proj, proj, proj)


def _out_proj_kernel(oa_hbm, ob_hbm, ga_ref, gb_ref, w_ref, r_ref, o_ref,
                     oa_buf, ob_buf, a_ref, sem):
    _prefetched_norm_prologue(
        [(oa_hbm, oa_buf, sem.at[0], ga_ref, 0),
         (ob_hbm, ob_buf, sem.at[1], gb_ref, oa_buf.shape[1])], a_ref)

    acc = jnp.dot(a_ref[...], w_ref[...].astype(BF16),
                  preferred_element_type=F32)
    o_ref[...] = r_ref[...] + acc


def _out_proj(oa, ob, ga, gb, w, layer, x):
    s, wa = oa.shape
    wb = ob.shape[1]
    d = w.shape[2]
    return pl.pallas_call(
        _out_proj_kernel,
        grid=(s // TM, d // TN_PROJ),
        in_specs=[
            pl.BlockSpec(memory_space=pl.ANY),
            pl.BlockSpec(memory_space=pl.ANY),
            pl.BlockSpec((1, wa), lambda i, j: (0, 0)),
            pl.BlockSpec((1, wb), lambda i, j: (0, 0)),
            pl.BlockSpec((None, wa + wb, TN_PROJ), lambda i, j: (layer, 0, j)),
            pl.BlockSpec((TM, TN_PROJ), lambda i, j: (i, j)),
        ],
        out_specs=pl.BlockSpec((TM, TN_PROJ), lambda i, j: (i, j)),
        out_shape=jax.ShapeDtypeStruct((s, d), F32),
        scratch_shapes=[pltpu.VMEM((TM, wa), F32), pltpu.VMEM((TM, wb), F32),
                        pltpu.VMEM((TM, wa + wb), BF16),
                        pltpu.SemaphoreType.DMA((2,))],
        compiler_params=_params("arbitrary", "arbitrary"),
        name="out_proj",
    )(oa, ob, ga, gb, w, x)


def _mem_kv_kernel(m_ref, g_ref, w_ref, o_ref, a_ref):
    @pl.when(pl.program_id(0) == 0)
    def _():
        _rms_norm_rows(m_ref, g_ref, a_ref)

    o_ref[...] = jnp.dot(a_ref[...], w_ref[...],
                         preferred_element_type=F32).astype(o_ref.dtype)


def _mem_kv(mem, g, wkv):
    m, d = mem.shape
    n = wkv.shape[1]
    return pl.pallas_call(
        _mem_kv_kernel,
        grid=(n // TN_PROJ,),
        in_specs=[
            pl.BlockSpec((m, d), lambda j: (0, 0)),
            pl.BlockSpec((1, d), lambda j: (0, 0)),
            pl.BlockSpec((d, TN_PROJ), lambda j: (0, j)),
        ],
        out_specs=pl.BlockSpec((m, TN_PROJ), lambda j: (0, j)),
        out_shape=jax.ShapeDtypeStruct((m, n), BF16),
        scratch_shapes=[pltpu.VMEM((m, d), BF16)],
        compiler_params=_params("arbitrary"),
        name="mem_kv",
    )(mem, g, wkv)


def _xattn_kernel(x_ref, g_ref, wq_ref, kv_ref, wo_ref, o_ref, h_ref, oc_ref):
    _rms_norm_rows(x_ref, g_ref, h_ref)
    q = jnp.dot(h_ref[...], wq_ref[...], preferred_element_type=F32)
    for hd in range(N_HEADS_MEM):
        lo, hi = hd * HEAD_DIM, (hd + 1) * HEAD_DIM
        qh = q[:, lo:hi].astype(BF16)
        kh = kv_ref[:, lo:hi]
        vh = kv_ref[:, MEM_WIDTH + lo:MEM_WIDTH + hi]
        s = lax.dot_general(qh, kh, (((1,), (1,)), ((), ())),
                            preferred_element_type=F32) * ATTN_SCALE
        m = jnp.max(s, axis=-1, keepdims=True)
        p = jnp.exp(s - m)
        l = jnp.sum(p, axis=-1, keepdims=True)
        o = jnp.dot(p.astype(BF16), vh, preferred_element_type=F32) / l
        oc_ref[:, lo:hi] = o.astype(oc_ref.dtype)
    o_ref[...] = x_ref[...] + jnp.dot(oc_ref[...], wo_ref[...],
                                      preferred_element_type=F32)


def _xattn(x, g, wq, kv, wo):
    s, d = x.shape
    m = kv.shape[0]
    return pl.pallas_call(
        _xattn_kernel,
        grid=(s // TM_X,),
        in_specs=[
            pl.BlockSpec((TM_X, d), lambda i: (i, 0)),
            pl.BlockSpec((1, d), lambda i: (0, 0)),
            pl.BlockSpec((d, MEM_WIDTH), lambda i: (0, 0)),
            pl.BlockSpec((m, 2 * MEM_WIDTH), lambda i: (0, 0)),
            pl.BlockSpec((MEM_WIDTH, d), lambda i: (0, 0)),
        ],
        out_specs=pl.BlockSpec((TM_X, d), lambda i: (i, 0)),
        out_shape=jax.ShapeDtypeStruct((s, d), F32),
        scratch_shapes=[pltpu.VMEM((TM_X, d), BF16),
                        pltpu.VMEM((TM_X, MEM_WIDTH), BF16)],
        compiler_params=_params("parallel"),
        name="xattn",
    )(x, g, wq, kv, wo)


def _final_norm_kernel(x_ref, g_ref, o_ref):
    _rms_norm_rows(x_ref, g_ref, o_ref)


def _final_norm(x, g):
    s, d = x.shape
    return pl.pallas_call(
        _final_norm_kernel,
        grid=(s // TM_FINAL,),
        in_specs=[pl.BlockSpec((TM_FINAL, d), lambda i: (i, 0)),
                  pl.BlockSpec((1, d), lambda i: (0, 0))],
        out_specs=pl.BlockSpec((TM_FINAL, d), lambda i: (i, 0)),
        out_shape=jax.ShapeDtypeStruct((s, d), F32),
        compiler_params=_params("parallel"),
        name="final_norm",
    )(x, g)


def _rope_tables(seq_len):
    t = jnp.arange(seq_len)
    row = (t // GRID_W).astype(F32)
    col = (t % GRID_W).astype(F32)
    axis_dim = HEAD_DIM // 2
    inv_freq = ROPE_THETA ** (-jnp.arange(0, axis_dim, 2, dtype=F32) / axis_dim)
    ang_r = row[:, None] * inv_freq
    ang_c = col[:, None] * inv_freq
    cos = jnp.concatenate([jnp.cos(ang_r), jnp.cos(ang_r),
                           jnp.cos(ang_c), jnp.cos(ang_c)], axis=-1)
    sin = jnp.concatenate([-jnp.sin(ang_r), jnp.sin(ang_r),
                           -jnp.sin(ang_c), jnp.sin(ang_c)], axis=-1)
    return cos, sin


def kernel(x, mem, ffn1_norm, ffn1_w_gate, ffn1_w_up, ffn1_w_down, mix_norm, w_in, q_norm_a, k_norm_a, rpb_b, out_norm_a, out_norm_b, w_out, xattn_norm, mem_norm, xattn_wq, xattn_wkv, xattn_wo, ffn2_norm, ffn2_w_gate, ffn2_w_up, ffn2_w_down, final_norm):
    batch, seq_len, d_model = x.shape
    depth = w_in.shape[0]
    assert batch == 1 and mem.shape[0] == 1
    assert seq_len % GRID_W == 0
    assert (seq_len // GRID_W) % (NA_UNROLL * NA_ROWS) == 0
    assert seq_len % (GQA_UNROLL * TK) == 0 and seq_len % TM == 0

    cos, sin = _rope_tables(seq_len)
    row = lambda v: v.reshape(1, -1)
    xs = x[0]
    mem2 = mem[0]
    for l in range(depth):
        bf = lambda w: w[l].astype(BF16)

        act = _ffn_up(xs, row(ffn1_norm[l]), ffn1_w_gate, ffn1_w_up, l)
        xs = _ffn_down(act, ffn1_w_down, l, xs)

        proj = _in_proj(xs, row(mix_norm[l]), w_in, l, row(q_norm_a[l]),
                        row(k_norm_a[l]), cos, sin)
        oa = _gqa(proj)
        ob = _na(proj, rpb_b[l].reshape(-1))
        xs = _out_proj(oa, ob, row(out_norm_a[l]), row(out_norm_b[l]),
                       w_out, l, xs)

        kv = _mem_kv(mem2, row(mem_norm[l]), bf(xattn_wkv))
        xs = _xattn(xs, row(xattn_norm[l]), bf(xattn_wq), kv, bf(xattn_wo))

        act = _ffn_up(xs, row(ffn2_norm[l]), ffn2_w_gate, ffn2_w_up, l)
        xs = _ffn_down(act, ffn2_w_down, l, xs)

    return _final_norm(xs, row(final_norm))[None]
```

```python
import functools

import jax
import jax.numpy as jnp
import numpy as np
from jax import lax
from jax.experimental import pallas as pl
from jax.experimental.pallas import tpu as pltpu

F32 = jnp.float32
BF16 = jnp.bfloat16

HEAD_DIM = 128
N_HEADS_A = 16
N_KV_A = 4
GQA_GROUP = N_HEADS_A // N_KV_A
N_HEADS_B = 16
WIDTH_A = N_HEADS_A * HEAD_DIM
WIDTH_B = N_HEADS_B * HEAD_DIM
KV_WIDTH_A = N_KV_A * HEAD_DIM
GRID_W = 64
WIN_H_MAX = 8
WIN_W = 16
N_HEADS_MEM = 4
MEM_WIDTH = N_HEADS_MEM * HEAD_DIM
ROPE_THETA = 10000.0
EPS = 1e-6
ATTN_SCALE = HEAD_DIM ** -0.5
LOG2_E = 1.4426950408889634
MASK_VALUE = -1e30

V7X_VMEM_BYTES = 64 * 1024 * 1024
VMEM_LIMIT_BYTES = V7X_VMEM_BYTES - 3 * 1024 * 1024

TM = 1024
TN_UP = 512
TN_DOWN = 512
KSPLIT_DOWN = 2
TN_PROJ = 512
RING_SLOTS = 2
RING_ROWS = 256
NORM_COLS = 1024
NORM_CHUNK = 256
TQ = 256
TK = 512
GQA_UNROLL = 4
NA_ROWS = 4
NA_KROWS = 12
NA_UNROLL = 4
TM_X = 256
TM_FINAL = 512


def _params(*sem):
    return pltpu.CompilerParams(dimension_semantics=sem,
                                vmem_limit_bytes=VMEM_LIMIT_BYTES)


def _rms_norm_rows(x_ref, g_ref, a_ref, col_off=0):
    tm, d = x_ref.shape
    chunk = min(NORM_CHUNK, tm)

    cb = min(NORM_COLS, d)

    def body(c, carry):
        r = pl.multiple_of(c * chunk, chunk)
        ss = jnp.zeros((chunk, 1), F32)
        for c0 in range(0, d, cb):
            xb = x_ref[pl.ds(r, chunk), c0:c0 + cb].astype(F32)
            ss = ss + jnp.sum(xb * xb, axis=-1, keepdims=True)
        inv = lax.rsqrt(ss * (1.0 / d) + EPS)
        for c0 in range(0, d, cb):
            xb = x_ref[pl.ds(r, chunk), c0:c0 + cb].astype(F32)
            y = (xb * inv) * g_ref[:, c0:c0 + cb]
            a_ref[pl.ds(r, chunk),
                  col_off + c0:col_off + c0 + cb] = y.astype(a_ref.dtype)
        return carry

    lax.fori_loop(0, tm // chunk, body, 0)


def _prefetched_norm_prologue(sources, a_ref):
    i = pl.program_id(0)
    j = pl.program_id(1)
    tm = a_ref.shape[0]

    def chunk_copy(x_hbm, ring, sem, tile, c):
        slots, rows = ring.shape[0], ring.shape[1]
        r = pl.multiple_of(tile * tm + c * rows, rows)
        return pltpu.make_async_copy(x_hbm.at[pl.ds(r, rows), :],
                                     ring.at[c % slots], sem.at[c % slots])

    def start_first_chunks(tile):
        for x_hbm, ring, sem, _, _ in sources:
            for c in range(ring.shape[0]):
                chunk_copy(x_hbm, ring, sem, tile, c).start()

    @pl.when(j == 0)
    def _():
        @pl.when(i == 0)
        def _():
            start_first_chunks(0)

        for x_hbm, ring, sem, g_ref, col_off in sources:
            slots, rows = ring.shape[0], ring.shape[1]
            for c in range(tm // rows):
                chunk_copy(x_hbm, ring, sem, i, c).wait()
                _rms_norm_rows(ring.at[c % slots], g_ref,
                               a_ref.at[pl.ds(c * rows, rows), :], col_off)
                if c + slots < tm // rows:
                    chunk_copy(x_hbm, ring, sem, i, c + slots).start()

    @pl.when((j == 1) & (i + 1 < pl.num_programs(0)))
    def _():
        start_first_chunks(i + 1)


def _ffn_up_kernel(x_hbm, g_ref, wg_ref, wu_ref, o_ref, ring, a_ref, sem):
    _prefetched_norm_prologue([(x_hbm, ring, sem, g_ref, 0)], a_ref)
    a = a_ref[...]
    gate = jnp.dot(a, wg_ref[...].astype(BF16), preferred_element_type=F32)
    up = jnp.dot(a, wu_ref[...].astype(BF16), preferred_element_type=F32)
    o_ref[...] = ((gate * jax.nn.sigmoid(gate)) * up).astype(o_ref.dtype)


def _ffn_up(x, g, wg, wu, layer):
    s, d = x.shape
    dff = wg.shape[2]
    return pl.pallas_call(
        _ffn_up_kernel,
        grid=(s // TM, pl.cdiv(dff, TN_UP)),
        in_specs=[
            pl.BlockSpec(memory_space=pl.ANY),
            pl.BlockSpec((1, d), lambda i, j: (0, 0)),
            pl.BlockSpec((None, d, TN_UP), lambda i, j: (layer, 0, j)),
            pl.BlockSpec((None, d, TN_UP), lambda i, j: (layer, 0, j)),
        ],
        out_specs=pl.BlockSpec((TM, TN_UP), lambda i, j: (i, j)),
        out_shape=jax.ShapeDtypeStruct((s, dff), BF16),
        scratch_shapes=[pltpu.VMEM((RING_SLOTS, RING_ROWS, d), F32),
                        pltpu.VMEM((TM, d), BF16),
                        pltpu.SemaphoreType.DMA((RING_SLOTS,))],
        compiler_params=_params("arbitrary", "arbitrary"),
        name="ffn_up",
    )(x, g, wg, wu)


def _ffn_down_kernel(act_hbm, w_ref, r_ref, o_ref, a0_buf, a1_buf, sem):
    i = pl.program_id(0)
    j = pl.program_id(1)
    k = pl.program_id(2)
    n_i = pl.num_programs(0)
    n_j = pl.num_programs(1)
    tm, tk = a0_buf.shape
    bufs = (a0_buf, a1_buf)

    def half_copy(tile, half):
        r = pl.multiple_of(tile * tm, tm)
        return pltpu.make_async_copy(
            act_hbm.at[pl.ds(r, tm), pl.ds(half * tk, tk)], bufs[half],
            sem.at[half])

    @pl.when((j == 0) & (k == 0))
    def _():
        @pl.when(i == 0)
        def _():
            half_copy(0, 0).start()

        half_copy(i, 1).start()
        half_copy(i, 0).wait()

    @pl.when((j == 0) & (k == 1))
    def _():
        half_copy(i, 1).wait()

    @pl.when((j == n_j - 1) & (k == 1) & (i + 1 < n_i))
    def _():
        half_copy(i + 1, 0).start()

    for half in range(KSPLIT_DOWN):
        @pl.when(k == half)
        def _(half=half):
            part = 0.5 * jnp.dot(bufs[half][...], w_ref[...].astype(BF16),
                                 preferred_element_type=F32)
            if half == 0:
                o_ref[...] = r_ref[...] + part
            else:
                o_ref[...] += part


def _ffn_down(act, wd, layer, x):
    s, dff = act.shape
    d = wd.shape[2]
    assert KSPLIT_DOWN == 2 and d // TN_DOWN >= 2
    tk = dff // KSPLIT_DOWN
    return pl.pallas_call(
        _ffn_down_kernel,
        grid=(s // TM, d // TN_DOWN, KSPLIT_DOWN),
        in_specs=[
            pl.BlockSpec(memory_space=pl.ANY),
            pl.BlockSpec((None, tk, TN_DOWN), lambda i, j, k: (layer, k, j)),
            pl.BlockSpec((TM, TN_DOWN), lambda i, j, k: (i, j)),
        ],
        out_specs=pl.BlockSpec((TM, TN_DOWN), lambda i, j, k: (i, j)),
        out_shape=jax.ShapeDtypeStruct((s, d), F32),
        scratch_shapes=[pltpu.VMEM((TM, tk), BF16), pltpu.VMEM((TM, tk), BF16),
                        pltpu.SemaphoreType.DMA((KSPLIT_DOWN,))],
        compiler_params=_params("arbitrary", "arbitrary", "arbitrary"),
        name="ffn_down",
    )(act, wd, x)


def _head_pair_matrices():
    lane = np.arange(2 * HEAD_DIM)
    head = lane // HEAD_DIM
    ones = (head[:, None] == head[None, :]).astype(np.float32)
    first_half = (lane % (HEAD_DIM // 2)) < (HEAD_DIM // 4)
    partner = np.where(first_half, lane + HEAD_DIM // 4, lane - HEAD_DIM // 4)
    perm = np.zeros((2 * HEAD_DIM, 2 * HEAD_DIM), np.float32)
    perm[partner, lane] = 1.0
    return jnp.asarray(ones, BF16), jnp.asarray(perm, BF16)


def _in_proj_kernel(x_hbm, g_ref, w_ref, qg_ref, kg_ref, cos_ref, sin_ref,
                    ones_ref, perm_ref, o_ref, ring, a_ref, sem):
    j = pl.program_id(1)
    n_q_tiles = WIDTH_A // TN_PROJ
    n_qk_tiles = (WIDTH_A + KV_WIDTH_A) // TN_PROJ
    _prefetched_norm_prologue([(x_hbm, ring, sem, g_ref, 0)], a_ref)

    def project():
        return jnp.dot(a_ref[...], w_ref[...].astype(BF16),
                       preferred_element_type=F32)

    @pl.when(j < n_qk_tiles)
    def _():
        acc = project()
        is_q = j < n_q_tiles
        gain = jnp.where(is_q, qg_ref[...], kg_ref[...])
        gain = jnp.concatenate([gain, gain], axis=1)
        scale = jnp.where(is_q, ATTN_SCALE * LOG2_E, 1.0).astype(F32)
        cos = jnp.concatenate([cos_ref[...], cos_ref[...]], axis=1)
        sin = jnp.concatenate([sin_ref[...], sin_ref[...]], axis=1)
        pair = 2 * HEAD_DIM
        for hp in range(TN_PROJ // pair):
            xh = acc[:, hp * pair:(hp + 1) * pair]
            ss = jnp.dot((xh * xh).astype(BF16), ones_ref[...],
                         preferred_element_type=F32)
            y = (xh * lax.rsqrt(ss * (1.0 / HEAD_DIM) + EPS)) * gain
            partner = jnp.dot(y.astype(BF16), perm_ref[...],
                              preferred_element_type=F32)
            out = (y * cos + partner * sin) * scale
            o_ref[:, hp * pair:(hp + 1) * pair] = out.astype(o_ref.dtype)

    @pl.when(j >= n_qk_tiles)
    def _():
        o_ref[...] = project().astype(o_ref.dtype)


def _in_proj(x, g, w, layer, qg, kg, cos, sin):
    s, d = x.shape
    n = w.shape[2]
    ones, perm = _head_pair_matrices()
    pair = 2 * HEAD_DIM
    return pl.pallas_call(
        _in_proj_kernel,
        grid=(s // TM, n // TN_PROJ),
        in_specs=[
            pl.BlockSpec(memory_space=pl.ANY),
            pl.BlockSpec((1, d), lambda i, j: (0, 0)),
            pl.BlockSpec((None, d, TN_PROJ), lambda i, j: (layer, 0, j)),
            pl.BlockSpec((1, HEAD_DIM), lambda i, j: (0, 0)),
            pl.BlockSpec((1, HEAD_DIM), lambda i, j: (0, 0)),
            pl.BlockSpec((TM, HEAD_DIM), lambda i, j: (i, 0)),
            pl.BlockSpec((TM, HEAD_DIM), lambda i, j: (i, 0)),
            pl.BlockSpec((pair, pair), lambda i, j: (0, 0)),
            pl.BlockSpec((pair, pair), lambda i, j: (0, 0)),
        ],
        out_specs=pl.BlockSpec((TM, TN_PROJ), lambda i, j: (i, j)),
        out_shape=jax.ShapeDtypeStruct((s, n), BF16),
        scratch_shapes=[pltpu.VMEM((RING_SLOTS, RING_ROWS, d), F32),
                        pltpu.VMEM((TM, d), BF16),
                        pltpu.SemaphoreType.DMA((RING_SLOTS,))],
        compiler_params=_params("arbitrary", "arbitrary"),
        name="in_proj",
    )(x, g, w, qg, kg, cos, sin, ones, perm)


def _gqa_kernel(q_ref, k_ref, v_ref, o_ref, qs_ref, vt_ref, m_ref, l_ref,
                acc_ref, st_ref):
    tq = q_ref.shape[0]
    s_len = k_ref.shape[0]
    n_chunks = s_len // TK

    @pl.when(pl.program_id(1) == 0)
    def _():
        def transpose_chunk(c, carry):
            r = pl.multiple_of(c * TK, TK)
            vc = v_ref[pl.ds(r, TK), :].astype(F32)
            vt_ref[:, pl.ds(r, TK)] = vc.T.astype(vt_ref.dtype)
            return carry

        lax.fori_loop(0, n_chunks, transpose_chunk, 0)

    for g in range(GQA_GROUP):
        qs_ref[g * tq:(g + 1) * tq, :] = q_ref[:, g * HEAD_DIM:(g + 1) * HEAD_DIM]
    m_ref[...] = jnp.full(m_ref.shape, MASK_VALUE, F32)
    l_ref[...] = jnp.zeros(l_ref.shape, F32)
    acc_ref[...] = jnp.zeros(acc_ref.shape, F32)

    def scores(c, slot):
        r = pl.multiple_of(c * TK, TK)
        st_ref[slot] = lax.dot_general(k_ref[pl.ds(r, TK), :], qs_ref[...],
                                       (((1,), (1,)), ((), ())),
                                       preferred_element_type=F32)

    def softmax_pv(c, slot):
        r = pl.multiple_of(c * TK, TK)
        for g in range(GQA_GROUP):
            cols = slice(g * tq, (g + 1) * tq)
            st = st_ref[slot, :, cols]
            m_old = m_ref[:, cols]
            m_new = jnp.maximum(m_old, jnp.max(st, axis=0, keepdims=True))
            alpha = jnp.exp2(m_old - m_new)
            p = jnp.exp2(st - m_new)
            l_ref[:, cols] = alpha * l_ref[:, cols] + jnp.sum(p, axis=0,
                                                              keepdims=True)
            pv = jnp.dot(vt_ref[:, pl.ds(r, TK)], p.astype(BF16),
                         preferred_element_type=F32)
            acc_ref[:, cols] = alpha * acc_ref[:, cols] + pv
            m_ref[:, cols] = m_new

    scores(0, 0)

    def body(i, carry):
        c = GQA_UNROLL * i
        for u in range(GQA_UNROLL):
            scores(jnp.minimum(c + u + 1, n_chunks - 1), (u + 1) % 2)
            softmax_pv(c + u, u % 2)
        return carry

    lax.fori_loop(0, n_chunks // GQA_UNROLL, body, 0)
    out = (acc_ref[...] / l_ref[...]).T
    for g in range(GQA_GROUP):
        o_ref[:, g * HEAD_DIM:(g + 1) * HEAD_DIM] = out[g * tq:(g + 1) * tq]


def _gqa(proj):
    s = proj.shape[0]
    k_blk = WIDTH_A // HEAD_DIM
    v_blk = (WIDTH_A + KV_WIDTH_A) // HEAD_DIM
    gw = GQA_GROUP * HEAD_DIM
    return pl.pallas_call(
        _gqa_kernel,
        grid=(N_KV_A, s // TQ),
        in_specs=[
            pl.BlockSpec((TQ, gw), lambda h, i: (i, h)),
            pl.BlockSpec((s, HEAD_DIM), lambda h, i: (0, k_blk + h)),
            pl.BlockSpec((s, HEAD_DIM), lambda h, i: (0, v_blk + h)),
        ],
        out_specs=pl.BlockSpec((TQ, gw), lambda h, i: (i, h)),
        out_shape=jax.ShapeDtypeStruct((s, WIDTH_A), F32),
        scratch_shapes=[
            pltpu.VMEM((GQA_GROUP * TQ, HEAD_DIM), BF16),
            pltpu.VMEM((HEAD_DIM, s), BF16),
            pltpu.VMEM((1, GQA_GROUP * TQ), F32),
            pltpu.VMEM((1, GQA_GROUP * TQ), F32),
            pltpu.VMEM((HEAD_DIM, GQA_GROUP * TQ), F32),
            pltpu.VMEM((2, TK, GQA_GROUP * TQ), F32),
        ],
        compiler_params=_params("arbitrary", "arbitrary"),
        name="gqa",
    )(proj, proj, proj)


def _na_classes(grid_rows):
    kh = min(WIN_H_MAX, grid_rows)
    n_blocks = grid_rows // NA_ROWS
    classes = []
    for rb in (0, 1, n_blocks - 1):
        start = min(max(NA_ROWS * rb - NA_ROWS, 0), grid_rows - NA_KROWS)
        table = []
        for qi in range(NA_ROWS):
            i = NA_ROWS * rb + qi
            rs = min(max(i - kh // 2, 0), grid_rows - kh)
            row = []
            for a in range(NA_KROWS):
                r = start + a
                row.append(r - i + (WIN_H_MAX - 1) if rs <= r < rs + kh else None)
            table.append(row)
        classes.append(table)
    return classes


def _na_kernel(rpb_ref, q_ref, k_ref, v_ref, o_ref, strip_ref, bias_ref, s_ref,
               *, classes, n_blocks):
    h = pl.program_id(0)
    n_rpb_rows = 2 * WIN_H_MAX - 1
    n_rpb_cols = 2 * WIN_W - 1
    qblk = NA_ROWS * GRID_W
    kblk = NA_KROWS * GRID_W

    jj = lax.broadcasted_iota(jnp.int32, (GRID_W, 2 * GRID_W), 0)
    cc = lax.broadcasted_iota(jnp.int32, (GRID_W, 2 * GRID_W), 1) % GRID_W
    rel = cc - jj + (WIN_W - 1)
    cs = jnp.clip(jj - WIN_W // 2, 0, GRID_W - WIN_W)
    col_ok = (cc >= cs) & (cc < cs + WIN_W)
    for dr in range(n_rpb_rows):
        base = (h * n_rpb_rows + dr) * n_rpb_cols

        def pick(d, t, base=base):
            return jnp.where(rel == d, rpb_ref[base + d], t)

        strip = lax.fori_loop(0, n_rpb_cols, pick,
                              jnp.zeros((GRID_W, 2 * GRID_W), F32))
        strip_ref[dr] = jnp.where(col_ok, strip, MASK_VALUE)

    left = lax.broadcasted_iota(jnp.int32, (GRID_W, 2 * GRID_W), 1) < GRID_W
    masked = jnp.full((GRID_W, 2 * GRID_W), MASK_VALUE, F32)
    for cls, table in enumerate(classes):
        for qi in range(NA_ROWS):
            for ap in range(NA_KROWS // 2):
                dl, dr_ = table[qi][2 * ap], table[qi][2 * ap + 1]
                lhs = masked if dl is None else strip_ref[dl]
                rhs = masked if dr_ is None else strip_ref[dr_]
                bias_ref[cls, qi * GRID_W:(qi + 1) * GRID_W,
                         ap * 2 * GRID_W:(ap + 1) * 2 * GRID_W] = (
                             jnp.where(left, lhs, rhs))

    def key_start(rb):
        sb = jnp.clip(rb - 1, 0, n_blocks - NA_KROWS // NA_ROWS)
        return pl.multiple_of(sb * qblk, qblk)

    def scores(rb, slot):
        q0 = pl.multiple_of(rb * qblk, qblk)
        s_ref[slot] = lax.dot_general(q_ref[pl.ds(q0, qblk), :],
                                      k_ref[pl.ds(key_start(rb), kblk), :],
                                      (((1,), (1,)), ((), ())),
                                      preferred_element_type=F32)

    def softmax_pv(rb, slot):
        cls = jnp.where(rb == 0, 0, jnp.where(rb == n_blocks - 1, 2, 1))
        q0 = pl.multiple_of(rb * qblk, qblk)
        s = s_ref[slot] * ATTN_SCALE + bias_ref[cls]
        m = jnp.max(s, axis=-1, keepdims=True)
        p = jnp.exp(s - m)
        l = jnp.sum(p, axis=-1, keepdims=True)
        o = jnp.dot(p.astype(BF16), v_ref[pl.ds(key_start(rb), kblk), :],
                    preferred_element_type=F32)
        o_ref[pl.ds(q0, qblk), :] = o / l

    scores(0, 0)

    def body(i, carry):
        rb = NA_UNROLL * i
        for u in range(NA_UNROLL):
            scores(jnp.minimum(rb + u + 1, n_blocks - 1), (u + 1) % 2)
            softmax_pv(rb + u, u % 2)
        return carry

    lax.fori_loop(0, n_blocks // NA_UNROLL, body, 0)


def _na(proj, rpb_flat):
    s = proj.shape[0]
    grid_rows = s // GRID_W
    n_blocks = grid_rows // NA_ROWS
    q_blk = (WIDTH_A + 2 * KV_WIDTH_A) // HEAD_DIM
    k_blk = q_blk + N_HEADS_B
    v_blk = k_blk + N_HEADS_B
    kern = functools.partial(_na_kernel, classes=_na_classes(grid_rows),
                             n_blocks=n_blocks)
    return pl.pallas_call(
        kern,
        grid=(N_HEADS_B,),
        in_specs=[
            pl.BlockSpec(memory_space=pltpu.SMEM),
            pl.BlockSpec((s, HEAD_DIM), lambda h: (0, q_blk + h)),
            pl.BlockSpec((s, HEAD_DIM), lambda h: (0, k_blk + h)),
            pl.BlockSpec((s, HEAD_DIM), lambda h: (0, v_blk + h)),
        ],
        out_specs=pl.BlockSpec((s, HEAD_DIM), lambda h: (0, h)),
        out_shape=jax.ShapeDtypeStruct((s, WIDTH_B), F32),
        scratch_shapes=[
            pltpu.VMEM((2 * WIN_H_MAX - 1, GRID_W, 2 * GRID_W), F32),
            pltpu.VMEM((3, NA_ROWS * GRID_W, NA_KROWS * GRID_W), F32),
            pltpu.VMEM((2, NA_ROWS * GRID_W, NA_KROWS * GRID_W), F32),
        ],
        compiler_params=_params("arbitrary"),
        name="na",
    )(rpb_flat, proj, proj, proj)


def _out_proj_kernel(oa_hbm, ob_hbm, ga_ref, gb_ref, w_ref, r_ref, o_ref,
                     oa_ring, ob_ring, a_ref, sem_a, sem_b):
    _prefetched_norm_prologue(
        [(oa_hbm, oa_ring, sem_a, ga_ref, 0),
         (ob_hbm, ob_ring, sem_b, gb_ref, oa_ring.shape[2])], a_ref)

    acc = jnp.dot(a_ref[...], w_ref[...].astype(BF16),
                  preferred_element_type=F32)
    o_ref[...] = r_ref[...] + acc


def _out_proj(oa, ob, ga, gb, w, layer, x):
    s, wa = oa.shape
    wb = ob.shape[1]
    d = w.shape[2]
    return pl.pallas_call(
        _out_proj_kernel,
        grid=(s // TM, d // TN_PROJ),
        in_specs=[
            pl.BlockSpec(memory_space=pl.ANY),
            pl.BlockSpec(memory_space=pl.ANY),
            pl.BlockSpec((1, wa), lambda i, j: (0, 0)),
            pl.BlockSpec((1, wb), lambda i, j: (0, 0)),
            pl.BlockSpec((None, wa + wb, TN_PROJ), lambda i, j: (layer, 0, j)),
            pl.BlockSpec((TM, TN_PROJ), lambda i, j: (i, j)),
        ],
        out_specs=pl.BlockSpec((TM, TN_PROJ), lambda i, j: (i, j)),
        out_shape=jax.ShapeDtypeStruct((s, d), F32),
        scratch_shapes=[pltpu.VMEM((RING_SLOTS, RING_ROWS, wa), F32),
                        pltpu.VMEM((RING_SLOTS, RING_ROWS, wb), F32),
                        pltpu.VMEM((TM, wa + wb), BF16),
                        pltpu.SemaphoreType.DMA((RING_SLOTS,)),
                        pltpu.SemaphoreType.DMA((RING_SLOTS,))],
        compiler_params=_params("arbitrary", "arbitrary"),
        name="out_proj",
    )(oa, ob, ga, gb, w, x)


def _mem_kv_kernel(m_ref, g_ref, w_ref, o_ref, a_ref):
    @pl.when(pl.program_id(0) == 0)
    def _():
        _rms_norm_rows(m_ref, g_ref, a_ref)

    o_ref[...] = jnp.dot(a_ref[...], w_ref[...],
                         preferred_element_type=F32).astype(o_ref.dtype)


def _mem_kv(mem, g, wkv):
    m, d = mem.shape
    n = wkv.shape[1]
    return pl.pallas_call(
        _mem_kv_kernel,
        grid=(n // TN_PROJ,),
        in_specs=[
            pl.BlockSpec((m, d), lambda j: (0, 0)),
            pl.BlockSpec((1, d), lambda j: (0, 0)),
            pl.BlockSpec((d, TN_PROJ), lambda j: (0, j)),
        ],
        out_specs=pl.BlockSpec((m, TN_PROJ), lambda j: (0, j)),
        out_shape=jax.ShapeDtypeStruct((m, n), BF16),
        scratch_shapes=[pltpu.VMEM((m, d), BF16)],
        compiler_params=_params("arbitrary"),
        name="mem_kv",
    )(mem, g, wkv)


def _xattn_kernel(x_ref, g_ref, wq_ref, kv_ref, wo_ref, o_ref, h_ref, oc_ref):
    _rms_norm_rows(x_ref, g_ref, h_ref)
    q = jnp.dot(h_ref[...], wq_ref[...], preferred_element_type=F32)
    for hd in range(N_HEADS_MEM):
        lo, hi = hd * HEAD_DIM, (hd + 1) * HEAD_DIM
        qh = q[:, lo:hi].astype(BF16)
        kh = kv_ref[:, lo:hi]
        vh = kv_ref[:, MEM_WIDTH + lo:MEM_WIDTH + hi]
        s = lax.dot_general(qh, kh, (((1,), (1,)), ((), ())),
                            preferred_element_type=F32) * ATTN_SCALE
        m = jnp.max(s, axis=-1, keepdims=True)
        p = jnp.exp(s - m)
        l = jnp.sum(p, axis=-1, keepdims=True)
        o = jnp.dot(p.astype(BF16), vh, preferred_element_type=F32) / l
        oc_ref[:, lo:hi] = o.astype(oc_ref.dtype)
    o_ref[...] = x_ref[...] + jnp.dot(oc_ref[...], wo_ref[...],
                                      preferred_element_type=F32)


def _xattn(x, g, wq, kv, wo):
    s, d = x.shape
    m = kv.shape[0]
    return pl.pallas_call(
        _xattn_kernel,
        grid=(s // TM_X,),
        in_specs=[
            pl.BlockSpec((TM_X, d), lambda i: (i, 0)),
            pl.BlockSpec((1, d), lambda i: (0, 0)),
            pl.BlockSpec((d, MEM_WIDTH), lambda i: (0, 0)),
            pl.BlockSpec((m, 2 * MEM_WIDTH), lambda i: (0, 0)),
            pl.BlockSpec((MEM_WIDTH, d), lambda i: (0, 0)),
        ],
        out_specs=pl.BlockSpec((TM_X, d), lambda i: (i, 0)),
        out_shape=jax.ShapeDtypeStruct((s, d), F32),
        scratch_shapes=[pltpu.VMEM((TM_X, d), BF16),
                        pltpu.VMEM((TM_X, MEM_WIDTH), BF16)],
        compiler_params=_params("parallel"),
        name="xattn",
    )(x, g, wq, kv, wo)


def _final_norm_kernel(x_ref, g_ref, o_ref):
    _rms_norm_rows(x_ref, g_ref, o_ref)


def _final_norm(x, g):
    s, d = x.shape
    return pl.pallas_call(
        _final_norm_kernel,
        grid=(s // TM_FINAL,),
        in_specs=[pl.BlockSpec((TM_FINAL, d), lambda i: (i, 0)),
                  pl.BlockSpec((1, d), lambda i: (0, 0))],
        out_specs=pl.BlockSpec((TM_FINAL, d), lambda i: (i, 0)),
        out_shape=jax.ShapeDtypeStruct((s, d), F32),
        compiler_params=_params("parallel"),
        name="final_norm",
    )(x, g)


def _rope_tables(seq_len):
    t = jnp.arange(seq_len)
    row = (t // GRID_W).astype(F32)
    col = (t % GRID_W).astype(F32)
    axis_dim = HEAD_DIM // 2
    inv_freq = ROPE_THETA ** (-jnp.arange(0, axis_dim, 2, dtype=F32) / axis_dim)
    ang_r = row[:, None] * inv_freq
    ang_c = col[:, None] * inv_freq
    cos = jnp.concatenate([jnp.cos(ang_r), jnp.cos(ang_r),
                           jnp.cos(ang_c), jnp.cos(ang_c)], axis=-1)
    sin = jnp.concatenate([-jnp.sin(ang_r), jnp.sin(ang_r),
                           -jnp.sin(ang_c), jnp.sin(ang_c)], axis=-1)
    return cos, sin


def kernel(x, mem, ffn1_norm, ffn1_w_gate, ffn1_w_up, ffn1_w_down, mix_norm, w_in, q_norm_a, k_norm_a, rpb_b, out_norm_a, out_norm_b, w_out, xattn_norm, mem_norm, xattn_wq, xattn_wkv, xattn_wo, ffn2_norm, ffn2_w_gate, ffn2_w_up, ffn2_w_down, final_norm):
    batch, seq_len, d_model = x.shape
    depth = w_in.shape[0]
    assert batch == 1 and mem.shape[0] == 1
    assert seq_len % GRID_W == 0
    assert (seq_len // GRID_W) % (NA_UNROLL * NA_ROWS) == 0
    assert seq_len % (GQA_UNROLL * TK) == 0 and seq_len % TM == 0

    cos, sin = _rope_tables(seq_len)
    row = lambda v: v.reshape(1, -1)
    xs = x[0]
    mem2 = mem[0]
    for l in range(depth):
        bf = lambda w: w[l].astype(BF16)

        act = _ffn_up(xs, row(ffn1_norm[l]), ffn1_w_gate, ffn1_w_up, l)
        xs = _ffn_down(act, ffn1_w_down, l, xs)

        proj = _in_proj(xs, row(mix_norm[l]), w_in, l, row(q_norm_a[l]),
                        row(k_norm_a[l]), cos, sin)
        oa = _gqa(proj)
        ob = _na(proj, rpb_b[l].reshape(-1))
        xs = _out_proj(oa, ob, row(out_norm_a[l]), row(out_norm_b[l]),
                       w_out, l, xs)

        kv = _mem_kv(mem2, row(mem_norm[l]), bf(xattn_wkv))
        xs = _xattn(xs, row(xattn_norm[l]), bf(xattn_wq), kv, bf(xattn_wo))

        act = _ffn_up(xs, row(ffn2_norm[l]), ffn2_w_gate, ffn2_w_up, l)
        xs = _ffn_down(act, ffn2_w_down, l, xs)

    return _final_norm(xs, row(final_norm))[None]
```

```python
import functools

import jax
import jax.numpy as jnp
import numpy as np
from jax import lax
from jax.experimental import pallas as pl
from jax.experimental.pallas import tpu as pltpu

F32 = jnp.float32
BF16 = jnp.bfloat16

HEAD_DIM = 128
N_HEADS_A = 16
N_KV_A = 4
GQA_GROUP = N_HEADS_A // N_KV_A
N_HEADS_B = 16
WIDTH_A = N_HEADS_A * HEAD_DIM
WIDTH_B = N_HEADS_B * HEAD_DIM
KV_WIDTH_A = N_KV_A * HEAD_DIM
GRID_W = 64
WIN_H_MAX = 8
WIN_W = 16
N_HEADS_MEM = 4
MEM_WIDTH = N_HEADS_MEM * HEAD_DIM
ROPE_THETA = 10000.0
EPS = 1e-6
ATTN_SCALE = HEAD_DIM ** -0.5
LOG2_E = 1.4426950408889634
MASK_VALUE = -1e30

V7X_VMEM_BYTES = 64 * 1024 * 1024
VMEM_LIMIT_BYTES = V7X_VMEM_BYTES - 6 * 1024 * 1024

TM = 1024
TN_UP = 256
TN_DOWN = 512
KSPLIT_DOWN = 2
TN_PROJ = 512
NORM_COLS = 1024
NORM_CHUNK = 256
TQ = 256
TK = 512
GQA_UNROLL = 4
NA_ROWS = 4
NA_KROWS = 12
NA_UNROLL = 4
TM_X = 256
TM_FINAL = 512


def _params(*sem):
    return pltpu.CompilerParams(dimension_semantics=sem,
                                vmem_limit_bytes=VMEM_LIMIT_BYTES)


def _rms_norm_rows(x_ref, g_ref, a_ref, col_off=0):
    tm, d = x_ref.shape
    chunk = min(NORM_CHUNK, tm)

    cb = min(NORM_COLS, d)

    def body(c, carry):
        r = pl.multiple_of(c * chunk, chunk)
        ss = jnp.zeros((chunk, 1), F32)
        for c0 in range(0, d, cb):
            xb = x_ref[pl.ds(r, chunk), c0:c0 + cb].astype(F32)
            ss = ss + jnp.sum(xb * xb, axis=-1, keepdims=True)
        inv = lax.rsqrt(ss * (1.0 / d) + EPS)
        for c0 in range(0, d, cb):
            xb = x_ref[pl.ds(r, chunk), c0:c0 + cb].astype(F32)
            y = (xb * inv) * g_ref[:, c0:c0 + cb]
            a_ref[pl.ds(r, chunk),
                  col_off + c0:col_off + c0 + cb] = y.astype(a_ref.dtype)
        return carry

    lax.fori_loop(0, tm // chunk, body, 0)


def _row_tile_copy(x_hbm, x_buf, sem, tile):
    rows = x_buf.shape[0]
    r = pl.multiple_of(tile * rows, rows)
    return pltpu.make_async_copy(x_hbm.at[pl.ds(r, rows), :], x_buf, sem)


def _prefetched_norm_prologue(sources, a_ref):
    i = pl.program_id(0)
    j = pl.program_id(1)

    @pl.when(j == 0)
    def _():
        @pl.when(i == 0)
        def _():
            for x_hbm, x_buf, sem, _, _ in sources:
                _row_tile_copy(x_hbm, x_buf, sem, 0).start()

        for x_hbm, x_buf, sem, g_ref, col_off in sources:
            _row_tile_copy(x_hbm, x_buf, sem, i).wait()
            _rms_norm_rows(x_buf, g_ref, a_ref, col_off)

    @pl.when((j == 1) & (i + 1 < pl.num_programs(0)))
    def _():
        for x_hbm, x_buf, sem, _, _ in sources:
            _row_tile_copy(x_hbm, x_buf, sem, i + 1).start()


def _ffn_up_kernel(x_hbm, g_ref, wg_ref, wu_ref, o_ref, x_buf, a_ref, sem):
    _prefetched_norm_prologue([(x_hbm, x_buf, sem.at[0], g_ref, 0)], a_ref)
    a = a_ref[...]
    gate = jnp.dot(a, wg_ref[...].astype(BF16), preferred_element_type=F32)
    up = jnp.dot(a, wu_ref[...].astype(BF16), preferred_element_type=F32)
    o_ref[...] = ((gate * jax.nn.sigmoid(gate)) * up).astype(o_ref.dtype)


def _ffn_up(x, g, wg, wu, layer):
    s, d = x.shape
    dff = wg.shape[2]
    return pl.pallas_call(
        _ffn_up_kernel,
        grid=(s // TM, dff // TN_UP),
        in_specs=[
            pl.BlockSpec(memory_space=pl.ANY),
            pl.BlockSpec((1, d), lambda i, j: (0, 0)),
            pl.BlockSpec((None, d, TN_UP), lambda i, j: (layer, 0, j)),
            pl.BlockSpec((None, d, TN_UP), lambda i, j: (layer, 0, j)),
        ],
        out_specs=pl.BlockSpec((TM, TN_UP), lambda i, j: (i, j)),
        out_shape=jax.ShapeDtypeStruct((s, dff), BF16),
        scratch_shapes=[pltpu.VMEM((TM, d), F32), pltpu.VMEM((TM, d), BF16),
                        pltpu.SemaphoreType.DMA((1,))],
        compiler_params=_params("arbitrary", "arbitrary"),
        name="ffn_up",
    )(x, g, wg, wu)


def _ffn_down_kernel(act_hbm, w_ref, r_ref, o_ref, a0_buf, a1_buf, sem):
    i = pl.program_id(0)
    j = pl.program_id(1)
    k = pl.program_id(2)
    n_i = pl.num_programs(0)
    n_j = pl.num_programs(1)
    tm, tk = a0_buf.shape
    bufs = (a0_buf, a1_buf)

    def half_copy(tile, half):
        r = pl.multiple_of(tile * tm, tm)
        return pltpu.make_async_copy(
            act_hbm.at[pl.ds(r, tm), pl.ds(half * tk, tk)], bufs[half],
            sem.at[half])

    @pl.when((j == 0) & (k == 0))
    def _():
        @pl.when(i == 0)
        def _():
            half_copy(0, 0).start()

        half_copy(i, 1).start()
        half_copy(i, 0).wait()

    @pl.when((j == 0) & (k == 1))
    def _():
        half_copy(i, 1).wait()

    @pl.when((j == n_j - 1) & (k == 1) & (i + 1 < n_i))
    def _():
        half_copy(i + 1, 0).start()

    for half in range(KSPLIT_DOWN):
        @pl.when(k == half)
        def _(half=half):
            part = 0.5 * jnp.dot(bufs[half][...], w_ref[...].astype(BF16),
                                 preferred_element_type=F32)
            if half == 0:
                o_ref[...] = r_ref[...] + part
            else:
                o_ref[...] += part


def _ffn_down(act, wd, layer, x):
    s, dff = act.shape
    d = wd.shape[2]
    assert KSPLIT_DOWN == 2 and d // TN_DOWN >= 2
    tk = dff // KSPLIT_DOWN
    return pl.pallas_call(
        _ffn_down_kernel,
        grid=(s // TM, d // TN_DOWN, KSPLIT_DOWN),
        in_specs=[
            pl.BlockSpec(memory_space=pl.ANY),
            pl.BlockSpec((None, tk, TN_DOWN), lambda i, j, k: (layer, k, j)),
            pl.BlockSpec((TM, TN_DOWN), lambda i, j, k: (i, j)),
        ],
        out_specs=pl.BlockSpec((TM, TN_DOWN), lambda i, j, k: (i, j)),
        out_shape=jax.ShapeDtypeStruct((s, d), F32),
        scratch_shapes=[pltpu.VMEM((TM, tk), BF16), pltpu.VMEM((TM, tk), BF16),
                        pltpu.SemaphoreType.DMA((KSPLIT_DOWN,))],
        compiler_params=_params("arbitrary", "arbitrary", "arbitrary"),
        name="ffn_down",
    )(act, wd, x)


def _head_pair_matrices():
    lane = np.arange(2 * HEAD_DIM)
    head = lane // HEAD_DIM
    ones = (head[:, None] == head[None, :]).astype(np.float32)
    first_half = (lane % (HEAD_DIM // 2)) < (HEAD_DIM // 4)
    partner = np.where(first_half, lane + HEAD_DIM // 4, lane - HEAD_DIM // 4)
    perm = np.zeros((2 * HEAD_DIM, 2 * HEAD_DIM), np.float32)
    perm[partner, lane] = 1.0
    return jnp.asarray(ones, BF16), jnp.asarray(perm, BF16)


def _in_proj_kernel(x_hbm, g_ref, w_ref, qg_ref, kg_ref, cos_ref, sin_ref,
                    ones_ref, perm_ref, o_ref, x_buf, a_ref, sem):
    j = pl.program_id(1)
    n_q_tiles = WIDTH_A // TN_PROJ
    n_qk_tiles = (WIDTH_A + KV_WIDTH_A) // TN_PROJ
    _prefetched_norm_prologue([(x_hbm, x_buf, sem.at[0], g_ref, 0)], a_ref)

    def project():
        return jnp.dot(a_ref[...], w_ref[...].astype(BF16),
                       preferred_element_type=F32)

    @pl.when(j < n_qk_tiles)
    def _():
        acc = project()
        is_q = j < n_q_tiles
        gain = jnp.where(is_q, qg_ref[...], kg_ref[...])
        gain = jnp.concatenate([gain, gain], axis=1)
        scale = jnp.where(is_q, ATTN_SCALE * LOG2_E, 1.0).astype(F32)
        cos = jnp.concatenate([cos_ref[...], cos_ref[...]], axis=1)
        sin = jnp.concatenate([sin_ref[...], sin_ref[...]], axis=1)
        pair = 2 * HEAD_DIM
        for hp in range(TN_PROJ // pair):
            xh = acc[:, hp * pair:(hp + 1) * pair]
            ss = jnp.dot((xh * xh).astype(BF16), ones_ref[...],
                         preferred_element_type=F32)
            y = (xh * lax.rsqrt(ss * (1.0 / HEAD_DIM) + EPS)) * gain
            partner = jnp.dot(y.astype(BF16), perm_ref[...],
                              preferred_element_type=F32)
            out = (y * cos + partner * sin) * scale
            o_ref[:, hp * pair:(hp + 1) * pair] = out.astype(o_ref.dtype)

    @pl.when(j >= n_qk_tiles)
    def _():
        o_ref[...] = project().astype(o_ref.dtype)


def _in_proj(x, g, w, layer, qg, kg, cos, sin):
    s, d = x.shape
    n = w.shape[2]
    ones, perm = _head_pair_matrices()
    pair = 2 * HEAD_DIM
    return pl.pallas_call(
        _in_proj_kernel,
        grid=(s // TM, n // TN_PROJ),
        in_specs=[
            pl.BlockSpec(memory_space=pl.ANY),
            pl.BlockSpec((1, d), lambda i, j: (0, 0)),
            pl.BlockSpec((None, d, TN_PROJ), lambda i, j: (layer, 0, j)),
            pl.BlockSpec((1, HEAD_DIM), lambda i, j: (0, 0)),
            pl.BlockSpec((1, HEAD_DIM), lambda i, j: (0, 0)),
            pl.BlockSpec((TM, HEAD_DIM), lambda i, j: (i, 0)),
            pl.BlockSpec((TM, HEAD_DIM), lambda i, j: (i, 0)),
            pl.BlockSpec((pair, pair), lambda i, j: (0, 0)),
            pl.BlockSpec((pair, pair), lambda i, j: (0, 0)),
        ],
        out_specs=pl.BlockSpec((TM, TN_PROJ), lambda i, j: (i, j)),
        out_shape=jax.ShapeDtypeStruct((s, n), BF16),
        scratch_shapes=[pltpu.VMEM((TM, d), F32), pltpu.VMEM((TM, d), BF16),
                        pltpu.SemaphoreType.DMA((1,))],
        compiler_params=_params("arbitrary", "arbitrary"),
        name="in_proj",
    )(x, g, w, qg, kg, cos, sin, ones, perm)


def _gqa_kernel(q_ref, k_ref, v_ref, o_ref, qs_ref, vt_ref, m_ref, l_ref,
                acc_ref, st_ref):
    tq = q_ref.shape[0]
    s_len = k_ref.shape[0]
    n_chunks = s_len // TK

    @pl.when(pl.program_id(1) == 0)
    def _():
        def transpose_chunk(c, carry):
            r = pl.multiple_of(c * TK, TK)
            vc = v_ref[pl.ds(r, TK), :].astype(F32)
            vt_ref[:, pl.ds(r, TK)] = vc.T.astype(vt_ref.dtype)
            return carry

        lax.fori_loop(0, n_chunks, transpose_chunk, 0)

    for g in range(GQA_GROUP):
        qs_ref[g * tq:(g + 1) * tq, :] = q_ref[:, g * HEAD_DIM:(g + 1) * HEAD_DIM]
    m_ref[...] = jnp.full(m_ref.shape, MASK_VALUE, F32)
    l_ref[...] = jnp.zeros(l_ref.shape, F32)
    acc_ref[...] = jnp.zeros(acc_ref.shape, F32)

    def scores(c, slot):
        r = pl.multiple_of(c * TK, TK)
        st_ref[slot] = lax.dot_general(k_ref[pl.ds(r, TK), :], qs_ref[...],
                                       (((1,), (1,)), ((), ())),
                                       preferred_element_type=F32)

    def softmax_pv(c, slot):
        r = pl.multiple_of(c * TK, TK)
        for g in range(GQA_GROUP):
            cols = slice(g * tq, (g + 1) * tq)
            st = st_ref[slot, :, cols]
            m_old = m_ref[:, cols]
            m_new = jnp.maximum(m_old, jnp.max(st, axis=0, keepdims=True))
            alpha = jnp.exp2(m_old - m_new)
            p = jnp.exp2(st - m_new)
            l_ref[:, cols] = alpha * l_ref[:, cols] + jnp.sum(p, axis=0,
                                                              keepdims=True)
            pv = jnp.dot(vt_ref[:, pl.ds(r, TK)], p.astype(BF16),
                         preferred_element_type=F32)
            acc_ref[:, cols] = alpha * acc_ref[:, cols] + pv
            m_ref[:, cols] = m_new

    scores(0, 0)

    def body(i, carry):
        c = GQA_UNROLL * i
        for u in range(GQA_UNROLL):
            scores(jnp.minimum(c + u + 1, n_chunks - 1), (u + 1) % GQA_UNROLL)
            softmax_pv(c + u, u)
        return carry

    lax.fori_loop(0, n_chunks // GQA_UNROLL, body, 0)
    out = (acc_ref[...] / l_ref[...]).T
    for g in range(GQA_GROUP):
        o_ref[:, g * HEAD_DIM:(g + 1) * HEAD_DIM] = out[g * tq:(g + 1) * tq]


def _gqa(proj):
    s = proj.shape[0]
    k_blk = WIDTH_A // HEAD_DIM
    v_blk = (WIDTH_A + KV_WIDTH_A) // HEAD_DIM
    gw = GQA_GROUP * HEAD_DIM
    return pl.pallas_call(
        _gqa_kernel,
        grid=(N_KV_A, s // TQ),
        in_specs=[
            pl.BlockSpec((TQ, gw), lambda h, i: (i, h)),
            pl.BlockSpec((s, HEAD_DIM), lambda h, i: (0, k_blk + h)),
            pl.BlockSpec((s, HEAD_DIM), lambda h, i: (0, v_blk + h)),
        ],
        out_specs=pl.BlockSpec((TQ, gw), lambda h, i: (i, h)),
        out_shape=jax.ShapeDtypeStruct((s, WIDTH_A), F32),
        scratch_shapes=[
            pltpu.VMEM((GQA_GROUP * TQ, HEAD_DIM), BF16),
            pltpu.VMEM((HEAD_DIM, s), BF16),
            pltpu.VMEM((1, GQA_GROUP * TQ), F32),
            pltpu.VMEM((1, GQA_GROUP * TQ), F32),
            pltpu.VMEM((HEAD_DIM, GQA_GROUP * TQ), F32),
            pltpu.VMEM((GQA_UNROLL, TK, GQA_GROUP * TQ), F32),
        ],
        compiler_params=_params("arbitrary", "arbitrary"),
        name="gqa",
    )(proj, proj, proj)


def _na_classes(grid_rows):
    kh = min(WIN_H_MAX, grid_rows)
    n_blocks = grid_rows // NA_ROWS
    classes = []
    for rb in (0, 1, n_blocks - 1):
        start = min(max(NA_ROWS * rb - NA_ROWS, 0), grid_rows - NA_KROWS)
        table = []
        for qi in range(NA_ROWS):
            i = NA_ROWS * rb + qi
            rs = min(max(i - kh // 2, 0), grid_rows - kh)
            row = []
            for a in range(NA_KROWS):
                r = start + a
                row.append(r - i + (WIN_H_MAX - 1) if rs <= r < rs + kh else None)
            table.append(row)
        classes.append(table)
    return classes


def _na_kernel(rpb_ref, q_ref, k_ref, v_ref, o_ref, strip_ref, bias_ref, s_ref,
               *, classes, n_blocks):
    h = pl.program_id(0)
    n_rpb_rows = 2 * WIN_H_MAX - 1
    n_rpb_cols = 2 * WIN_W - 1
    qblk = NA_ROWS * GRID_W
    kblk = NA_KROWS * GRID_W

    jj = lax.broadcasted_iota(jnp.int32, (GRID_W, 2 * GRID_W), 0)
    cc = lax.broadcasted_iota(jnp.int32, (GRID_W, 2 * GRID_W), 1) % GRID_W
    rel = cc - jj + (WIN_W - 1)
    cs = jnp.clip(jj - WIN_W // 2, 0, GRID_W - WIN_W)
    col_ok = (cc >= cs) & (cc < cs + WIN_W)
    for dr in range(n_rpb_rows):
        base = (h * n_rpb_rows + dr) * n_rpb_cols

        def pick(d, t, base=base):
            return jnp.where(rel == d, rpb_ref[base + d], t)

        strip = lax.fori_loop(0, n_rpb_cols, pick,
                              jnp.zeros((GRID_W, 2 * GRID_W), F32))
        strip_ref[dr] = jnp.where(col_ok, strip, MASK_VALUE)

    left = lax.broadcasted_iota(jnp.int32, (GRID_W, 2 * GRID_W), 1) < GRID_W
    masked = jnp.full((GRID_W, 2 * GRID_W), MASK_VALUE, F32)
    for cls, table in enumerate(classes):
        for qi in range(NA_ROWS):
            for ap in range(NA_KROWS // 2):
                dl, dr_ = table[qi][2 * ap], table[qi][2 * ap + 1]
                lhs = masked if dl is None else strip_ref[dl]
                rhs = masked if dr_ is None else strip_ref[dr_]
                bias_ref[cls, qi * GRID_W:(qi + 1) * GRID_W,
                         ap * 2 * GRID_W:(ap + 1) * 2 * GRID_W] = (
                             jnp.where(left, lhs, rhs))

    def key_start(rb):
        sb = jnp.clip(rb - 1, 0, n_blocks - NA_KROWS // NA_ROWS)
        return pl.multiple_of(sb * qblk, qblk)

    def scores(rb, slot):
        q0 = pl.multiple_of(rb * qblk, qblk)
        s_ref[slot] = lax.dot_general(q_ref[pl.ds(q0, qblk), :],
                                      k_ref[pl.ds(key_start(rb), kblk), :],
                                      (((1,), (1,)), ((), ())),
                                      preferred_element_type=F32)

    def softmax_pv(rb, slot):
        cls = jnp.where(rb == 0, 0, jnp.where(rb == n_blocks - 1, 2, 1))
        q0 = pl.multiple_of(rb * qblk, qblk)
        s = s_ref[slot] * ATTN_SCALE + bias_ref[cls]
        m = jnp.max(s, axis=-1, keepdims=True)
        p = jnp.exp(s - m)
        l = jnp.sum(p, axis=-1, keepdims=True)
        o = jnp.dot(p.astype(BF16), v_ref[pl.ds(key_start(rb), kblk), :],
                    preferred_element_type=F32)
        o_ref[pl.ds(q0, qblk), :] = o / l

    scores(0, 0)

    def body(i, carry):
        rb = NA_UNROLL * i
        for u in range(NA_UNROLL):
            scores(jnp.minimum(rb + u + 1, n_blocks - 1), (u + 1) % NA_UNROLL)
            softmax_pv(rb + u, u)
        return carry

    lax.fori_loop(0, n_blocks // NA_UNROLL, body, 0)


def _na(proj, rpb_flat):
    s = proj.shape[0]
    grid_rows = s // GRID_W
    n_blocks = grid_rows // NA_ROWS
    q_blk = (WIDTH_A + 2 * KV_WIDTH_A) // HEAD_DIM
    k_blk = q_blk + N_HEADS_B
    v_blk = k_blk + N_HEADS_B
    kern = functools.partial(_na_kernel, classes=_na_classes(grid_rows),
                             n_blocks=n_blocks)
    return pl.pallas_call(
        kern,
        grid=(N_HEADS_B,),
        in_specs=[
            pl.BlockSpec(memory_space=pltpu.SMEM),
            pl.BlockSpec((s, HEAD_DIM), lambda h: (0, q_blk + h)),
            pl.BlockSpec((s, HEAD_DIM), lambda h: (0, k_blk + h)),
            pl.BlockSpec((s, HEAD_DIM), lambda h: (0, v_blk + h)),
        ],
        out_specs=pl.BlockSpec((s, HEAD_DIM), lambda h: (0, h)),
        out_shape=jax.ShapeDtypeStruct((s, WIDTH_B), F32),
        scratch_shapes=[
            pltpu.VMEM((2 * WIN_H_MAX - 1, GRID_W, 2 * GRID_W), F32),
            pltpu.VMEM((3, NA_ROWS * GRID_W, NA_KROWS * GRID_W), F32),
            pltpu.VMEM((NA_UNROLL, NA_ROWS * GRID_W, NA_KROWS * GRID_W), F32),
        ],
        compiler_params=_params("arbitrary"),
        name="na",
    )(rpb_flat, proj, proj, proj)


def _out_proj_kernel(oa_hbm, ob_hbm, ga_ref, gb_ref, w_ref, r_ref, o_ref,
                     oa_buf, ob_buf, a_ref, sem):
    _prefetched_norm_prologue(
        [(oa_hbm, oa_buf, sem.at[0], ga_ref, 0),
         (ob_hbm, ob_buf, sem.at[1], gb_ref, oa_buf.shape[1])], a_ref)

    acc = jnp.dot(a_ref[...], w_ref[...].astype(BF16),
                  preferred_element_type=F32)
    o_ref[...] = r_ref[...] + acc


def _out_proj(oa, ob, ga, gb, w, layer, x):
    s, wa = oa.shape
    wb = ob.shape[1]
    d = w.shape[2]
    return pl.pallas_call(
        _out_proj_kernel,
        grid=(s // TM, d // TN_PROJ),
        in_specs=[
            pl.BlockSpec(memory_space=pl.ANY),
            pl.BlockSpec(memory_space=pl.ANY),
            pl.BlockSpec((1, wa), lambda i, j: (0, 0)),
            pl.BlockSpec((1, wb), lambda i, j: (0, 0)),
            pl.BlockSpec((None, wa + wb, TN_PROJ), lambda i, j: (layer, 0, j)),
            pl.BlockSpec((TM, TN_PROJ), lambda i, j: (i, j)),
        ],
        out_specs=pl.BlockSpec((TM, TN_PROJ), lambda i, j: (i, j)),
        out_shape=jax.ShapeDtypeStruct((s, d), F32),
        scratch_shapes=[pltpu.VMEM((TM, wa), F32), pltpu.VMEM((TM, wb), F32),
                        pltpu.VMEM((TM, wa + wb), BF16),
                        pltpu.SemaphoreType.DMA((2,))],
        compiler_params=_params("arbitrary", "arbitrary"),
        name="out_proj",
    )(oa, ob, ga, gb, w, x)


def _mem_kv_kernel(m_ref, g_ref, w_ref, o_ref, a_ref):
    @pl.when(pl.program_id(0) == 0)
    def _():
        _rms_norm_rows(m_ref, g_ref, a_ref)

    o_ref[...] = jnp.dot(a_ref[...], w_ref[...],
                         preferred_element_type=F32).astype(o_ref.dtype)


def _mem_kv(mem, g, wkv):
    m, d = mem.shape
    n = wkv.shape[1]
    return pl.pallas_call(
        _mem_kv_kernel,
        grid=(n // TN_PROJ,),
        in_specs=[
            pl.BlockSpec((m, d), lambda j: (0, 0)),
            pl.BlockSpec((1, d), lambda j: (0, 0)),
            pl.BlockSpec((d, TN_PROJ), lambda j: (0, j)),
        ],
        out_specs=pl.BlockSpec((m, TN_PROJ), lambda j: (0, j)),
        out_shape=jax.ShapeDtypeStruct((m, n), BF16),
        scratch_shapes=[pltpu.VMEM((m, d), BF16)],
        compiler_params=_params("arbitrary"),
        name="mem_kv",
    )(mem, g, wkv)


def _xattn_kernel(x_ref, g_ref, wq_ref, kv_ref, wo_ref, o_ref, h_ref, oc_ref):
    _rms_norm_rows(x_ref, g_ref, h_ref)
    q = jnp.dot(h_ref[...], wq_ref[...], preferred_element_type=F32)
    for hd in range(N_HEADS_MEM):
        lo, hi = hd * HEAD_DIM, (hd + 1) * HEAD_DIM
        qh = q[:, lo:hi].astype(BF16)
        kh = kv_ref[:, lo:hi]
        vh = kv_ref[:, MEM_WIDTH + lo:MEM_WIDTH + hi]
        s = lax.dot_general(qh, kh, (((1,), (1,)), ((), ())),
                            preferred_element_type=F32) * ATTN_SCALE
        m = jnp.max(s, axis=-1, keepdims=True)
        p = jnp.exp(s - m)
        l = jnp.sum(p, axis=-1, keepdims=True)
        o = jnp.dot(p.astype(BF16), vh, preferred_element_type=F32) / l
        oc_ref[:, lo:hi] = o.astype(oc_ref.dtype)
    o_ref[...] = x_ref[...] + jnp.dot(oc_ref[...], wo_ref[...],
                                      preferred_element_type=F32)


def _xattn(x, g, wq, kv, wo):
    s, d = x.shape
    m = kv.shape[0]
    return pl.pallas_call(
        _xattn_kernel,
        grid=(s // TM_X,),
        in_specs=[
            pl.BlockSpec((TM_X, d), lambda i: (i, 0)),
            pl.BlockSpec((1, d), lambda i: (0, 0)),
            pl.BlockSpec((d, MEM_WIDTH), lambda i: (0, 0)),
            pl.BlockSpec((m, 2 * MEM_WIDTH), lambda i: (0, 0)),
            pl.BlockSpec((MEM_WIDTH, d), lambda i: (0, 0)),
        ],
        out_specs=pl.BlockSpec((TM_X, d), lambda i: (i, 0)),
        out_shape=jax.ShapeDtypeStruct((s, d), F32),
        scratch_shapes=[pltpu.VMEM((TM_X, d), BF16),
                        pltpu.VMEM((TM_X, MEM_WIDTH), BF16)],
        compiler_params=_params("parallel"),
        name="xattn",
    )(x, g, wq, kv, wo)


def _final_norm_kernel(x_ref, g_ref, o_ref):
    _rms_norm_rows(x_ref, g_ref, o_ref)


def _final_norm(x, g):
    s, d = x.shape
    return pl.pallas_call(
        _final_norm_kernel,
        grid=(s // TM_FINAL,),
        in_specs=[pl.BlockSpec((TM_FINAL, d), lambda i: (i, 0)),
                  pl.BlockSpec((1, d), lambda i: (0, 0))],
        out_specs=pl.BlockSpec((TM_FINAL, d), lambda i: (i, 0)),
        out_shape=jax.ShapeDtypeStruct((s, d), F32),
        compiler_params=_params("parallel"),
        name="final_norm",
    )(x, g)


def _rope_tables(seq_len):
    t = jnp.arange(seq_len)
    row = (t // GRID_W).astype(F32)
    col = (t % GRID_W).astype(F32)
    axis_dim = HEAD_DIM // 2
    inv_freq = ROPE_THETA ** (-jnp.arange(0, axis_dim, 2, dtype=F32) / axis_dim)
    ang_r = row[:, None] * inv_freq
    ang_c = col[:, None] * inv_freq
    cos = jnp.concatenate([jnp.cos(ang_r), jnp.cos(ang_r),
                           jnp.cos(ang_c), jnp.cos(ang_c)], axis=-1)
    sin = jnp.concatenate([-jnp.sin(ang_r), jnp.sin(ang_r),
                           -jnp.sin(ang_c), jnp.sin(ang_c)], axis=-1)
    return cos, sin


def kernel(x, mem, ffn1_norm, ffn1_w_gate, ffn1_w_up, ffn1_w_down, mix_norm, w_in, q_norm_a, k_norm_a, rpb_b, out_norm_a, out_norm_b, w_out, xattn_norm, mem_norm, xattn_wq, xattn_wkv, xattn_wo, ffn2_norm, ffn2_w_gate, ffn2_w_up, ffn2_w_down, final_norm):
    batch, seq_len, d_model = x.shape
    depth = w_in.shape[0]
    assert batch == 1 and mem.shape[0] == 1
    assert seq_len % GRID_W == 0
    assert (seq_len // GRID_W) % (NA_UNROLL * NA_ROWS) == 0
    assert seq_len % (GQA_UNROLL * TK) == 0 and seq_len % TM == 0

    cos, sin = _rope_tables(seq_len)
    row = lambda v: v.reshape(1, -1)
    xs = x[0]
    mem2 = mem[0]
    for l in range(depth):
        bf = lambda w: w[l].astype(BF16)

        act = _ffn_up(xs, row(ffn1_norm[l]), ffn1_w_gate, ffn1_w_up, l)
        xs = _ffn_down(act, ffn1_w_down, l, xs)

        proj = _in_proj(xs, row(mix_norm[l]), w_in, l, row(q_norm_a[l]),
                        row(k_norm_a[l]), cos, sin)
        oa = _gqa(proj)
        ob = _na(proj, rpb_b[l].reshape(-1))
        xs = _out_proj(oa, ob, row(out_norm_a[l]), row(out_norm_b[l]),
                       w_out, l, xs)

        kv = _mem_kv(mem2, row(mem_norm[l]), bf(xattn_wkv))
        xs = _xattn(xs, row(xattn_norm[l]), bf(xattn_wq), kv, bf(xattn_wo))

        act = _ffn_up(xs, row(ffn2_norm[l]), ffn2_w_gate, ffn2_w_up, l)
        xs = _ffn_down(act, ffn2_w_down, l, xs)

    return _final_norm(xs, row(final_norm))[None]
```

```python
import functools

import jax
import jax.numpy as jnp
import numpy as np
from jax import lax
from jax.experimental import pallas as pl
from jax.experimental.pallas import tpu as pltpu

F32 = jnp.float32
BF16 = jnp.bfloat16

HEAD_DIM = 128
N_HEADS_A = 16
N_KV_A = 4
GQA_GROUP = N_HEADS_A // N_KV_A
N_HEADS_B = 16
WIDTH_A = N_HEADS_A * HEAD_DIM
WIDTH_B = N_HEADS_B * HEAD_DIM
KV_WIDTH_A = N_KV_A * HEAD_DIM
GRID_W = 64
WIN_H_MAX = 8
WIN_W = 16
N_HEADS_MEM = 4
MEM_WIDTH = N_HEADS_MEM * HEAD_DIM
ROPE_THETA = 10000.0
EPS = 1e-6
ATTN_SCALE = HEAD_DIM ** -0.5
LOG2_E = 1.4426950408889634
MASK_VALUE = -1e30

V7X_VMEM_BYTES = 64 * 1024 * 1024
VMEM_LIMIT_BYTES = V7X_VMEM_BYTES - 6 * 1024 * 1024

TM = 1024
TN_UP = 256
TN_DOWN = 512
KSPLIT_DOWN = 2
TN_PROJ = 512
NORM_COLS = 1024
NORM_CHUNK = 256
TQ = 256
TK = 512
GQA_UNROLL = 4
NA_ROWS = 4
NA_KROWS = 12
NA_UNROLL = 4
TM_X = 256
TM_FINAL = 512


def _params(*sem):
    return pltpu.CompilerParams(dimension_semantics=sem,
                                vmem_limit_bytes=VMEM_LIMIT_BYTES)


def _rms_norm_chunk(x_ref, g_ref, a_ref, r, chunk, col_off=0):
    d = x_ref.shape[1]
    cb = min(NORM_COLS, d)
    ss = jnp.zeros((chunk, 1), F32)
    for c0 in range(0, d, cb):
        xb = x_ref[pl.ds(r, chunk), c0:c0 + cb].astype(F32)
        ss = ss + jnp.sum(xb * xb, axis=-1, keepdims=True)
    inv = lax.rsqrt(ss * (1.0 / d) + EPS)
    for c0 in range(0, d, cb):
        xb = x_ref[pl.ds(r, chunk), c0:c0 + cb].astype(F32)
        y = (xb * inv) * g_ref[:, c0:c0 + cb]
        a_ref[pl.ds(r, chunk),
              col_off + c0:col_off + c0 + cb] = y.astype(a_ref.dtype)


def _rms_norm_rows(x_ref, g_ref, a_ref, col_off=0):
    tm = x_ref.shape[0]
    chunk = min(NORM_CHUNK, tm)

    def body(c, carry):
        _rms_norm_chunk(x_ref, g_ref, a_ref, pl.multiple_of(c * chunk, chunk),
                        chunk, col_off)
        return carry

    lax.fori_loop(0, tm // chunk, body, 0)


def _row_tile_copy(x_hbm, x_buf, sem, tile):
    rows = x_buf.shape[0]
    r = pl.multiple_of(tile * rows, rows)
    return pltpu.make_async_copy(x_hbm.at[pl.ds(r, rows), :], x_buf, sem)


def _normed_matmul_steps(sources, a_ref, compute_rows, first_step_rows=None):
    i = pl.program_id(0)
    j = pl.program_id(1)
    tm = a_ref.shape[0]
    chunk = min(NORM_CHUNK, tm)

    @pl.when(j == 0)
    def _():
        @pl.when(i == 0)
        def _():
            for x_hbm, x_buf, sem, _, _ in sources:
                _row_tile_copy(x_hbm, x_buf, sem, 0).start()

        for x_hbm, x_buf, sem, _, _ in sources:
            _row_tile_copy(x_hbm, x_buf, sem, i).wait()
        for r0 in range(0, tm, chunk):
            for _, x_buf, _, g_ref, col_off in sources:
                _rms_norm_chunk(x_buf, g_ref, a_ref, r0, chunk, col_off)
            (first_step_rows or compute_rows)(r0, chunk)

    @pl.when((j == 1) & (i + 1 < pl.num_programs(0)))
    def _():
        for x_hbm, x_buf, sem, _, _ in sources:
            _row_tile_copy(x_hbm, x_buf, sem, i + 1).start()

    @pl.when(j > 0)
    def _():
        compute_rows(0, tm)


def _ffn_up_kernel(x_hbm, g_ref, wg_ref, wu_ref, o_ref, x_buf, a_ref, sem):
    def swiglu_rows(r0, rows):
        a = a_ref[r0:r0 + rows, :]
        gate = jnp.dot(a, wg_ref[...].astype(BF16), preferred_element_type=F32)
        up = jnp.dot(a, wu_ref[...].astype(BF16), preferred_element_type=F32)
        o_ref[r0:r0 + rows, :] = (
            (gate * jax.nn.sigmoid(gate)) * up).astype(o_ref.dtype)

    _normed_matmul_steps([(x_hbm, x_buf, sem.at[0], g_ref, 0)], a_ref,
                         swiglu_rows)


def _ffn_up(x, g, wg, wu, layer):
    s, d = x.shape
    dff = wg.shape[2]
    return pl.pallas_call(
        _ffn_up_kernel,
        grid=(s // TM, dff // TN_UP),
        in_specs=[
            pl.BlockSpec(memory_space=pl.ANY),
            pl.BlockSpec((1, d), lambda i, j: (0, 0)),
            pl.BlockSpec((None, d, TN_UP), lambda i, j: (layer, 0, j)),
            pl.BlockSpec((None, d, TN_UP), lambda i, j: (layer, 0, j)),
        ],
        out_specs=pl.BlockSpec((TM, TN_UP), lambda i, j: (i, j)),
        out_shape=jax.ShapeDtypeStruct((s, dff), BF16),
        scratch_shapes=[pltpu.VMEM((TM, d), F32), pltpu.VMEM((TM, d), BF16),
                        pltpu.SemaphoreType.DMA((1,))],
        compiler_params=_params("arbitrary", "arbitrary"),
        name="ffn_up",
    )(x, g, wg, wu)


def _ffn_down_kernel(act_hbm, w_ref, r_ref, o_ref, a0_buf, a1_buf, sem):
    i = pl.program_id(0)
    j = pl.program_id(1)
    k = pl.program_id(2)
    n_i = pl.num_programs(0)
    n_j = pl.num_programs(1)
    tm, tk = a0_buf.shape
    bufs = (a0_buf, a1_buf)

    def half_copy(tile, half):
        r = pl.multiple_of(tile * tm, tm)
        return pltpu.make_async_copy(
            act_hbm.at[pl.ds(r, tm), pl.ds(half * tk, tk)], bufs[half],
            sem.at[half])

    @pl.when((j == 0) & (k == 0))
    def _():
        @pl.when(i == 0)
        def _():
            half_copy(0, 0).start()

        half_copy(i, 1).start()
        half_copy(i, 0).wait()

    @pl.when((j == 0) & (k == 1))
    def _():
        half_copy(i, 1).wait()

    @pl.when((j == n_j - 1) & (k == 1) & (i + 1 < n_i))
    def _():
        half_copy(i + 1, 0).start()

    for half in range(KSPLIT_DOWN):
        @pl.when(k == half)
        def _(half=half):
            part = 0.5 * jnp.dot(bufs[half][...], w_ref[...].astype(BF16),
                                 preferred_element_type=F32)
            if half == 0:
                o_ref[...] = r_ref[...] + part
            else:
                o_ref[...] += part


def _ffn_down(act, wd, layer, x):
    s, dff = act.shape
    d = wd.shape[2]
    assert KSPLIT_DOWN == 2 and d // TN_DOWN >= 2
    tk = dff // KSPLIT_DOWN
    return pl.pallas_call(
        _ffn_down_kernel,
        grid=(s // TM, d // TN_DOWN, KSPLIT_DOWN),
        in_specs=[
            pl.BlockSpec(memory_space=pl.ANY),
            pl.BlockSpec((None, tk, TN_DOWN), lambda i, j, k: (layer, k, j)),
            pl.BlockSpec((TM, TN_DOWN), lambda i, j, k: (i, j)),
        ],
        out_specs=pl.BlockSpec((TM, TN_DOWN), lambda i, j, k: (i, j)),
        out_shape=jax.ShapeDtypeStruct((s, d), F32),
        scratch_shapes=[pltpu.VMEM((TM, tk), BF16), pltpu.VMEM((TM, tk), BF16),
                        pltpu.SemaphoreType.DMA((KSPLIT_DOWN,))],
        compiler_params=_params("arbitrary", "arbitrary", "arbitrary"),
        name="ffn_down",
    )(act, wd, x)


def _head_pair_matrices():
    lane = np.arange(2 * HEAD_DIM)
    head = lane // HEAD_DIM
    ones = (head[:, None] == head[None, :]).astype(np.float32)
    first_half = (lane % (HEAD_DIM // 2)) < (HEAD_DIM // 4)
    partner = np.where(first_half, lane + HEAD_DIM // 4, lane - HEAD_DIM // 4)
    perm = np.zeros((2 * HEAD_DIM, 2 * HEAD_DIM), np.float32)
    perm[partner, lane] = 1.0
    return jnp.asarray(ones, BF16), jnp.asarray(perm, BF16)


def _in_proj_kernel(x_hbm, g_ref, w_ref, qg_ref, kg_ref, cos_ref, sin_ref,
                    ones_ref, perm_ref, o_ref, x_buf, a_ref, sem):
    j = pl.program_id(1)
    n_q_tiles = WIDTH_A // TN_PROJ
    n_qk_tiles = (WIDTH_A + KV_WIDTH_A) // TN_PROJ
    def project(r0, rows):
        return jnp.dot(a_ref[r0:r0 + rows, :], w_ref[...].astype(BF16),
                       preferred_element_type=F32)

    def qk_rows(r0, rows):
        acc = project(r0, rows)
        is_q = j < n_q_tiles
        gain = jnp.where(is_q, qg_ref[...], kg_ref[...])
        gain = jnp.concatenate([gain, gain], axis=1)
        scale = jnp.where(is_q, ATTN_SCALE * LOG2_E, 1.0).astype(F32)
        cos = cos_ref[r0:r0 + rows, :]
        sin = sin_ref[r0:r0 + rows, :]
        cos = jnp.concatenate([cos, cos], axis=1)
        sin = jnp.concatenate([sin, sin], axis=1)
        pair = 2 * HEAD_DIM
        for hp in range(TN_PROJ // pair):
            xh = acc[:, hp * pair:(hp + 1) * pair]
            ss = jnp.dot((xh * xh).astype(BF16), ones_ref[...],
                         preferred_element_type=F32)
            y = (xh * lax.rsqrt(ss * (1.0 / HEAD_DIM) + EPS)) * gain
            partner = jnp.dot(y.astype(BF16), perm_ref[...],
                              preferred_element_type=F32)
            out = (y * cos + partner * sin) * scale
            o_ref[r0:r0 + rows, hp * pair:(hp + 1) * pair] = (
                out.astype(o_ref.dtype))

    def later_steps(r0, rows):
        @pl.when(j < n_qk_tiles)
        def _():
            qk_rows(r0, rows)

        @pl.when(j >= n_qk_tiles)
        def _():
            o_ref[r0:r0 + rows, :] = project(r0, rows).astype(o_ref.dtype)

    _normed_matmul_steps([(x_hbm, x_buf, sem.at[0], g_ref, 0)], a_ref,
                         later_steps, first_step_rows=qk_rows)


def _in_proj(x, g, w, layer, qg, kg, cos, sin):
    s, d = x.shape
    n = w.shape[2]
    ones, perm = _head_pair_matrices()
    pair = 2 * HEAD_DIM
    return pl.pallas_call(
        _in_proj_kernel,
        grid=(s // TM, n // TN_PROJ),
        in_specs=[
            pl.BlockSpec(memory_space=pl.ANY),
            pl.BlockSpec((1, d), lambda i, j: (0, 0)),
            pl.BlockSpec((None, d, TN_PROJ), lambda i, j: (layer, 0, j)),
            pl.BlockSpec((1, HEAD_DIM), lambda i, j: (0, 0)),
            pl.BlockSpec((1, HEAD_DIM), lambda i, j: (0, 0)),
            pl.BlockSpec((TM, HEAD_DIM), lambda i, j: (i, 0)),
            pl.BlockSpec((TM, HEAD_DIM), lambda i, j: (i, 0)),
            pl.BlockSpec((pair, pair), lambda i, j: (0, 0)),
            pl.BlockSpec((pair, pair), lambda i, j: (0, 0)),
        ],
        out_specs=pl.BlockSpec((TM, TN_PROJ), lambda i, j: (i, j)),
        out_shape=jax.ShapeDtypeStruct((s, n), BF16),
        scratch_shapes=[pltpu.VMEM((TM, d), F32), pltpu.VMEM((TM, d), BF16),
                        pltpu.SemaphoreType.DMA((1,))],
        compiler_params=_params("arbitrary", "arbitrary"),
        name="in_proj",
    )(x, g, w, qg, kg, cos, sin, ones, perm)


def _gqa_kernel(q_ref, k_ref, v_ref, o_ref, qs_ref, vt_ref, m_ref, l_ref,
                acc_ref, st_ref):
    tq = q_ref.shape[0]
    s_len = k_ref.shape[0]
    n_chunks = s_len // TK

    @pl.when(pl.program_id(1) == 0)
    def _():
        def transpose_chunk(c, carry):
            r = pl.multiple_of(c * TK, TK)
            vc = v_ref[pl.ds(r, TK), :].astype(F32)
            vt_ref[:, pl.ds(r, TK)] = vc.T.astype(vt_ref.dtype)
            return carry

        lax.fori_loop(0, n_chunks, transpose_chunk, 0)

    for g in range(GQA_GROUP):
        qs_ref[g * tq:(g + 1) * tq, :] = q_ref[:, g * HEAD_DIM:(g + 1) * HEAD_DIM]
    m_ref[...] = jnp.full(m_ref.shape, MASK_VALUE, F32)
    l_ref[...] = jnp.zeros(l_ref.shape, F32)
    acc_ref[...] = jnp.zeros(acc_ref.shape, F32)

    def scores(c, slot):
        r = pl.multiple_of(c * TK, TK)
        st_ref[slot] = lax.dot_general(k_ref[pl.ds(r, TK), :], qs_ref[...],
                                       (((1,), (1,)), ((), ())),
                                       preferred_element_type=F32)

    def softmax_pv(c, slot):
        r = pl.multiple_of(c * TK, TK)
        for g in range(GQA_GROUP):
            cols = slice(g * tq, (g + 1) * tq)
            st = st_ref[slot, :, cols]
            m_old = m_ref[:, cols]
            m_new = jnp.maximum(m_old, jnp.max(st, axis=0, keepdims=True))
            alpha = jnp.exp2(m_old - m_new)
            p = jnp.exp2(st - m_new)
            l_ref[:, cols] = alpha * l_ref[:, cols] + jnp.sum(p, axis=0,
                                                              keepdims=True)
            pv = jnp.dot(vt_ref[:, pl.ds(r, TK)], p.astype(BF16),
                         preferred_element_type=F32)
            acc_ref[:, cols] = alpha * acc_ref[:, cols] + pv
            m_ref[:, cols] = m_new

    scores(0, 0)

    def body(i, carry):
        c = GQA_UNROLL * i
        for u in range(GQA_UNROLL):
            scores(jnp.minimum(c + u + 1, n_chunks - 1), (u + 1) % GQA_UNROLL)
            softmax_pv(c + u, u)
        return carry

    lax.fori_loop(0, n_chunks // GQA_UNROLL, body, 0)
    out = (acc_ref[...] / l_ref[...]).T
    for g in range(GQA_GROUP):
        o_ref[:, g * HEAD_DIM:(g + 1) * HEAD_DIM] = out[g * tq:(g + 1) * tq]


def _gqa(proj):
    s = proj.shape[0]
    k_blk = WIDTH_A // HEAD_DIM
    v_blk = (WIDTH_A + KV_WIDTH_A) // HEAD_DIM
    gw = GQA_GROUP * HEAD_DIM
    return pl.pallas_call(
        _gqa_kernel,
        grid=(N_KV_A, s // TQ),
        in_specs=[
            pl.BlockSpec((TQ, gw), lambda h, i: (i, h)),
            pl.BlockSpec((s, HEAD_DIM), lambda h, i: (0, k_blk + h)),
            pl.BlockSpec((s, HEAD_DIM), lambda h, i: (0, v_blk + h)),
        ],
        out_specs=pl.BlockSpec((TQ, gw), lambda h, i: (i, h)),
        out_shape=jax.ShapeDtypeStruct((s, WIDTH_A), F32),
        scratch_shapes=[
            pltpu.VMEM((GQA_GROUP * TQ, HEAD_DIM), BF16),
            pltpu.VMEM((HEAD_DIM, s), BF16),
            pltpu.VMEM((1, GQA_GROUP * TQ), F32),
            pltpu.VMEM((1, GQA_GROUP * TQ), F32),
            pltpu.VMEM((HEAD_DIM, GQA_GROUP * TQ), F32),
            pltpu.VMEM((GQA_UNROLL, TK, GQA_GROUP * TQ), F32),
        ],
        compiler_params=_params("arbitrary", "arbitrary"),
        name="gqa",
    )(proj, proj, proj)


def _na_classes(grid_rows):
    kh = min(WIN_H_MAX, grid_rows)
    n_blocks = grid_rows // NA_ROWS
    classes = []
    for rb in (0, 1, n_blocks - 1):
        start = min(max(NA_ROWS * rb - NA_ROWS, 0), grid_rows - NA_KROWS)
        table = []
        for qi in range(NA_ROWS):
            i = NA_ROWS * rb + qi
            rs = min(max(i - kh // 2, 0), grid_rows - kh)
            row = []
            for a in range(NA_KROWS):
                r = start + a
                row.append(r - i + (WIN_H_MAX - 1) if rs <= r < rs + kh else None)
            table.append(row)
        classes.append(table)
    return classes


def _na_kernel(rpb_ref, q_ref, k_ref, v_ref, o_ref, strip_ref, bias_ref, s_ref,
               *, classes, n_blocks):
    h = pl.program_id(0)
    n_rpb_rows = 2 * WIN_H_MAX - 1
    n_rpb_cols = 2 * WIN_W - 1
    qblk = NA_ROWS * GRID_W
    kblk = NA_KROWS * GRID_W

    jj = lax.broadcasted_iota(jnp.int32, (GRID_W, 2 * GRID_W), 0)
    cc = lax.broadcasted_iota(jnp.int32, (GRID_W, 2 * GRID_W), 1) % GRID_W
    rel = cc - jj + (WIN_W - 1)
    cs = jnp.clip(jj - WIN_W // 2, 0, GRID_W - WIN_W)
    col_ok = (cc >= cs) & (cc < cs + WIN_W)
    for dr in range(n_rpb_rows):
        base = (h * n_rpb_rows + dr) * n_rpb_cols

        def pick(d, t, base=base):
            return jnp.where(rel == d, rpb_ref[base + d], t)

        strip = lax.fori_loop(0, n_rpb_cols, pick,
                              jnp.zeros((GRID_W, 2 * GRID_W), F32))
        strip_ref[dr] = jnp.where(col_ok, strip, MASK_VALUE)

    left = lax.broadcasted_iota(jnp.int32, (GRID_W, 2 * GRID_W), 1) < GRID_W
    masked = jnp.full((GRID_W, 2 * GRID_W), MASK_VALUE, F32)
    for cls, table in enumerate(classes):
        for qi in range(NA_ROWS):
            for ap in range(NA_KROWS // 2):
                dl, dr_ = table[qi][2 * ap], table[qi][2 * ap + 1]
                lhs = masked if dl is None else strip_ref[dl]
                rhs = masked if dr_ is None else strip_ref[dr_]
                bias_ref[cls, qi * GRID_W:(qi + 1) * GRID_W,
                         ap * 2 * GRID_W:(ap + 1) * 2 * GRID_W] = (
                             jnp.where(left, lhs, rhs))

    def key_start(rb):
        sb = jnp.clip(rb - 1, 0, n_blocks - NA_KROWS // NA_ROWS)
        return pl.multiple_of(sb * qblk, qblk)

    def scores(rb, slot):
        q0 = pl.multiple_of(rb * qblk, qblk)
        s_ref[slot] = lax.dot_general(q_ref[pl.ds(q0, qblk), :],
                                      k_ref[pl.ds(key_start(rb), kblk), :],
                                      (((1,), (1,)), ((), ())),
                                      preferred_element_type=F32)

    def softmax_pv(rb, slot):
        cls = jnp.where(rb == 0, 0, jnp.where(rb == n_blocks - 1, 2, 1))
        q0 = pl.multiple_of(rb * qblk, qblk)
        s = s_ref[slot] * ATTN_SCALE + bias_ref[cls]
        m = jnp.max(s, axis=-1, keepdims=True)
        p = jnp.exp(s - m)
        l = jnp.sum(p, axis=-1, keepdims=True)
        o = jnp.dot(p.astype(BF16), v_ref[pl.ds(key_start(rb), kblk), :],
                    preferred_element_type=F32)
        o_ref[pl.ds(q0, qblk), :] = o / l

    scores(0, 0)

    def body(i, carry):
        rb = NA_UNROLL * i
        for u in range(NA_UNROLL):
            scores(jnp.minimum(rb + u + 1, n_blocks - 1), (u + 1) % NA_UNROLL)
            softmax_pv(rb + u, u)
        return carry

    lax.fori_loop(0, n_blocks // NA_UNROLL, body, 0)


def _na(proj, rpb_flat):
    s = proj.shape[0]
    grid_rows = s // GRID_W
    n_blocks = grid_rows // NA_ROWS
    q_blk = (WIDTH_A + 2 * KV_WIDTH_A) // HEAD_DIM
    k_blk = q_blk + N_HEADS_B
    v_blk = k_blk + N_HEADS_B
    kern = functools.partial(_na_kernel, classes=_na_classes(grid_rows),
                             n_blocks=n_blocks)
    return pl.pallas_call(
        kern,
        grid=(N_HEADS_B,),
        in_specs=[
            pl.BlockSpec(memory_space=pltpu.SMEM),
            pl.BlockSpec((s, HEAD_DIM), lambda h: (0, q_blk + h)),
            pl.BlockSpec((s, HEAD_DIM), lambda h: (0, k_blk + h)),
            pl.BlockSpec((s, HEAD_DIM), lambda h: (0, v_blk + h)),
        ],
        out_specs=pl.BlockSpec((s, HEAD_DIM), lambda h: (0, h)),
        out_shape=jax.ShapeDtypeStruct((s, WIDTH_B), F32),
        scratch_shapes=[
            pltpu.VMEM((2 * WIN_H_MAX - 1, GRID_W, 2 * GRID_W), F32),
            pltpu.VMEM((3, NA_ROWS * GRID_W, NA_KROWS * GRID_W), F32),
            pltpu.VMEM((NA_UNROLL, NA_ROWS * GRID_W, NA_KROWS * GRID_W), F32),
        ],
        compiler_params=_params("arbitrary"),
        name="na",
    )(rpb_flat, proj, proj, proj)


def _out_proj_kernel(oa_hbm, ob_hbm, ga_ref, gb_ref, w_ref, r_ref, o_ref,
                     oa_buf, ob_buf, a_ref, sem):
    def project_rows(r0, rows):
        acc = jnp.dot(a_ref[r0:r0 + rows, :], w_ref[...].astype(BF16),
                      preferred_element_type=F32)
        o_ref[r0:r0 + rows, :] = r_ref[r0:r0 + rows, :] + acc

    _normed_matmul_steps(
        [(oa_hbm, oa_buf, sem.at[0], ga_ref, 0),
         (ob_hbm, ob_buf, sem.at[1], gb_ref, oa_buf.shape[1])], a_ref,
        project_rows)


def _out_proj(oa, ob, ga, gb, w, layer, x):
    s, wa = oa.shape
    wb = ob.shape[1]
    d = w.shape[2]
    return pl.pallas_call(
        _out_proj_kernel,
        grid=(s // TM, d // TN_PROJ),
        in_specs=[
            pl.BlockSpec(memory_space=pl.ANY),
            pl.BlockSpec(memory_space=pl.ANY),
            pl.BlockSpec((1, wa), lambda i, j: (0, 0)),
            pl.BlockSpec((1, wb), lambda i, j: (0, 0)),
            pl.BlockSpec((None, wa + wb, TN_PROJ), lambda i, j: (layer, 0, j)),
            pl.BlockSpec((TM, TN_PROJ), lambda i, j: (i, j)),
        ],
        out_specs=pl.BlockSpec((TM, TN_PROJ), lambda i, j: (i, j)),
        out_shape=jax.ShapeDtypeStruct((s, d), F32),
        scratch_shapes=[pltpu.VMEM((TM, wa), F32), pltpu.VMEM((TM, wb), F32),
                        pltpu.VMEM((TM, wa + wb), BF16),
                        pltpu.SemaphoreType.DMA((2,))],
        compiler_params=_params("arbitrary", "arbitrary"),
        name="out_proj",
    )(oa, ob, ga, gb, w, x)


def _mem_kv_kernel(m_ref, g_ref, w_ref, o_ref, a_ref):
    @pl.when(pl.program_id(0) == 0)
    def _():
        _rms_norm_rows(m_ref, g_ref, a_ref)

    o_ref[...] = jnp.dot(a_ref[...], w_ref[...],
                         preferred_element_type=F32).astype(o_ref.dtype)


def _mem_kv(mem, g, wkv):
    m, d = mem.shape
    n = wkv.shape[1]
    return pl.pallas_call(
        _mem_kv_kernel,
        grid=(n // TN_PROJ,),
        in_specs=[
            pl.BlockSpec((m, d), lambda j: (0, 0)),
            pl.BlockSpec((1, d), lambda j: (0, 0)),
            pl.BlockSpec((d, TN_PROJ), lambda j: (0, j)),
        ],
        out_specs=pl.BlockSpec((m, TN_PROJ), lambda j: (0, j)),
        out_shape=jax.ShapeDtypeStruct((m, n), BF16),
        scratch_shapes=[pltpu.VMEM((m, d), BF16)],
        compiler_params=_params("arbitrary"),
        name="mem_kv",
    )(mem, g, wkv)


def _xattn_kernel(x_ref, g_ref, wq_ref, kv_ref, wo_ref, o_ref, h_ref, oc_ref):
    _rms_norm_rows(x_ref, g_ref, h_ref)
    q = jnp.dot(h_ref[...], wq_ref[...], preferred_element_type=F32)
    for hd in range(N_HEADS_MEM):
        lo, hi = hd * HEAD_DIM, (hd + 1) * HEAD_DIM
        qh = q[:, lo:hi].astype(BF16)
        kh = kv_ref[:, lo:hi]
        vh = kv_ref[:, MEM_WIDTH + lo:MEM_WIDTH + hi]
        s = lax.dot_general(qh, kh, (((1,), (1,)), ((), ())),
                            preferred_element_type=F32) * ATTN_SCALE
        m = jnp.max(s, axis=-1, keepdims=True)
        p = jnp.exp(s - m)
        l = jnp.sum(p, axis=-1, keepdims=True)
        o = jnp.dot(p.astype(BF16), vh, preferred_element_type=F32) / l
        oc_ref[:, lo:hi] = o.astype(oc_ref.dtype)
    o_ref[...] = x_ref[...] + jnp.dot(oc_ref[...], wo_ref[...],
                                      preferred_element_type=F32)


def _xattn(x, g, wq, kv, wo):
    s, d = x.shape
    m = kv.shape[0]
    return pl.pallas_call(
        _xattn_kernel,
        grid=(s // TM_X,),
        in_specs=[
            pl.BlockSpec((TM_X, d), lambda i: (i, 0)),
            pl.BlockSpec((1, d), lambda i: (0, 0)),
            pl.BlockSpec((d, MEM_WIDTH), lambda i: (0, 0)),
            pl.BlockSpec((m, 2 * MEM_WIDTH), lambda i: (0, 0)),
            pl.BlockSpec((MEM_WIDTH, d), lambda i: (0, 0)),
        ],
        out_specs=pl.BlockSpec((TM_X, d), lambda i: (i, 0)),
        out_shape=jax.ShapeDtypeStruct((s, d), F32),
        scratch_shapes=[pltpu.VMEM((TM_X, d), BF16),
                        pltpu.VMEM((TM_X, MEM_WIDTH), BF16)],
        compiler_params=_params("parallel"),
        name="xattn",
    )(x, g, wq, kv, wo)


def _final_norm_kernel(x_ref, g_ref, o_ref):
    _rms_norm_rows(x_ref, g_ref, o_ref)


def _final_norm(x, g):
    s, d = x.shape
    return pl.pallas_call(
        _final_norm_kernel,
        grid=(s // TM_FINAL,),
        in_specs=[pl.BlockSpec((TM_FINAL, d), lambda i: (i, 0)),
                  pl.BlockSpec((1, d), lambda i: (0, 0))],
        out_specs=pl.BlockSpec((TM_FINAL, d), lambda i: (i, 0)),
        out_shape=jax.ShapeDtypeStruct((s, d), F32),
        compiler_params=_params("parallel"),
        name="final_norm",
    )(x, g)


def _rope_tables(seq_len):
    t = jnp.arange(seq_len)
    row = (t // GRID_W).astype(F32)
    col = (t % GRID_W).astype(F32)
    axis_dim = HEAD_DIM // 2
    inv_freq = ROPE_THETA ** (-jnp.arange(0, axis_dim, 2, dtype=F32) / axis_dim)
    ang_r = row[:, None] * inv_freq
    ang_c = col[:, None] * inv_freq
    cos = jnp.concatenate([jnp.cos(ang_r), jnp.cos(ang_r),
                           jnp.cos(ang_c), jnp.cos(ang_c)], axis=-1)
    sin = jnp.concatenate([-jnp.sin(ang_r), jnp.sin(ang_r),
                           -jnp.sin(ang_c), jnp.sin(ang_c)], axis=-1)
    return cos, sin


def kernel(x, mem, ffn1_norm, ffn1_w_gate, ffn1_w_up, ffn1_w_down, mix_norm, w_in, q_norm_a, k_norm_a, rpb_b, out_norm_a, out_norm_b, w_out, xattn_norm, mem_norm, xattn_wq, xattn_wkv, xattn_wo, ffn2_norm, ffn2_w_gate, ffn2_w_up, ffn2_w_down, final_norm):
    batch, seq_len, d_model = x.shape
    depth = w_in.shape[0]
    assert batch == 1 and mem.shape[0] == 1
    assert seq_len % GRID_W == 0
    assert (seq_len // GRID_W) % (NA_UNROLL * NA_ROWS) == 0
    assert seq_len % (GQA_UNROLL * TK) == 0 and seq_len % TM == 0

    cos, sin = _rope_tables(seq_len)
    row = lambda v: v.reshape(1, -1)
    xs = x[0]
    mem2 = mem[0]
    for l in range(depth):
        bf = lambda w: w[l].astype(BF16)

        act = _ffn_up(xs, row(ffn1_norm[l]), ffn1_w_gate, ffn1_w_up, l)
        xs = _ffn_down(act, ffn1_w_down, l, xs)

        proj = _in_proj(xs, row(mix_norm[l]), w_in, l, row(q_norm_a[l]),
                        row(k_norm_a[l]), cos, sin)
        oa = _gqa(proj)
        ob = _na(proj, rpb_b[l].reshape(-1))
        xs = _out_proj(oa, ob, row(out_norm_a[l]), row(out_norm_b[l]),
                       w_out, l, xs)

        kv = _mem_kv(mem2, row(mem_norm[l]), bf(xattn_wkv))
        xs = _xattn(xs, row(xattn_norm[l]), bf(xattn_wq), kv, bf(xattn_wo))

        act = _ffn_up(xs, row(ffn2_norm[l]), ffn2_w_gate, ffn2_w_up, l)
        xs = _ffn_down(act, ffn2_w_down, l, xs)

    return _final_norm(xs, row(final_norm))[None]
```

```python
import functools

import jax
import jax.numpy as jnp
import numpy as np
from jax import lax
from jax.experimental import pallas as pl
from jax.experimental.pallas import tpu as pltpu

F32 = jnp.float32
BF16 = jnp.bfloat16

HEAD_DIM = 128
N_HEADS_A = 16
N_KV_A = 4
GQA_GROUP = N_HEADS_A // N_KV_A
N_HEADS_B = 16
WIDTH_A = N_HEADS_A * HEAD_DIM
WIDTH_B = N_HEADS_B * HEAD_DIM
KV_WIDTH_A = N_KV_A * HEAD_DIM
GRID_W = 64
WIN_H_MAX = 8
WIN_W = 16
N_HEADS_MEM = 4
MEM_WIDTH = N_HEADS_MEM * HEAD_DIM
ROPE_THETA = 10000.0
EPS = 1e-6
ATTN_SCALE = HEAD_DIM ** -0.5
LOG2_E = 1.4426950408889634
MASK_VALUE = -1e30

V7X_VMEM_BYTES = 64 * 1024 * 1024
VMEM_LIMIT_BYTES = V7X_VMEM_BYTES - 6 * 1024 * 1024

TM = 1024
TN_UP = 256
TN_DOWN = 512
KSPLIT_DOWN = 2
TN_PROJ = 512
NORM_COLS = 1024
NORM_CHUNK = 256
TQ = 256
TK = 512
GQA_UNROLL = 4
NA_ROWS = 4
NA_KROWS = 12
NA_UNROLL = 4
TM_X = 256
TM_FINAL = 512


def _params(*sem):
    return pltpu.CompilerParams(dimension_semantics=sem,
                                vmem_limit_bytes=VMEM_LIMIT_BYTES)


def _rms_norm_chunk(x_ref, g_ref, a_ref, r, chunk, col_off=0):
    d = x_ref.shape[1]
    cb = min(NORM_COLS, d)
    ss = jnp.zeros((chunk, 1), F32)
    for c0 in range(0, d, cb):
        xb = x_ref[pl.ds(r, chunk), c0:c0 + cb].astype(F32)
        ss = ss + jnp.sum(xb * xb, axis=-1, keepdims=True)
    inv = lax.rsqrt(ss * (1.0 / d) + EPS)
    for c0 in range(0, d, cb):
        xb = x_ref[pl.ds(r, chunk), c0:c0 + cb].astype(F32)
        y = (xb * inv) * g_ref[:, c0:c0 + cb]
        a_ref[pl.ds(r, chunk),
              col_off + c0:col_off + c0 + cb] = y.astype(a_ref.dtype)


def _rms_norm_rows(x_ref, g_ref, a_ref, col_off=0):
    tm = x_ref.shape[0]
    chunk = min(NORM_CHUNK, tm)

    def body(c, carry):
        _rms_norm_chunk(x_ref, g_ref, a_ref, pl.multiple_of(c * chunk, chunk),
                        chunk, col_off)
        return carry

    lax.fori_loop(0, tm // chunk, body, 0)


def _row_tile_copy(x_hbm, x_buf, sem, tile):
    rows = x_buf.shape[0]
    r = pl.multiple_of(tile * rows, rows)
    return pltpu.make_async_copy(x_hbm.at[pl.ds(r, rows), :], x_buf, sem)


def _normed_matmul_steps(sources, a_ref, compute_rows, first_step_rows=None):
    i = pl.program_id(0)
    j = pl.program_id(1)
    tm = a_ref.shape[0]
    chunk = min(NORM_CHUNK, tm)

    @pl.when(j == 0)
    def _():
        @pl.when(i == 0)
        def _():
            for x_hbm, x_buf, sem, _, _ in sources:
                _row_tile_copy(x_hbm, x_buf, sem, 0).start()

        for x_hbm, x_buf, sem, _, _ in sources:
            _row_tile_copy(x_hbm, x_buf, sem, i).wait()
        for r0 in range(0, tm, chunk):
            for _, x_buf, _, g_ref, col_off in sources:
                _rms_norm_chunk(x_buf, g_ref, a_ref, r0, chunk, col_off)
            (first_step_rows or compute_rows)(r0, chunk)

    @pl.when((j == 1) & (i + 1 < pl.num_programs(0)))
    def _():
        for x_hbm, x_buf, sem, _, _ in sources:
            _row_tile_copy(x_hbm, x_buf, sem, i + 1).start()

    @pl.when(j > 0)
    def _():
        compute_rows(0, tm)


def _ffn_up_kernel(x_hbm, g_ref, wg_ref, wu_ref, o_ref, x_buf, a_ref, sem):
    def swiglu_rows(r0, rows):
        a = a_ref[r0:r0 + rows, :]
        gate = jnp.dot(a, wg_ref[...].astype(BF16), preferred_element_type=F32)
        up = jnp.dot(a, wu_ref[...].astype(BF16), preferred_element_type=F32)
        o_ref[r0:r0 + rows, :] = (
            (gate * jax.nn.sigmoid(gate)) * up).astype(o_ref.dtype)

    _normed_matmul_steps([(x_hbm, x_buf, sem.at[0], g_ref, 0)], a_ref,
                         swiglu_rows)


def _ffn_up(x, g, wg, wu, layer):
    s, d = x.shape
    dff = wg.shape[2]
    return pl.pallas_call(
        _ffn_up_kernel,
        grid=(s // TM, dff // TN_UP),
        in_specs=[
            pl.BlockSpec(memory_space=pl.ANY),
            pl.BlockSpec((1, d), lambda i, j: (0, 0)),
            pl.BlockSpec((None, d, TN_UP), lambda i, j: (layer, 0, j)),
            pl.BlockSpec((None, d, TN_UP), lambda i, j: (layer, 0, j)),
        ],
        out_specs=pl.BlockSpec((TM, TN_UP), lambda i, j: (i, j)),
        out_shape=jax.ShapeDtypeStruct((s, dff), BF16),
        scratch_shapes=[pltpu.VMEM((TM, d), F32), pltpu.VMEM((TM, d), BF16),
                        pltpu.SemaphoreType.DMA((1,))],
        compiler_params=_params("arbitrary", "arbitrary"),
        name="ffn_up",
    )(x, g, wg, wu)


def _ffn_down_kernel(act_hbm, w_ref, r_ref, o_ref, a0_buf, a1_buf, sem):
    i = pl.program_id(0)
    j = pl.program_id(1)
    k = pl.program_id(2)
    n_i = pl.num_programs(0)
    n_j = pl.num_programs(1)
    tm, tk = a0_buf.shape
    bufs = (a0_buf, a1_buf)

    def half_copy(tile, half):
        r = pl.multiple_of(tile * tm, tm)
        return pltpu.make_async_copy(
            act_hbm.at[pl.ds(r, tm), pl.ds(half * tk, tk)], bufs[half],
            sem.at[half])

    @pl.when((j == 0) & (k == 0))
    def _():
        @pl.when(i == 0)
        def _():
            half_copy(0, 0).start()

        half_copy(i, 1).start()
        half_copy(i, 0).wait()

    @pl.when((j == 0) & (k == 1))
    def _():
        half_copy(i, 1).wait()

    @pl.when((j == n_j - 1) & (k == 1) & (i + 1 < n_i))
    def _():
        half_copy(i + 1, 0).start()

    for half in range(KSPLIT_DOWN):
        @pl.when(k == half)
        def _(half=half):
            part = 0.5 * jnp.dot(bufs[half][...], w_ref[...].astype(BF16),
                                 preferred_element_type=F32)
            if half == 0:
                o_ref[...] = r_ref[...] + part
            else:
                o_ref[...] += part


def _ffn_down(act, wd, layer, x):
    s, dff = act.shape
    d = wd.shape[2]
    assert KSPLIT_DOWN == 2 and d // TN_DOWN >= 2
    tk = dff // KSPLIT_DOWN
    return pl.pallas_call(
        _ffn_down_kernel,
        grid=(s // TM, d // TN_DOWN, KSPLIT_DOWN),
        in_specs=[
            pl.BlockSpec(memory_space=pl.ANY),
            pl.BlockSpec((None, tk, TN_DOWN), lambda i, j, k: (layer, k, j)),
            pl.BlockSpec((TM, TN_DOWN), lambda i, j, k: (i, j)),
        ],
        out_specs=pl.BlockSpec((TM, TN_DOWN), lambda i, j, k: (i, j)),
        out_shape=jax.ShapeDtypeStruct((s, d), F32),
        scratch_shapes=[pltpu.VMEM((TM, tk), BF16), pltpu.VMEM((TM, tk), BF16),
                        pltpu.SemaphoreType.DMA((KSPLIT_DOWN,))],
        compiler_params=_params("arbitrary", "arbitrary", "arbitrary"),
        name="ffn_down",
    )(act, wd, x)


def _head_pair_matrices():
    lane = np.arange(2 * HEAD_DIM)
    head = lane // HEAD_DIM
    ones = (head[:, None] == head[None, :]).astype(np.float32)
    first_half = (lane % (HEAD_DIM // 2)) < (HEAD_DIM // 4)
    partner = np.where(first_half, lane + HEAD_DIM // 4, lane - HEAD_DIM // 4)
    perm = np.zeros((2 * HEAD_DIM, 2 * HEAD_DIM), np.float32)
    perm[partner, lane] = 1.0
    return jnp.asarray(ones, BF16), jnp.asarray(perm, BF16)


def _in_proj_kernel(x_hbm, g_ref, w_ref, qg_ref, kg_ref, cos_ref, sin_ref,
                    ones_ref, perm_ref, o_ref, x_buf, a_ref, sem):
    j = pl.program_id(1)
    n_q_tiles = WIDTH_A // TN_PROJ
    n_qk_tiles = (WIDTH_A + KV_WIDTH_A) // TN_PROJ
    nb_q_first = (WIDTH_A + 2 * KV_WIDTH_A) // TN_PROJ
    def project(r0, rows):
        return jnp.dot(a_ref[r0:r0 + rows, :], w_ref[...].astype(BF16),
                       preferred_element_type=F32)

    def qk_rows(r0, rows):
        acc = project(r0, rows)
        is_q = j < n_q_tiles
        gain = jnp.where(is_q, qg_ref[...], kg_ref[...])
        gain = jnp.concatenate([gain, gain], axis=1)
        scale = jnp.where(is_q, ATTN_SCALE * LOG2_E, 1.0).astype(F32)
        cos = cos_ref[r0:r0 + rows, :]
        sin = sin_ref[r0:r0 + rows, :]
        cos = jnp.concatenate([cos, cos], axis=1)
        sin = jnp.concatenate([sin, sin], axis=1)
        pair = 2 * HEAD_DIM
        for hp in range(TN_PROJ // pair):
            xh = acc[:, hp * pair:(hp + 1) * pair]
            ss = jnp.dot((xh * xh).astype(BF16), ones_ref[...],
                         preferred_element_type=F32)
            y = (xh * lax.rsqrt(ss * (1.0 / HEAD_DIM) + EPS)) * gain
            partner = jnp.dot(y.astype(BF16), perm_ref[...],
                              preferred_element_type=F32)
            out = (y * cos + partner * sin) * scale
            o_ref[r0:r0 + rows, hp * pair:(hp + 1) * pair] = (
                out.astype(o_ref.dtype))

    def later_steps(r0, rows):
        @pl.when(j < n_qk_tiles)
        def _():
            qk_rows(r0, rows)

        @pl.when(j >= n_qk_tiles)
        def _():
            nb_q = (j >= nb_q_first) & (j < nb_q_first + WIDTH_B // TN_PROJ)
            scale = jnp.where(nb_q, ATTN_SCALE * LOG2_E, 1.0).astype(F32)
            o_ref[r0:r0 + rows, :] = (project(r0, rows) * scale).astype(
                o_ref.dtype)

    _normed_matmul_steps([(x_hbm, x_buf, sem.at[0], g_ref, 0)], a_ref,
                         later_steps, first_step_rows=qk_rows)


def _in_proj(x, g, w, layer, qg, kg, cos, sin):
    s, d = x.shape
    n = w.shape[2]
    ones, perm = _head_pair_matrices()
    pair = 2 * HEAD_DIM
    return pl.pallas_call(
        _in_proj_kernel,
        grid=(s // TM, n // TN_PROJ),
        in_specs=[
            pl.BlockSpec(memory_space=pl.ANY),
            pl.BlockSpec((1, d), lambda i, j: (0, 0)),
            pl.BlockSpec((None, d, TN_PROJ), lambda i, j: (layer, 0, j)),
            pl.BlockSpec((1, HEAD_DIM), lambda i, j: (0, 0)),
            pl.BlockSpec((1, HEAD_DIM), lambda i, j: (0, 0)),
            pl.BlockSpec((TM, HEAD_DIM), lambda i, j: (i, 0)),
            pl.BlockSpec((TM, HEAD_DIM), lambda i, j: (i, 0)),
            pl.BlockSpec((pair, pair), lambda i, j: (0, 0)),
            pl.BlockSpec((pair, pair), lambda i, j: (0, 0)),
        ],
        out_specs=pl.BlockSpec((TM, TN_PROJ), lambda i, j: (i, j)),
        out_shape=jax.ShapeDtypeStruct((s, n), BF16),
        scratch_shapes=[pltpu.VMEM((TM, d), F32), pltpu.VMEM((TM, d), BF16),
                        pltpu.SemaphoreType.DMA((1,))],
        compiler_params=_params("arbitrary", "arbitrary"),
        name="in_proj",
    )(x, g, w, qg, kg, cos, sin, ones, perm)


def _gqa_kernel(q_ref, k_ref, v_ref, o_ref, qs_ref, vt_ref, m_ref, l_ref,
                acc_ref, st_ref):
    tq = q_ref.shape[0]
    s_len = k_ref.shape[0]
    n_chunks = s_len // TK

    @pl.when(pl.program_id(1) == 0)
    def _():
        def transpose_chunk(c, carry):
            r = pl.multiple_of(c * TK, TK)
            vc = v_ref[pl.ds(r, TK), :].astype(F32)
            vt_ref[:, pl.ds(r, TK)] = vc.T.astype(vt_ref.dtype)
            return carry

        lax.fori_loop(0, n_chunks, transpose_chunk, 0)

    for g in range(GQA_GROUP):
        qs_ref[g * tq:(g + 1) * tq, :] = q_ref[:, g * HEAD_DIM:(g + 1) * HEAD_DIM]
    m_ref[...] = jnp.full(m_ref.shape, MASK_VALUE, F32)
    l_ref[...] = jnp.zeros(l_ref.shape, F32)
    acc_ref[...] = jnp.zeros(acc_ref.shape, F32)

    def scores(c, slot):
        r = pl.multiple_of(c * TK, TK)
        st_ref[slot] = lax.dot_general(k_ref[pl.ds(r, TK), :], qs_ref[...],
                                       (((1,), (1,)), ((), ())),
                                       preferred_element_type=F32)

    def softmax_pv(c, slot):
        r = pl.multiple_of(c * TK, TK)
        for g in range(GQA_GROUP):
            cols = slice(g * tq, (g + 1) * tq)
            st = st_ref[slot, :, cols]
            m_old = m_ref[:, cols]
            m_new = jnp.maximum(m_old, jnp.max(st, axis=0, keepdims=True))
            alpha = jnp.exp2(m_old - m_new)
            p = jnp.exp2(st - m_new)
            l_ref[:, cols] = alpha * l_ref[:, cols] + jnp.sum(p, axis=0,
                                                              keepdims=True)
            pv = jnp.dot(vt_ref[:, pl.ds(r, TK)], p.astype(BF16),
                         preferred_element_type=F32)
            acc_ref[:, cols] = alpha * acc_ref[:, cols] + pv
            m_ref[:, cols] = m_new

    scores(0, 0)

    def body(i, carry):
        c = GQA_UNROLL * i
        for u in range(GQA_UNROLL):
            scores(jnp.minimum(c + u + 1, n_chunks - 1), (u + 1) % GQA_UNROLL)
            softmax_pv(c + u, u)
        return carry

    lax.fori_loop(0, n_chunks // GQA_UNROLL, body, 0)
    out = (acc_ref[...] / l_ref[...]).T
    for g in range(GQA_GROUP):
        o_ref[:, g * HEAD_DIM:(g + 1) * HEAD_DIM] = out[g * tq:(g + 1) * tq]


def _gqa(proj):
    s = proj.shape[0]
    k_blk = WIDTH_A // HEAD_DIM
    v_blk = (WIDTH_A + KV_WIDTH_A) // HEAD_DIM
    gw = GQA_GROUP * HEAD_DIM
    return pl.pallas_call(
        _gqa_kernel,
        grid=(N_KV_A, s // TQ),
        in_specs=[
            pl.BlockSpec((TQ, gw), lambda h, i: (i, h)),
            pl.BlockSpec((s, HEAD_DIM), lambda h, i: (0, k_blk + h)),
            pl.BlockSpec((s, HEAD_DIM), lambda h, i: (0, v_blk + h)),
        ],
        out_specs=pl.BlockSpec((TQ, gw), lambda h, i: (i, h)),
        out_shape=jax.ShapeDtypeStruct((s, WIDTH_A), F32),
        scratch_shapes=[
            pltpu.VMEM((GQA_GROUP * TQ, HEAD_DIM), BF16),
            pltpu.VMEM((HEAD_DIM, s), BF16),
            pltpu.VMEM((1, GQA_GROUP * TQ), F32),
            pltpu.VMEM((1, GQA_GROUP * TQ), F32),
            pltpu.VMEM((HEAD_DIM, GQA_GROUP * TQ), F32),
            pltpu.VMEM((GQA_UNROLL, TK, GQA_GROUP * TQ), F32),
        ],
        compiler_params=_params("arbitrary", "arbitrary"),
        name="gqa",
    )(proj, proj, proj)


def _na_classes(grid_rows):
    kh = min(WIN_H_MAX, grid_rows)
    n_blocks = grid_rows // NA_ROWS
    classes = []
    for rb in (0, 1, n_blocks - 1):
        start = min(max(NA_ROWS * rb - NA_ROWS, 0), grid_rows - NA_KROWS)
        table = []
        for qi in range(NA_ROWS):
            i = NA_ROWS * rb + qi
            rs = min(max(i - kh // 2, 0), grid_rows - kh)
            row = []
            for a in range(NA_KROWS):
                r = start + a
                row.append(r - i + (WIN_H_MAX - 1) if rs <= r < rs + kh else None)
            table.append(row)
        classes.append(table)
    return classes


def _na_kernel(rpb_ref, q_ref, k_ref, v_ref, o_ref, strip_ref, bias_ref, s_ref,
               *, classes, n_blocks):
    h = pl.program_id(0)
    n_rpb_rows = 2 * WIN_H_MAX - 1
    n_rpb_cols = 2 * WIN_W - 1
    qblk = NA_ROWS * GRID_W
    kblk = NA_KROWS * GRID_W

    jj = lax.broadcasted_iota(jnp.int32, (GRID_W, 2 * GRID_W), 0)
    cc = lax.broadcasted_iota(jnp.int32, (GRID_W, 2 * GRID_W), 1) % GRID_W
    rel = cc - jj + (WIN_W - 1)
    cs = jnp.clip(jj - WIN_W // 2, 0, GRID_W - WIN_W)
    col_ok = (cc >= cs) & (cc < cs + WIN_W)
    for dr in range(n_rpb_rows):
        base = (h * n_rpb_rows + dr) * n_rpb_cols

        def pick(d, t, base=base):
            return jnp.where(rel == d, rpb_ref[base + d], t)

        strip = lax.fori_loop(0, n_rpb_cols, pick,
                              jnp.zeros((GRID_W, 2 * GRID_W), F32))
        strip_ref[dr] = jnp.where(col_ok, strip * LOG2_E, MASK_VALUE)

    left = lax.broadcasted_iota(jnp.int32, (GRID_W, 2 * GRID_W), 1) < GRID_W
    masked = jnp.full((GRID_W, 2 * GRID_W), MASK_VALUE, F32)
    for cls, table in enumerate(classes):
        for qi in range(NA_ROWS):
            for ap in range(NA_KROWS // 2):
                dl, dr_ = table[qi][2 * ap], table[qi][2 * ap + 1]
                lhs = masked if dl is None else strip_ref[dl]
                rhs = masked if dr_ is None else strip_ref[dr_]
                bias_ref[cls, qi * GRID_W:(qi + 1) * GRID_W,
                         ap * 2 * GRID_W:(ap + 1) * 2 * GRID_W] = (
                             jnp.where(left, lhs, rhs))

    def key_start(rb):
        sb = jnp.clip(rb - 1, 0, n_blocks - NA_KROWS // NA_ROWS)
        return pl.multiple_of(sb * qblk, qblk)

    def scores(rb, slot):
        q0 = pl.multiple_of(rb * qblk, qblk)
        s_ref[slot] = lax.dot_general(q_ref[pl.ds(q0, qblk), :],
                                      k_ref[pl.ds(key_start(rb), kblk), :],
                                      (((1,), (1,)), ((), ())),
                                      preferred_element_type=F32)

    def softmax_pv(rb, slot):
        cls = jnp.where(rb == 0, 0, jnp.where(rb == n_blocks - 1, 2, 1))
        q0 = pl.multiple_of(rb * qblk, qblk)
        s = s_ref[slot] + bias_ref[cls]
        m = jnp.max(s, axis=-1, keepdims=True)
        p = jnp.exp2(s - m)
        l = jnp.sum(p, axis=-1, keepdims=True)
        o = jnp.dot(p.astype(BF16), v_ref[pl.ds(key_start(rb), kblk), :],
                    preferred_element_type=F32)
        o_ref[pl.ds(q0, qblk), :] = o / l

    scores(0, 0)

    def body(i, carry):
        rb = NA_UNROLL * i
        for u in range(NA_UNROLL):
            scores(jnp.minimum(rb + u + 1, n_blocks - 1), (u + 1) % NA_UNROLL)
            softmax_pv(rb + u, u)
        return carry

    lax.fori_loop(0, n_blocks // NA_UNROLL, body, 0)


def _na(proj, rpb_flat):
    s = proj.shape[0]
    grid_rows = s // GRID_W
    n_blocks = grid_rows // NA_ROWS
    q_blk = (WIDTH_A + 2 * KV_WIDTH_A) // HEAD_DIM
    k_blk = q_blk + N_HEADS_B
    v_blk = k_blk + N_HEADS_B
    kern = functools.partial(_na_kernel, classes=_na_classes(grid_rows),
                             n_blocks=n_blocks)
    return pl.pallas_call(
        kern,
        grid=(N_HEADS_B,),
        in_specs=[
            pl.BlockSpec(memory_space=pltpu.SMEM),
            pl.BlockSpec((s, HEAD_DIM), lambda h: (0, q_blk + h)),
            pl.BlockSpec((s, HEAD_DIM), lambda h: (0, k_blk + h)),
            pl.BlockSpec((s, HEAD_DIM), lambda h: (0, v_blk + h)),
        ],
        out_specs=pl.BlockSpec((s, HEAD_DIM), lambda h: (0, h)),
        out_shape=jax.ShapeDtypeStruct((s, WIDTH_B), F32),
        scratch_shapes=[
            pltpu.VMEM((2 * WIN_H_MAX - 1, GRID_W, 2 * GRID_W), F32),
            pltpu.VMEM((3, NA_ROWS * GRID_W, NA_KROWS * GRID_W), F32),
            pltpu.VMEM((NA_UNROLL, NA_ROWS * GRID_W, NA_KROWS * GRID_W), F32),
        ],
        compiler_params=_params("arbitrary"),
        name="na",
    )(rpb_flat, proj, proj, proj)


def _out_proj_kernel(oa_hbm, ob_hbm, ga_ref, gb_ref, w_ref, r_ref, o_ref,
                     oa_buf, ob_buf, a_ref, sem):
    def project_rows(r0, rows):
        acc = jnp.dot(a_ref[r0:r0 + rows, :], w_ref[...].astype(BF16),
                      preferred_element_type=F32)
        o_ref[r0:r0 + rows, :] = r_ref[r0:r0 + rows, :] + acc

    _normed_matmul_steps(
        [(oa_hbm, oa_buf, sem.at[0], ga_ref, 0),
         (ob_hbm, ob_buf, sem.at[1], gb_ref, oa_buf.shape[1])], a_ref,
        project_rows)


def _out_proj(oa, ob, ga, gb, w, layer, x):
    s, wa = oa.shape
    wb = ob.shape[1]
    d = w.shape[2]
    return pl.pallas_call(
        _out_proj_kernel,
        grid=(s // TM, d // TN_PROJ),
        in_specs=[
            pl.BlockSpec(memory_space=pl.ANY),
            pl.BlockSpec(memory_space=pl.ANY),
            pl.BlockSpec((1, wa), lambda i, j: (0, 0)),
            pl.BlockSpec((1, wb), lambda i, j: (0, 0)),
            pl.BlockSpec((None, wa + wb, TN_PROJ), lambda i, j: (layer, 0, j)),
            pl.BlockSpec((TM, TN_PROJ), lambda i, j: (i, j)),
        ],
        out_specs=pl.BlockSpec((TM, TN_PROJ), lambda i, j: (i, j)),
        out_shape=jax.ShapeDtypeStruct((s, d), F32),
        scratch_shapes=[pltpu.VMEM((TM, wa), F32), pltpu.VMEM((TM, wb), F32),
                        pltpu.VMEM((TM, wa + wb), BF16),
                        pltpu.SemaphoreType.DMA((2,))],
        compiler_params=_params("arbitrary", "arbitrary"),
        name="out_proj",
    )(oa, ob, ga, gb, w, x)


def _mem_kv_kernel(m_ref, g_ref, w_ref, o_ref, a_ref):
    @pl.when(pl.program_id(0) == 0)
    def _():
        _rms_norm_rows(m_ref, g_ref, a_ref)

    o_ref[...] = jnp.dot(a_ref[...], w_ref[...],
                         preferred_element_type=F32).astype(o_ref.dtype)


def _mem_kv(mem, g, wkv):
    m, d = mem.shape
    n = wkv.shape[1]
    return pl.pallas_call(
        _mem_kv_kernel,
        grid=(n // TN_PROJ,),
        in_specs=[
            pl.BlockSpec((m, d), lambda j: (0, 0)),
            pl.BlockSpec((1, d), lambda j: (0, 0)),
            pl.BlockSpec((d, TN_PROJ), lambda j: (0, j)),
        ],
        out_specs=pl.BlockSpec((m, TN_PROJ), lambda j: (0, j)),
        out_shape=jax.ShapeDtypeStruct((m, n), BF16),
        scratch_shapes=[pltpu.VMEM((m, d), BF16)],
        compiler_params=_params("arbitrary"),
        name="mem_kv",
    )(mem, g, wkv)


def _xattn_kernel(x_ref, g_ref, wq_ref, kv_ref, wo_ref, o_ref, h_ref, oc_ref):
    _rms_norm_rows(x_ref, g_ref, h_ref)
    q = jnp.dot(h_ref[...], wq_ref[...], preferred_element_type=F32)
    for hd in range(N_HEADS_MEM):
        lo, hi = hd * HEAD_DIM, (hd + 1) * HEAD_DIM
        qh = q[:, lo:hi].astype(BF16)
        kh = kv_ref[:, lo:hi]
        vh = kv_ref[:, MEM_WIDTH + lo:MEM_WIDTH + hi]
        s = lax.dot_general(qh, kh, (((1,), (1,)), ((), ())),
                            preferred_element_type=F32) * ATTN_SCALE
        m = jnp.max(s, axis=-1, keepdims=True)
        p = jnp.exp(s - m)
        l = jnp.sum(p, axis=-1, keepdims=True)
        o = jnp.dot(p.astype(BF16), vh, preferred_element_type=F32) / l
        oc_ref[:, lo:hi] = o.astype(oc_ref.dtype)
    o_ref[...] = x_ref[...] + jnp.dot(oc_ref[...], wo_ref[...],
                                      preferred_element_type=F32)


def _xattn(x, g, wq, kv, wo):
    s, d = x.shape
    m = kv.shape[0]
    return pl.pallas_call(
        _xattn_kernel,
        grid=(s // TM_X,),
        in_specs=[
            pl.BlockSpec((TM_X, d), lambda i: (i, 0)),
            pl.BlockSpec((1, d), lambda i: (0, 0)),
            pl.BlockSpec((d, MEM_WIDTH), lambda i: (0, 0)),
            pl.BlockSpec((m, 2 * MEM_WIDTH), lambda i: (0, 0)),
            pl.BlockSpec((MEM_WIDTH, d), lambda i: (0, 0)),
        ],
        out_specs=pl.BlockSpec((TM_X, d), lambda i: (i, 0)),
        out_shape=jax.ShapeDtypeStruct((s, d), F32),
        scratch_shapes=[pltpu.VMEM((TM_X, d), BF16),
                        pltpu.VMEM((TM_X, MEM_WIDTH), BF16)],
        compiler_params=_params("parallel"),
        name="xattn",
    )(x, g, wq, kv, wo)


def _final_norm_kernel(x_ref, g_ref, o_ref):
    _rms_norm_rows(x_ref, g_ref, o_ref)


def _final_norm(x, g):
    s, d = x.shape
    return pl.pallas_call(
        _final_norm_kernel,
        grid=(s // TM_FINAL,),
        in_specs=[pl.BlockSpec((TM_FINAL, d), lambda i: (i, 0)),
                  pl.BlockSpec((1, d), lambda i: (0, 0))],
        out_specs=pl.BlockSpec((TM_FINAL, d), lambda i: (i, 0)),
        out_shape=jax.ShapeDtypeStruct((s, d), F32),
        compiler_params=_params("parallel"),
        name="final_norm",
    )(x, g)


def _rope_tables(seq_len):
    grid_rows = seq_len // GRID_W
    axis_dim = HEAD_DIM // 2
    inv_freq = ROPE_THETA ** (-jnp.arange(0, axis_dim, 2, dtype=F32) / axis_dim)
    ang_r = jnp.arange(grid_rows, dtype=F32)[:, None] * inv_freq
    ang_c = jnp.arange(GRID_W, dtype=F32)[:, None] * inv_freq
    per_row = lambda v: jnp.repeat(v, GRID_W, axis=0)
    per_col = lambda v: jnp.tile(v, (grid_rows, 1))
    cos_r, sin_r = per_row(jnp.cos(ang_r)), per_row(jnp.sin(ang_r))
    cos_c, sin_c = per_col(jnp.cos(ang_c)), per_col(jnp.sin(ang_c))
    cos = jnp.concatenate([cos_r, cos_r, cos_c, cos_c], axis=-1)
    sin = jnp.concatenate([-sin_r, sin_r, -sin_c, sin_c], axis=-1)
    return cos, sin


def kernel(x, mem, ffn1_norm, ffn1_w_gate, ffn1_w_up, ffn1_w_down, mix_norm, w_in, q_norm_a, k_norm_a, rpb_b, out_norm_a, out_norm_b, w_out, xattn_norm, mem_norm, xattn_wq, xattn_wkv, xattn_wo, ffn2_norm, ffn2_w_gate, ffn2_w_up, ffn2_w_down, final_norm):
    batch, seq_len, d_model = x.shape
    depth = w_in.shape[0]
    assert batch == 1 and mem.shape[0] == 1
    assert seq_len % GRID_W == 0
    assert (seq_len // GRID_W) % (NA_UNROLL * NA_ROWS) == 0
    assert seq_len % (GQA_UNROLL * TK) == 0 and seq_len % TM == 0

    cos, sin = _rope_tables(seq_len)
    row = lambda v: v.reshape(1, -1)
    xs = x[0]
    mem2 = mem[0]
    for l in range(depth):
        bf = lambda w: w[l].astype(BF16)

        act = _ffn_up(xs, row(ffn1_norm[l]), ffn1_w_gate, ffn1_w_up, l)
        xs = _ffn_down(act, ffn1_w_down, l, xs)

        proj = _in_proj(xs, row(mix_norm[l]), w_in, l, row(q_norm_a[l]),
                        row(k_norm_a[l]), cos, sin)
        oa = _gqa(proj)
        ob = _na(proj, rpb_b[l].reshape(-1))
        xs = _out_proj(oa, ob, row(out_norm_a[l]), row(out_norm_b[l]),
                       w_out, l, xs)

        kv = _mem_kv(mem2, row(mem_norm[l]), bf(xattn_wkv))
        xs = _xattn(xs, row(xattn_norm[l]), bf(xattn_wq), kv, bf(xattn_wo))

        act = _ffn_up(xs, row(ffn2_norm[l]), ffn2_w_gate, ffn2_w_up, l)
        xs = _ffn_down(act, ffn2_w_down, l, xs)

    return _final_norm(xs, row(final_norm))[None]
```

```python
import functools

import jax
import jax.numpy as jnp
import numpy as np
from jax import lax
from jax.experimental import pallas as pl
from jax.experimental.pallas import tpu as pltpu

F32 = jnp.float32
BF16 = jnp.bfloat16

HEAD_DIM = 128
N_HEADS_A = 16
N_KV_A = 4
GQA_GROUP = N_HEADS_A // N_KV_A
N_HEADS_B = 16
WIDTH_A = N_HEADS_A * HEAD_DIM
WIDTH_B = N_HEADS_B * HEAD_DIM
KV_WIDTH_A = N_KV_A * HEAD_DIM
GRID_W = 64
WIN_H_MAX = 8
WIN_W = 16
N_HEADS_MEM = 4
MEM_WIDTH = N_HEADS_MEM * HEAD_DIM
ROPE_THETA = 10000.0
EPS = 1e-6
ATTN_SCALE = HEAD_DIM ** -0.5
LOG2_E = 1.4426950408889634
MASK_VALUE = -1e30

V7X_VMEM_BYTES = 64 * 1024 * 1024
VMEM_LIMIT_BYTES = V7X_VMEM_BYTES - 6 * 1024 * 1024

TM = 1024
TN_UP = 256
TN_DOWN = 512
KSPLIT_DOWN = 2
TN_PROJ = 512
NORM_COLS = 1024
NORM_CHUNK = 256
TQ = 256
TK = 512
GQA_UNROLL = 4
NA_ROWS = 4
NA_KROWS = 12
NA_UNROLL = 4
TM_X = 256
TM_FINAL = 512


def _params(*sem):
    return pltpu.CompilerParams(dimension_semantics=sem,
                                vmem_limit_bytes=VMEM_LIMIT_BYTES)


def _rms_norm_chunk(x_ref, g_ref, a_ref, r, chunk, col_off=0):
    d = x_ref.shape[1]
    cb = min(NORM_COLS, d)
    ss = jnp.zeros((chunk, 1), F32)
    for c0 in range(0, d, cb):
        xb = x_ref[pl.ds(r, chunk), c0:c0 + cb].astype(F32)
        ss = ss + jnp.sum(xb * xb, axis=-1, keepdims=True)
    inv = lax.rsqrt(ss * (1.0 / d) + EPS)
    for c0 in range(0, d, cb):
        xb = x_ref[pl.ds(r, chunk), c0:c0 + cb].astype(F32)
        y = (xb * inv) * g_ref[:, c0:c0 + cb]
        a_ref[pl.ds(r, chunk),
              col_off + c0:col_off + c0 + cb] = y.astype(a_ref.dtype)


def _rms_norm_rows(x_ref, g_ref, a_ref, col_off=0):
    tm = x_ref.shape[0]
    chunk = min(NORM_CHUNK, tm)

    def body(c, carry):
        _rms_norm_chunk(x_ref, g_ref, a_ref, pl.multiple_of(c * chunk, chunk),
                        chunk, col_off)
        return carry

    lax.fori_loop(0, tm // chunk, body, 0)


def _row_tile_copy(x_hbm, x_buf, sem, tile):
    rows = x_buf.shape[0]
    r = pl.multiple_of(tile * rows, rows)
    return pltpu.make_async_copy(x_hbm.at[pl.ds(r, rows), :], x_buf, sem)


def _normed_matmul_steps(sources, a_ref, compute_rows, first_step_rows=None):
    i = pl.program_id(0)
    j = pl.program_id(1)
    tm = a_ref.shape[0]
    chunk = min(NORM_CHUNK, tm)

    @pl.when(j == 0)
    def _():
        @pl.when(i == 0)
        def _():
            for x_hbm, x_buf, sem, _, _ in sources:
                _row_tile_copy(x_hbm, x_buf, sem, 0).start()

        for x_hbm, x_buf, sem, _, _ in sources:
            _row_tile_copy(x_hbm, x_buf, sem, i).wait()
        for r0 in range(0, tm, chunk):
            for _, x_buf, _, g_ref, col_off in sources:
                _rms_norm_chunk(x_buf, g_ref, a_ref, r0, chunk, col_off)
            (first_step_rows or compute_rows)(r0, chunk)

    @pl.when((j == 1) & (i + 1 < pl.num_programs(0)))
    def _():
        for x_hbm, x_buf, sem, _, _ in sources:
            _row_tile_copy(x_hbm, x_buf, sem, i + 1).start()

    @pl.when(j > 0)
    def _():
        compute_rows(0, tm)


def _ffn_up_kernel(x_hbm, g_ref, wg_ref, wu_ref, o_ref, x_buf, a_ref, sem):
    def swiglu_rows(r0, rows):
        a = a_ref[r0:r0 + rows, :]
        gate = jnp.dot(a, wg_ref[...].astype(BF16), preferred_element_type=F32)
        up = jnp.dot(a, wu_ref[...].astype(BF16), preferred_element_type=F32)
        o_ref[r0:r0 + rows, :] = (
            (gate * jax.nn.sigmoid(gate)) * up).astype(o_ref.dtype)

    _normed_matmul_steps([(x_hbm, x_buf, sem.at[0], g_ref, 0)], a_ref,
                         swiglu_rows)


def _ffn_up(x, g, wg, wu, layer):
    s, d = x.shape
    dff = wg.shape[2]
    return pl.pallas_call(
        _ffn_up_kernel,
        grid=(s // TM, dff // TN_UP),
        in_specs=[
            pl.BlockSpec(memory_space=pl.ANY),
            pl.BlockSpec((1, d), lambda i, j: (0, 0)),
            pl.BlockSpec((None, d, TN_UP), lambda i, j: (layer, 0, j)),
            pl.BlockSpec((None, d, TN_UP), lambda i, j: (layer, 0, j)),
        ],
        out_specs=pl.BlockSpec((TM, TN_UP), lambda i, j: (i, j)),
        out_shape=jax.ShapeDtypeStruct((s, dff), BF16),
        scratch_shapes=[pltpu.VMEM((TM, d), F32), pltpu.VMEM((TM, d), BF16),
                        pltpu.SemaphoreType.DMA((1,))],
        compiler_params=_params("arbitrary", "arbitrary"),
        name="ffn_up",
    )(x, g, wg, wu)


def _ffn_down_kernel(act_hbm, w_ref, r_ref, o_ref, a0_buf, a1_buf, sem):
    i = pl.program_id(0)
    j = pl.program_id(1)
    k = pl.program_id(2)
    n_i = pl.num_programs(0)
    n_j = pl.num_programs(1)
    tm, tk = a0_buf.shape
    bufs = (a0_buf, a1_buf)

    def half_copy(tile, half):
        r = pl.multiple_of(tile * tm, tm)
        return pltpu.make_async_copy(
            act_hbm.at[pl.ds(r, tm), pl.ds(half * tk, tk)], bufs[half],
            sem.at[half])

    @pl.when((j == 0) & (k == 0))
    def _():
        @pl.when(i == 0)
        def _():
            half_copy(0, 0).start()

        half_copy(i, 1).start()
        half_copy(i, 0).wait()

    @pl.when((j == 0) & (k == 1))
    def _():
        half_copy(i, 1).wait()

    @pl.when((j == n_j - 1) & (k == 1) & (i + 1 < n_i))
    def _():
        half_copy(i + 1, 0).start()

    for half in range(KSPLIT_DOWN):
        @pl.when(k == half)
        def _(half=half):
            part = 0.5 * jnp.dot(bufs[half][...], w_ref[...].astype(BF16),
                                 preferred_element_type=F32)
            if half == 0:
                o_ref[...] = r_ref[...] + part
            else:
                o_ref[...] += part


def _ffn_down(act, wd, layer, x):
    s, dff = act.shape
    d = wd.shape[2]
    assert KSPLIT_DOWN == 2 and d // TN_DOWN >= 2
    tk = dff // KSPLIT_DOWN
    return pl.pallas_call(
        _ffn_down_kernel,
        grid=(s // TM, d // TN_DOWN, KSPLIT_DOWN),
        in_specs=[
            pl.BlockSpec(memory_space=pl.ANY),
            pl.BlockSpec((None, tk, TN_DOWN), lambda i, j, k: (layer, k, j)),
            pl.BlockSpec((TM, TN_DOWN), lambda i, j, k: (i, j)),
        ],
        out_specs=pl.BlockSpec((TM, TN_DOWN), lambda i, j, k: (i, j)),
        out_shape=jax.ShapeDtypeStruct((s, d), F32),
        scratch_shapes=[pltpu.VMEM((TM, tk), BF16), pltpu.VMEM((TM, tk), BF16),
                        pltpu.SemaphoreType.DMA((KSPLIT_DOWN,))],
        compiler_params=_params("arbitrary", "arbitrary", "arbitrary"),
        name="ffn_down",
    )(act, wd, x)


def _head_pair_matrices():
    lane = np.arange(2 * HEAD_DIM)
    head = lane // HEAD_DIM
    ones = (head[:, None] == head[None, :]).astype(np.float32)
    first_half = (lane % (HEAD_DIM // 2)) < (HEAD_DIM // 4)
    partner = np.where(first_half, lane + HEAD_DIM // 4, lane - HEAD_DIM // 4)
    perm = np.zeros((2 * HEAD_DIM, 2 * HEAD_DIM), np.float32)
    perm[partner, lane] = 1.0
    return jnp.asarray(ones, BF16), jnp.asarray(perm, BF16)


def _in_proj_kernel(x_hbm, g_ref, w_ref, qg_ref, kg_ref, cos_ref, sin_ref,
                    ones_ref, perm_ref, o_ref, x_buf, a_ref, sem):
    j = pl.program_id(1)
    n_q_tiles = WIDTH_A // TN_PROJ
    n_qk_tiles = (WIDTH_A + KV_WIDTH_A) // TN_PROJ
    nb_q_first = (WIDTH_A + 2 * KV_WIDTH_A) // TN_PROJ
    def project(r0, rows):
        return jnp.dot(a_ref[r0:r0 + rows, :], w_ref[...].astype(BF16),
                       preferred_element_type=F32)

    def qk_rows(r0, rows):
        acc = project(r0, rows)
        is_q = j < n_q_tiles
        gain = jnp.where(is_q, qg_ref[...], kg_ref[...])
        gain = jnp.concatenate([gain, gain], axis=1)
        scale = jnp.where(is_q, ATTN_SCALE * LOG2_E, 1.0).astype(F32)
        cos = cos_ref[r0:r0 + rows, :]
        sin = sin_ref[r0:r0 + rows, :]
        cos = jnp.concatenate([cos, cos], axis=1)
        sin = jnp.concatenate([sin, sin], axis=1)
        pair = 2 * HEAD_DIM
        for hp in range(TN_PROJ // pair):
            xh = acc[:, hp * pair:(hp + 1) * pair]
            ss = jnp.dot((xh * xh).astype(BF16), ones_ref[...],
                         preferred_element_type=F32)
            y = (xh * lax.rsqrt(ss * (1.0 / HEAD_DIM) + EPS)) * gain
            partner = jnp.dot(y.astype(BF16), perm_ref[...],
                              preferred_element_type=F32)
            out = (y * cos + partner * sin) * scale
            o_ref[r0:r0 + rows, hp * pair:(hp + 1) * pair] = (
                out.astype(o_ref.dtype))

    def later_steps(r0, rows):
        @pl.when(j < n_qk_tiles)
        def _():
            qk_rows(r0, rows)

        @pl.when(j >= n_qk_tiles)
        def _():
            nb_q = (j >= nb_q_first) & (j < nb_q_first + WIDTH_B // TN_PROJ)
            scale = jnp.where(nb_q, ATTN_SCALE * LOG2_E, 1.0).astype(F32)
            o_ref[r0:r0 + rows, :] = (project(r0, rows) * scale).astype(
                o_ref.dtype)

    _normed_matmul_steps([(x_hbm, x_buf, sem.at[0], g_ref, 0)], a_ref,
                         later_steps, first_step_rows=qk_rows)


def _in_proj(x, g, w, layer, qg, kg, cos, sin):
    s, d = x.shape
    n = w.shape[2]
    ones, perm = _head_pair_matrices()
    pair = 2 * HEAD_DIM
    return pl.pallas_call(
        _in_proj_kernel,
        grid=(s // TM, n // TN_PROJ),
        in_specs=[
            pl.BlockSpec(memory_space=pl.ANY),
            pl.BlockSpec((1, d), lambda i, j: (0, 0)),
            pl.BlockSpec((None, d, TN_PROJ), lambda i, j: (layer, 0, j)),
            pl.BlockSpec((1, HEAD_DIM), lambda i, j: (0, 0)),
            pl.BlockSpec((1, HEAD_DIM), lambda i, j: (0, 0)),
            pl.BlockSpec((TM, HEAD_DIM), lambda i, j: (i, 0)),
            pl.BlockSpec((TM, HEAD_DIM), lambda i, j: (i, 0)),
            pl.BlockSpec((pair, pair), lambda i, j: (0, 0)),
            pl.BlockSpec((pair, pair), lambda i, j: (0, 0)),
        ],
        out_specs=pl.BlockSpec((TM, TN_PROJ), lambda i, j: (i, j)),
        out_shape=jax.ShapeDtypeStruct((s, n), BF16),
        scratch_shapes=[pltpu.VMEM((TM, d), F32), pltpu.VMEM((TM, d), BF16),
                        pltpu.SemaphoreType.DMA((1,))],
        compiler_params=_params("arbitrary", "arbitrary"),
        name="in_proj",
    )(x, g, w, qg, kg, cos, sin, ones, perm)


def _gqa_kernel(q_ref, k_ref, v_ref, o_ref, qs_ref, vt_ref, m_ref, l_ref,
                acc_ref, st_ref):
    tq = o_ref.shape[0]
    s_len = k_ref.shape[0]
    n_chunks = s_len // TK
    i = pl.program_id(1)
    last_block = pl.num_programs(1) - 1

    def stack_queries(blk):
        r = pl.multiple_of(blk * tq, tq)
        for g in range(GQA_GROUP):
            qs_ref[g * tq:(g + 1) * tq, :] = (
                q_ref[pl.ds(r, tq), g * HEAD_DIM:(g + 1) * HEAD_DIM])

    def scores(c, slot):
        r = pl.multiple_of(c * TK, TK)
        st_ref[slot] = lax.dot_general(k_ref[pl.ds(r, TK), :], qs_ref[...],
                                       (((1,), (1,)), ((), ())),
                                       preferred_element_type=F32)

    def softmax_pv(c, slot):
        r = pl.multiple_of(c * TK, TK)
        for g in range(GQA_GROUP):
            cols = slice(g * tq, (g + 1) * tq)
            st = st_ref[slot, :, cols]
            m_old = m_ref[:, cols]
            m_new = jnp.maximum(m_old, jnp.max(st, axis=0, keepdims=True))
            alpha = jnp.exp2(m_old - m_new)
            p = jnp.exp2(st - m_new)
            l_ref[:, cols] = alpha * l_ref[:, cols] + jnp.sum(p, axis=0,
                                                              keepdims=True)
            pv = jnp.dot(vt_ref[:, pl.ds(r, TK)], p.astype(BF16),
                         preferred_element_type=F32)
            acc_ref[:, cols] = alpha * acc_ref[:, cols] + pv
            m_ref[:, cols] = m_new

    @pl.when(i == 0)
    def _():
        def transpose_chunk(c, carry):
            r = pl.multiple_of(c * TK, TK)
            vc = v_ref[pl.ds(r, TK), :].astype(F32)
            vt_ref[:, pl.ds(r, TK)] = vc.T.astype(vt_ref.dtype)
            return carry

        lax.fori_loop(0, n_chunks, transpose_chunk, 0)
        stack_queries(0)
        scores(0, 0)

    m_ref[...] = jnp.full(m_ref.shape, MASK_VALUE, F32)
    l_ref[...] = jnp.zeros(l_ref.shape, F32)
    acc_ref[...] = jnp.zeros(acc_ref.shape, F32)

    def body(it, carry):
        c = GQA_UNROLL * it
        for u in range(GQA_UNROLL):
            scores(c + u + 1, (u + 1) % GQA_UNROLL)
            softmax_pv(c + u, u)
        return carry

    lax.fori_loop(0, n_chunks // GQA_UNROLL - 1, body, 0)
    c = n_chunks - GQA_UNROLL
    for u in range(GQA_UNROLL - 1):
        scores(c + u + 1, u + 1)
        softmax_pv(c + u, u)
    stack_queries(jnp.minimum(i + 1, last_block))
    scores(0, 0)
    softmax_pv(n_chunks - 1, GQA_UNROLL - 1)

    out = (acc_ref[...] / l_ref[...]).T
    for g in range(GQA_GROUP):
        o_ref[:, g * HEAD_DIM:(g + 1) * HEAD_DIM] = out[g * tq:(g + 1) * tq]


def _gqa(proj):
    s = proj.shape[0]
    k_blk = WIDTH_A // HEAD_DIM
    v_blk = (WIDTH_A + KV_WIDTH_A) // HEAD_DIM
    gw = GQA_GROUP * HEAD_DIM
    return pl.pallas_call(
        _gqa_kernel,
        grid=(N_KV_A, s // TQ),
        in_specs=[
            pl.BlockSpec((s, gw), lambda h, i: (0, h)),
            pl.BlockSpec((s, HEAD_DIM), lambda h, i: (0, k_blk + h)),
            pl.BlockSpec((s, HEAD_DIM), lambda h, i: (0, v_blk + h)),
        ],
        out_specs=pl.BlockSpec((TQ, gw), lambda h, i: (i, h)),
        out_shape=jax.ShapeDtypeStruct((s, WIDTH_A), F32),
        scratch_shapes=[
            pltpu.VMEM((GQA_GROUP * TQ, HEAD_DIM), BF16),
            pltpu.VMEM((HEAD_DIM, s), BF16),
            pltpu.VMEM((1, GQA_GROUP * TQ), F32),
            pltpu.VMEM((1, GQA_GROUP * TQ), F32),
            pltpu.VMEM((HEAD_DIM, GQA_GROUP * TQ), F32),
            pltpu.VMEM((GQA_UNROLL, TK, GQA_GROUP * TQ), F32),
        ],
        compiler_params=_params("arbitrary", "arbitrary"),
        name="gqa",
    )(proj, proj, proj)


def _na_classes(grid_rows):
    kh = min(WIN_H_MAX, grid_rows)
    n_blocks = grid_rows // NA_ROWS
    classes = []
    for rb in (0, 1, n_blocks - 1):
        start = min(max(NA_ROWS * rb - NA_ROWS, 0), grid_rows - NA_KROWS)
        table = []
        for qi in range(NA_ROWS):
            i = NA_ROWS * rb + qi
            rs = min(max(i - kh // 2, 0), grid_rows - kh)
            row = []
            for a in range(NA_KROWS):
                r = start + a
                row.append(r - i + (WIN_H_MAX - 1) if rs <= r < rs + kh else None)
            table.append(row)
        classes.append(table)
    return classes


def _na_kernel(rpb_ref, q_ref, k_ref, v_ref, o_ref, strip_ref, bias_ref, s_ref,
               *, classes, n_blocks):
    h = pl.program_id(0)
    n_rpb_rows = 2 * WIN_H_MAX - 1
    n_rpb_cols = 2 * WIN_W - 1
    qblk = NA_ROWS * GRID_W
    kblk = NA_KROWS * GRID_W

    jj = lax.broadcasted_iota(jnp.int32, (GRID_W, 2 * GRID_W), 0)
    cc = lax.broadcasted_iota(jnp.int32, (GRID_W, 2 * GRID_W), 1) % GRID_W
    rel = cc - jj + (WIN_W - 1)
    cs = jnp.clip(jj - WIN_W // 2, 0, GRID_W - WIN_W)
    col_ok = (cc >= cs) & (cc < cs + WIN_W)
    for dr in range(n_rpb_rows):
        base = (h * n_rpb_rows + dr) * n_rpb_cols

        def pick(d, t, base=base):
            return jnp.where(rel == d, rpb_ref[base + d], t)

        strip = lax.fori_loop(0, n_rpb_cols, pick,
                              jnp.zeros((GRID_W, 2 * GRID_W), F32))
        strip_ref[dr] = jnp.where(col_ok, strip * LOG2_E, MASK_VALUE)

    left = lax.broadcasted_iota(jnp.int32, (GRID_W, 2 * GRID_W), 1) < GRID_W
    masked = jnp.full((GRID_W, 2 * GRID_W), MASK_VALUE, F32)
    for cls, table in enumerate(classes):
        for qi in range(NA_ROWS):
            for ap in range(NA_KROWS // 2):
                dl, dr_ = table[qi][2 * ap], table[qi][2 * ap + 1]
                lhs = masked if dl is None else strip_ref[dl]
                rhs = masked if dr_ is None else strip_ref[dr_]
                bias_ref[cls, qi * GRID_W:(qi + 1) * GRID_W,
                         ap * 2 * GRID_W:(ap + 1) * 2 * GRID_W] = (
                             jnp.where(left, lhs, rhs))

    def key_start(rb):
        sb = jnp.clip(rb - 1, 0, n_blocks - NA_KROWS // NA_ROWS)
        return pl.multiple_of(sb * qblk, qblk)

    def scores(rb, slot):
        q0 = pl.multiple_of(rb * qblk, qblk)
        s_ref[slot] = lax.dot_general(q_ref[pl.ds(q0, qblk), :],
                                      k_ref[pl.ds(key_start(rb), kblk), :],
                                      (((1,), (1,)), ((), ())),
                                      preferred_element_type=F32)

    def softmax_pv(rb, slot):
        cls = jnp.where(rb == 0, 0, jnp.where(rb == n_blocks - 1, 2, 1))
        q0 = pl.multiple_of(rb * qblk, qblk)
        s = s_ref[slot] + bias_ref[cls]
        m = jnp.max(s, axis=-1, keepdims=True)
        p = jnp.exp2(s - m)
        l = jnp.sum(p, axis=-1, keepdims=True)
        o = jnp.dot(p.astype(BF16), v_ref[pl.ds(key_start(rb), kblk), :],
                    preferred_element_type=F32)
        o_ref[pl.ds(q0, qblk), :] = o / l

    scores(0, 0)

    def body(i, carry):
        rb = NA_UNROLL * i
        for u in range(NA_UNROLL):
            scores(jnp.minimum(rb + u + 1, n_blocks - 1), (u + 1) % NA_UNROLL)
            softmax_pv(rb + u, u)
        return carry

    lax.fori_loop(0, n_blocks // NA_UNROLL, body, 0)


def _na(proj, rpb_flat):
    s = proj.shape[0]
    grid_rows = s // GRID_W
    n_blocks = grid_rows // NA_ROWS
    q_blk = (WIDTH_A + 2 * KV_WIDTH_A) // HEAD_DIM
    k_blk = q_blk + N_HEADS_B
    v_blk = k_blk + N_HEADS_B
    kern = functools.partial(_na_kernel, classes=_na_classes(grid_rows),
                             n_blocks=n_blocks)
    return pl.pallas_call(
        kern,
        grid=(N_HEADS_B,),
        in_specs=[
            pl.BlockSpec(memory_space=pltpu.SMEM),
            pl.BlockSpec((s, HEAD_DIM), lambda h: (0, q_blk + h)),
            pl.BlockSpec((s, HEAD_DIM), lambda h: (0, k_blk + h)),
            pl.BlockSpec((s, HEAD_DIM), lambda h: (0, v_blk + h)),
        ],
        out_specs=pl.BlockSpec((s, HEAD_DIM), lambda h: (0, h)),
        out_shape=jax.ShapeDtypeStruct((s, WIDTH_B), F32),
        scratch_shapes=[
            pltpu.VMEM((2 * WIN_H_MAX - 1, GRID_W, 2 * GRID_W), F32),
            pltpu.VMEM((3, NA_ROWS * GRID_W, NA_KROWS * GRID_W), F32),
            pltpu.VMEM((NA_UNROLL, NA_ROWS * GRID_W, NA_KROWS * GRID_W), F32),
        ],
        compiler_params=_params("arbitrary"),
        name="na",
    )(rpb_flat, proj, proj, proj)


def _out_proj_kernel(oa_hbm, ob_hbm, ga_ref, gb_ref, w_ref, r_ref, o_ref,
                     oa_buf, ob_buf, a_ref, sem):
    def project_rows(r0, rows):
        acc = jnp.dot(a_ref[r0:r0 + rows, :], w_ref[...].astype(BF16),
                      preferred_element_type=F32)
        o_ref[r0:r0 + rows, :] = r_ref[r0:r0 + rows, :] + acc

    _normed_matmul_steps(
        [(oa_hbm, oa_buf, sem.at[0], ga_ref, 0),
         (ob_hbm, ob_buf, sem.at[1], gb_ref, oa_buf.shape[1])], a_ref,
        project_rows)


def _out_proj(oa, ob, ga, gb, w, layer, x):
    s, wa = oa.shape
    wb = ob.shape[1]
    d = w.shape[2]
    return pl.pallas_call(
        _out_proj_kernel,
        grid=(s // TM, d // TN_PROJ),
        in_specs=[
            pl.BlockSpec(memory_space=pl.ANY),
            pl.BlockSpec(memory_space=pl.ANY),
            pl.BlockSpec((1, wa), lambda i, j: (0, 0)),
            pl.BlockSpec((1, wb), lambda i, j: (0, 0)),
            pl.BlockSpec((None, wa + wb, TN_PROJ), lambda i, j: (layer, 0, j)),
            pl.BlockSpec((TM, TN_PROJ), lambda i, j: (i, j)),
        ],
        out_specs=pl.BlockSpec((TM, TN_PROJ), lambda i, j: (i, j)),
        out_shape=jax.ShapeDtypeStruct((s, d), F32),
        scratch_shapes=[pltpu.VMEM((TM, wa), F32), pltpu.VMEM((TM, wb), F32),
                        pltpu.VMEM((TM, wa + wb), BF16),
                        pltpu.SemaphoreType.DMA((2,))],
        compiler_params=_params("arbitrary", "arbitrary"),
        name="out_proj",
    )(oa, ob, ga, gb, w, x)


def _mem_kv_kernel(m_ref, g_ref, w_ref, o_ref, a_ref):
    @pl.when(pl.program_id(0) == 0)
    def _():
        _rms_norm_rows(m_ref, g_ref, a_ref)

    o_ref[...] = jnp.dot(a_ref[...], w_ref[...],
                         preferred_element_type=F32).astype(o_ref.dtype)


def _mem_kv(mem, g, wkv):
    m, d = mem.shape
    n = wkv.shape[1]
    return pl.pallas_call(
        _mem_kv_kernel,
        grid=(n // TN_PROJ,),
        in_specs=[
            pl.BlockSpec((m, d), lambda j: (0, 0)),
            pl.BlockSpec((1, d), lambda j: (0, 0)),
            pl.BlockSpec((d, TN_PROJ), lambda j: (0, j)),
        ],
        out_specs=pl.BlockSpec((m, TN_PROJ), lambda j: (0, j)),
        out_shape=jax.ShapeDtypeStruct((m, n), BF16),
        scratch_shapes=[pltpu.VMEM((m, d), BF16)],
        compiler_params=_params("arbitrary"),
        name="mem_kv",
    )(mem, g, wkv)


def _xattn_kernel(x_ref, g_ref, wq_ref, kv_ref, wo_ref, o_ref, h_ref, oc_ref):
    _rms_norm_rows(x_ref, g_ref, h_ref)
    q = jnp.dot(h_ref[...], wq_ref[...], preferred_element_type=F32)
    for hd in range(N_HEADS_MEM):
        lo, hi = hd * HEAD_DIM, (hd + 1) * HEAD_DIM
        qh = q[:, lo:hi].astype(BF16)
        kh = kv_ref[:, lo:hi]
        vh = kv_ref[:, MEM_WIDTH + lo:MEM_WIDTH + hi]
        s = lax.dot_general(qh, kh, (((1,), (1,)), ((), ())),
                            preferred_element_type=F32) * ATTN_SCALE
        m = jnp.max(s, axis=-1, keepdims=True)
        p = jnp.exp(s - m)
        l = jnp.sum(p, axis=-1, keepdims=True)
        o = jnp.dot(p.astype(BF16), vh, preferred_element_type=F32) / l
        oc_ref[:, lo:hi] = o.astype(oc_ref.dtype)
    o_ref[...] = x_ref[...] + jnp.dot(oc_ref[...], wo_ref[...],
                                      preferred_element_type=F32)


def _xattn(x, g, wq, kv, wo):
    s, d = x.shape
    m = kv.shape[0]
    return pl.pallas_call(
        _xattn_kernel,
        grid=(s // TM_X,),
        in_specs=[
            pl.BlockSpec((TM_X, d), lambda i: (i, 0)),
            pl.BlockSpec((1, d), lambda i: (0, 0)),
            pl.BlockSpec((d, MEM_WIDTH), lambda i: (0, 0)),
            pl.BlockSpec((m, 2 * MEM_WIDTH), lambda i: (0, 0)),
            pl.BlockSpec((MEM_WIDTH, d), lambda i: (0, 0)),
        ],
        out_specs=pl.BlockSpec((TM_X, d), lambda i: (i, 0)),
        out_shape=jax.ShapeDtypeStruct((s, d), F32),
        scratch_shapes=[pltpu.VMEM((TM_X, d), BF16),
                        pltpu.VMEM((TM_X, MEM_WIDTH), BF16)],
        compiler_params=_params("parallel"),
        name="xattn",
    )(x, g, wq, kv, wo)


def _final_norm_kernel(x_ref, g_ref, o_ref):
    _rms_norm_rows(x_ref, g_ref, o_ref)


def _final_norm(x, g):
    s, d = x.shape
    return pl.pallas_call(
        _final_norm_kernel,
        grid=(s // TM_FINAL,),
        in_specs=[pl.BlockSpec((TM_FINAL, d), lambda i: (i, 0)),
                  pl.BlockSpec((1, d), lambda i: (0, 0))],
        out_specs=pl.BlockSpec((TM_FINAL, d), lambda i: (i, 0)),
        out_shape=jax.ShapeDtypeStruct((s, d), F32),
        compiler_params=_params("parallel"),
        name="final_norm",
    )(x, g)


def _rope_tables(seq_len):
    grid_rows = seq_len // GRID_W
    axis_dim = HEAD_DIM // 2
    inv_freq = ROPE_THETA ** (-jnp.arange(0, axis_dim, 2, dtype=F32) / axis_dim)
    ang_r = jnp.arange(grid_rows, dtype=F32)[:, None] * inv_freq
    ang_c = jnp.arange(GRID_W, dtype=F32)[:, None] * inv_freq
    per_row = lambda v: jnp.repeat(v, GRID_W, axis=0)
    per_col = lambda v: jnp.tile(v, (grid_rows, 1))
    cos_r, sin_r = per_row(jnp.cos(ang_r)), per_row(jnp.sin(ang_r))
    cos_c, sin_c = per_col(jnp.cos(ang_c)), per_col(jnp.sin(ang_c))
    cos = jnp.concatenate([cos_r, cos_r, cos_c, cos_c], axis=-1)
    sin = jnp.concatenate([-sin_r, sin_r, -sin_c, sin_c], axis=-1)
    return cos, sin


def kernel(x, mem, ffn1_norm, ffn1_w_gate, ffn1_w_up, ffn1_w_down, mix_norm, w_in, q_norm_a, k_norm_a, rpb_b, out_norm_a, out_norm_b, w_out, xattn_norm, mem_norm, xattn_wq, xattn_wkv, xattn_wo, ffn2_norm, ffn2_w_gate, ffn2_w_up, ffn2_w_down, final_norm):
    batch, seq_len, d_model = x.shape
    depth = w_in.shape[0]
    assert batch == 1 and mem.shape[0] == 1
    assert seq_len % GRID_W == 0
    assert (seq_len // GRID_W) % (NA_UNROLL * NA_ROWS) == 0
    assert seq_len % (GQA_UNROLL * TK) == 0 and seq_len % TM == 0

    cos, sin = _rope_tables(seq_len)
    row = lambda v: v.reshape(1, -1)
    xs = x[0]
    mem2 = mem[0]
    for l in range(depth):
        bf = lambda w: w[l].astype(BF16)

        act = _ffn_up(xs, row(ffn1_norm[l]), ffn1_w_gate, ffn1_w_up, l)
        xs = _ffn_down(act, ffn1_w_down, l, xs)

        proj = _in_proj(xs, row(mix_norm[l]), w_in, l, row(q_norm_a[l]),
                        row(k_norm_a[l]), cos, sin)
        oa = _gqa(proj)
        ob = _na(proj, rpb_b[l].reshape(-1))
        xs = _out_proj(oa, ob, row(out_norm_a[l]), row(out_norm_b[l]),
                       w_out, l, xs)

        kv = _mem_kv(mem2, row(mem_norm[l]), bf(xattn_wkv))
        xs = _xattn(xs, row(xattn_norm[l]), bf(xattn_wq), kv, bf(xattn_wo))

        act = _ffn_up(xs, row(ffn2_norm[l]), ffn2_w_gate, ffn2_w_up, l)
        xs = _ffn_down(act, ffn2_w_down, l, xs)

    return _final_norm(xs, row(final_norm))[None]
```

```python
import functools

import jax
import jax.numpy as jnp
import numpy as np
from jax import lax
from jax.experimental import pallas as pl
from jax.experimental.pallas import tpu as pltpu

F32 = jnp.float32
BF16 = jnp.bfloat16

HEAD_DIM = 128
N_HEADS_A = 16
N_KV_A = 4
GQA_GROUP = N_HEADS_A // N_KV_A
N_HEADS_B = 16
WIDTH_A = N_HEADS_A * HEAD_DIM
WIDTH_B = N_HEADS_B * HEAD_DIM
KV_WIDTH_A = N_KV_A * HEAD_DIM
GRID_W = 64
WIN_H_MAX = 8
WIN_W = 16
N_HEADS_MEM = 4
MEM_WIDTH = N_HEADS_MEM * HEAD_DIM
ROPE_THETA = 10000.0
EPS = 1e-6
ATTN_SCALE = HEAD_DIM ** -0.5
LOG2_E = 1.4426950408889634
MASK_VALUE = -1e30

V7X_VMEM_BYTES = 64 * 1024 * 1024
VMEM_LIMIT_BYTES = V7X_VMEM_BYTES - 6 * 1024 * 1024

TM = 1024
TN_UP = 256
TN_DOWN = 512
KSPLIT_DOWN = 2
TN_PROJ = 512
NORM_COLS = 1024
NORM_CHUNK = 256
TQ = 256
TK = 512
GQA_ONES_ROWS = 16
GQA_UNROLL = 4
NA_ROWS = 4
NA_KROWS = 12
NA_UNROLL = 4
TM_X = 256
TM_FINAL = 512


def _params(*sem):
    return pltpu.CompilerParams(dimension_semantics=sem,
                                vmem_limit_bytes=VMEM_LIMIT_BYTES)


def _rms_norm_chunk(x_ref, g_ref, a_ref, r, chunk, col_off=0):
    d = x_ref.shape[1]
    cb = min(NORM_COLS, d)
    ss = jnp.zeros((chunk, 1), F32)
    for c0 in range(0, d, cb):
        xb = x_ref[pl.ds(r, chunk), c0:c0 + cb].astype(F32)
        ss = ss + jnp.sum(xb * xb, axis=-1, keepdims=True)
    inv = lax.rsqrt(ss * (1.0 / d) + EPS)
    for c0 in range(0, d, cb):
        xb = x_ref[pl.ds(r, chunk), c0:c0 + cb].astype(F32)
        y = (xb * inv) * g_ref[:, c0:c0 + cb]
        a_ref[pl.ds(r, chunk),
              col_off + c0:col_off + c0 + cb] = y.astype(a_ref.dtype)


def _rms_norm_rows(x_ref, g_ref, a_ref, col_off=0):
    tm = x_ref.shape[0]
    chunk = min(NORM_CHUNK, tm)

    def body(c, carry):
        _rms_norm_chunk(x_ref, g_ref, a_ref, pl.multiple_of(c * chunk, chunk),
                        chunk, col_off)
        return carry

    lax.fori_loop(0, tm // chunk, body, 0)


def _row_tile_copy(x_hbm, x_buf, sem, tile):
    rows = x_buf.shape[0]
    r = pl.multiple_of(tile * rows, rows)
    return pltpu.make_async_copy(x_hbm.at[pl.ds(r, rows), :], x_buf, sem)


def _normed_matmul_steps(sources, a_ref, compute_rows, first_step_rows=None):
    i = pl.program_id(0)
    j = pl.program_id(1)
    tm = a_ref.shape[0]
    chunk = min(NORM_CHUNK, tm)

    @pl.when(j == 0)
    def _():
        @pl.when(i == 0)
        def _():
            for x_hbm, x_buf, sem, _, _ in sources:
                _row_tile_copy(x_hbm, x_buf, sem, 0).start()

        for x_hbm, x_buf, sem, _, _ in sources:
            _row_tile_copy(x_hbm, x_buf, sem, i).wait()
        for r0 in range(0, tm, chunk):
            for _, x_buf, _, g_ref, col_off in sources:
                _rms_norm_chunk(x_buf, g_ref, a_ref, r0, chunk, col_off)
            (first_step_rows or compute_rows)(r0, chunk)

    @pl.when((j == 1) & (i + 1 < pl.num_programs(0)))
    def _():
        for x_hbm, x_buf, sem, _, _ in sources:
            _row_tile_copy(x_hbm, x_buf, sem, i + 1).start()

    @pl.when(j > 0)
    def _():
        compute_rows(0, tm)


def _ffn_up_kernel(x_hbm, g_ref, wg_ref, wu_ref, o_ref, x_buf, a_ref, sem):
    def swiglu_rows(r0, rows):
        a = a_ref[r0:r0 + rows, :]
        gate = jnp.dot(a, wg_ref[...].astype(BF16), preferred_element_type=F32)
        up = jnp.dot(a, wu_ref[...].astype(BF16), preferred_element_type=F32)
        o_ref[r0:r0 + rows, :] = (
            (gate * jax.nn.sigmoid(gate)) * up).astype(o_ref.dtype)

    _normed_matmul_steps([(x_hbm, x_buf, sem.at[0], g_ref, 0)], a_ref,
                         swiglu_rows)


def _ffn_up(x, g, wg, wu, layer):
    s, d = x.shape
    dff = wg.shape[2]
    return pl.pallas_call(
        _ffn_up_kernel,
        grid=(s // TM, dff // TN_UP),
        in_specs=[
            pl.BlockSpec(memory_space=pl.ANY),
            pl.BlockSpec((1, d), lambda i, j: (0, 0)),
            pl.BlockSpec((None, d, TN_UP), lambda i, j: (layer, 0, j)),
            pl.BlockSpec((None, d, TN_UP), lambda i, j: (layer, 0, j)),
        ],
        out_specs=pl.BlockSpec((TM, TN_UP), lambda i, j: (i, j)),
        out_shape=jax.ShapeDtypeStruct((s, dff), BF16),
        scratch_shapes=[pltpu.VMEM((TM, d), F32), pltpu.VMEM((TM, d), BF16),
                        pltpu.SemaphoreType.DMA((1,))],
        compiler_params=_params("arbitrary", "arbitrary"),
        name="ffn_up",
    )(x, g, wg, wu)


def _ffn_down_kernel(act_hbm, w_ref, r_ref, o_ref, a0_buf, a1_buf, sem):
    i = pl.program_id(0)
    j = pl.program_id(1)
    k = pl.program_id(2)
    n_i = pl.num_programs(0)
    n_j = pl.num_programs(1)
    tm, tk = a0_buf.shape
    bufs = (a0_buf, a1_buf)

    def half_copy(tile, half):
        r = pl.multiple_of(tile * tm, tm)
        return pltpu.make_async_copy(
            act_hbm.at[pl.ds(r, tm), pl.ds(half * tk, tk)], bufs[half],
            sem.at[half])

    @pl.when((j == 0) & (k == 0))
    def _():
        @pl.when(i == 0)
        def _():
            half_copy(0, 0).start()

        half_copy(i, 1).start()
        half_copy(i, 0).wait()

    @pl.when((j == 0) & (k == 1))
    def _():
        half_copy(i, 1).wait()

    @pl.when((j == n_j - 1) & (k == 1) & (i + 1 < n_i))
    def _():
        half_copy(i + 1, 0).start()

    for half in range(KSPLIT_DOWN):
        @pl.when(k == half)
        def _(half=half):
            part = 0.5 * jnp.dot(bufs[half][...], w_ref[...].astype(BF16),
                                 preferred_element_type=F32)
            if half == 0:
                o_ref[...] = r_ref[...] + part
            else:
                o_ref[...] += part


def _ffn_down(act, wd, layer, x):
    s, dff = act.shape
    d = wd.shape[2]
    assert KSPLIT_DOWN == 2 and d // TN_DOWN >= 2
    tk = dff // KSPLIT_DOWN
    return pl.pallas_call(
        _ffn_down_kernel,
        grid=(s // TM, d // TN_DOWN, KSPLIT_DOWN),
        in_specs=[
            pl.BlockSpec(memory_space=pl.ANY),
            pl.BlockSpec((None, tk, TN_DOWN), lambda i, j, k: (layer, k, j)),
            pl.BlockSpec((TM, TN_DOWN), lambda i, j, k: (i, j)),
        ],
        out_specs=pl.BlockSpec((TM, TN_DOWN), lambda i, j, k: (i, j)),
        out_shape=jax.ShapeDtypeStruct((s, d), F32),
        scratch_shapes=[pltpu.VMEM((TM, tk), BF16), pltpu.VMEM((TM, tk), BF16),
                        pltpu.SemaphoreType.DMA((KSPLIT_DOWN,))],
        compiler_params=_params("arbitrary", "arbitrary", "arbitrary"),
        name="ffn_down",
    )(act, wd, x)


def _head_pair_matrices():
    lane = np.arange(2 * HEAD_DIM)
    head = lane // HEAD_DIM
    ones = (head[:, None] == head[None, :]).astype(np.float32)
    first_half = (lane % (HEAD_DIM // 2)) < (HEAD_DIM // 4)
    partner = np.where(first_half, lane + HEAD_DIM // 4, lane - HEAD_DIM // 4)
    perm = np.zeros((2 * HEAD_DIM, 2 * HEAD_DIM), np.float32)
    perm[partner, lane] = 1.0
    return jnp.asarray(ones, BF16), jnp.asarray(perm, BF16)


def _in_proj_kernel(x_hbm, g_ref, w_ref, qg_ref, kg_ref, cos_ref, sin_ref,
                    ones_ref, perm_ref, o_ref, x_buf, a_ref, sem):
    j = pl.program_id(1)
    n_q_tiles = WIDTH_A // TN_PROJ
    n_qk_tiles = (WIDTH_A + KV_WIDTH_A) // TN_PROJ
    nb_q_first = (WIDTH_A + 2 * KV_WIDTH_A) // TN_PROJ
    def project(r0, rows):
        return jnp.dot(a_ref[r0:r0 + rows, :], w_ref[...].astype(BF16),
                       preferred_element_type=F32)

    def qk_rows(r0, rows):
        acc = project(r0, rows)
        is_q = j < n_q_tiles
        gain = jnp.where(is_q, qg_ref[...], kg_ref[...])
        gain = jnp.concatenate([gain, gain], axis=1)
        scale = jnp.where(is_q, ATTN_SCALE * LOG2_E, 1.0).astype(F32)
        cos = cos_ref[r0:r0 + rows, :]
        sin = sin_ref[r0:r0 + rows, :]
        cos = jnp.concatenate([cos, cos], axis=1)
        sin = jnp.concatenate([sin, sin], axis=1)
        pair = 2 * HEAD_DIM
        for hp in range(TN_PROJ // pair):
            xh = acc[:, hp * pair:(hp + 1) * pair]
            ss = jnp.dot((xh * xh).astype(BF16), ones_ref[...],
                         preferred_element_type=F32)
            y = (xh * lax.rsqrt(ss * (1.0 / HEAD_DIM) + EPS)) * gain
            partner = jnp.dot(y.astype(BF16), perm_ref[...],
                              preferred_element_type=F32)
            out = (y * cos + partner * sin) * scale
            o_ref[r0:r0 + rows, hp * pair:(hp + 1) * pair] = (
                out.astype(o_ref.dtype))

    def later_steps(r0, rows):
        @pl.when(j < n_qk_tiles)
        def _():
            qk_rows(r0, rows)

        @pl.when(j >= n_qk_tiles)
        def _():
            nb_q = (j >= nb_q_first) & (j < nb_q_first + WIDTH_B // TN_PROJ)
            scale = jnp.where(nb_q, ATTN_SCALE * LOG2_E, 1.0).astype(F32)
            o_ref[r0:r0 + rows, :] = (project(r0, rows) * scale).astype(
                o_ref.dtype)

    _normed_matmul_steps([(x_hbm, x_buf, sem.at[0], g_ref, 0)], a_ref,
                         later_steps, first_step_rows=qk_rows)


def _in_proj(x, g, w, layer, qg, kg, cos, sin):
    s, d = x.shape
    n = w.shape[2]
    ones, perm = _head_pair_matrices()
    pair = 2 * HEAD_DIM
    return pl.pallas_call(
        _in_proj_kernel,
        grid=(s // TM, n // TN_PROJ),
        in_specs=[
            pl.BlockSpec(memory_space=pl.ANY),
            pl.BlockSpec((1, d), lambda i, j: (0, 0)),
            pl.BlockSpec((None, d, TN_PROJ), lambda i, j: (layer, 0, j)),
            pl.BlockSpec((1, HEAD_DIM), lambda i, j: (0, 0)),
            pl.BlockSpec((1, HEAD_DIM), lambda i, j: (0, 0)),
            pl.BlockSpec((TM, HEAD_DIM), lambda i, j: (i, 0)),
            pl.BlockSpec((TM, HEAD_DIM), lambda i, j: (i, 0)),
            pl.BlockSpec((pair, pair), lambda i, j: (0, 0)),
            pl.BlockSpec((pair, pair), lambda i, j: (0, 0)),
        ],
        out_specs=pl.BlockSpec((TM, TN_PROJ), lambda i, j: (i, j)),
        out_shape=jax.ShapeDtypeStruct((s, n), BF16),
        scratch_shapes=[pltpu.VMEM((TM, d), F32), pltpu.VMEM((TM, d), BF16),
                        pltpu.SemaphoreType.DMA((1,))],
        compiler_params=_params("arbitrary", "arbitrary"),
        name="in_proj",
    )(x, g, w, qg, kg, cos, sin, ones, perm)


def _gqa_kernel(q_ref, k_ref, v_ref, o_ref, qs_ref, vt_ref, m_ref, acc_ref,
                st_ref):
    tq = o_ref.shape[0]
    s_len = k_ref.shape[0]
    n_chunks = s_len // TK
    i = pl.program_id(1)
    last_block = pl.num_programs(1) - 1

    def stack_queries(blk):
        r = pl.multiple_of(blk * tq, tq)
        for g in range(GQA_GROUP):
            qs_ref[g * tq:(g + 1) * tq, :] = (
                q_ref[pl.ds(r, tq), g * HEAD_DIM:(g + 1) * HEAD_DIM])

    def scores(c, slot):
        r = pl.multiple_of(c * TK, TK)
        st_ref[slot] = lax.dot_general(k_ref[pl.ds(r, TK), :], qs_ref[...],
                                       (((1,), (1,)), ((), ())),
                                       preferred_element_type=F32)

    def softmax_pv(c, slot):
        r = pl.multiple_of(c * TK, TK)
        for g in range(GQA_GROUP):
            cols = slice(g * tq, (g + 1) * tq)
            st = st_ref[slot, :, cols]
            m_old = m_ref[:, cols]
            m_new = jnp.maximum(m_old, jnp.max(st, axis=0, keepdims=True))
            alpha = jnp.exp2(m_old - m_new)
            p = jnp.exp2(st - m_new)
            pv = jnp.dot(vt_ref[:, pl.ds(r, TK)], p.astype(BF16),
                         preferred_element_type=F32)
            acc_ref[:, cols] = alpha * acc_ref[:, cols] + pv
            m_ref[:, cols] = m_new

    @pl.when(i == 0)
    def _():
        def transpose_chunk(c, carry):
            r = pl.multiple_of(c * TK, TK)
            vc = v_ref[pl.ds(r, TK), :].astype(F32)
            vt_ref[:HEAD_DIM, pl.ds(r, TK)] = vc.T.astype(vt_ref.dtype)
            vt_ref[HEAD_DIM:, pl.ds(r, TK)] = jnp.ones((GQA_ONES_ROWS, TK),
                                                       vt_ref.dtype)
            return carry

        lax.fori_loop(0, n_chunks, transpose_chunk, 0)
        stack_queries(0)
        scores(0, 0)

    m_ref[...] = jnp.full(m_ref.shape, MASK_VALUE, F32)
    acc_ref[...] = jnp.zeros(acc_ref.shape, F32)

    def body(it, carry):
        c = GQA_UNROLL * it
        for u in range(GQA_UNROLL):
            scores(c + u + 1, (u + 1) % GQA_UNROLL)
            softmax_pv(c + u, u)
        return carry

    lax.fori_loop(0, n_chunks // GQA_UNROLL - 1, body, 0)
    c = n_chunks - GQA_UNROLL
    for u in range(GQA_UNROLL - 1):
        scores(c + u + 1, u + 1)
        softmax_pv(c + u, u)
    stack_queries(jnp.minimum(i + 1, last_block))
    scores(0, 0)
    softmax_pv(n_chunks - 1, GQA_UNROLL - 1)

    out = (acc_ref[:HEAD_DIM, :] / acc_ref[HEAD_DIM:HEAD_DIM + 1, :]).T
    for g in range(GQA_GROUP):
        o_ref[:, g * HEAD_DIM:(g + 1) * HEAD_DIM] = out[g * tq:(g + 1) * tq]


def _gqa(proj):
    s = proj.shape[0]
    k_blk = WIDTH_A // HEAD_DIM
    v_blk = (WIDTH_A + KV_WIDTH_A) // HEAD_DIM
    gw = GQA_GROUP * HEAD_DIM
    return pl.pallas_call(
        _gqa_kernel,
        grid=(N_KV_A, s // TQ),
        in_specs=[
            pl.BlockSpec((s, gw), lambda h, i: (0, h)),
            pl.BlockSpec((s, HEAD_DIM), lambda h, i: (0, k_blk + h)),
            pl.BlockSpec((s, HEAD_DIM), lambda h, i: (0, v_blk + h)),
        ],
        out_specs=pl.BlockSpec((TQ, gw), lambda h, i: (i, h)),
        out_shape=jax.ShapeDtypeStruct((s, WIDTH_A), F32),
        scratch_shapes=[
            pltpu.VMEM((GQA_GROUP * TQ, HEAD_DIM), BF16),
            pltpu.VMEM((HEAD_DIM + GQA_ONES_ROWS, s), BF16),
            pltpu.VMEM((1, GQA_GROUP * TQ), F32),
            pltpu.VMEM((HEAD_DIM + GQA_ONES_ROWS, GQA_GROUP * TQ), F32),
            pltpu.VMEM((GQA_UNROLL, TK, GQA_GROUP * TQ), F32),
        ],
        compiler_params=_params("arbitrary", "arbitrary"),
        name="gqa",
    )(proj, proj, proj)


def _na_classes(grid_rows):
    kh = min(WIN_H_MAX, grid_rows)
    n_blocks = grid_rows // NA_ROWS
    classes = []
    for rb in (0, 1, n_blocks - 1):
        start = min(max(NA_ROWS * rb - NA_ROWS, 0), grid_rows - NA_KROWS)
        table = []
        for qi in range(NA_ROWS):
            i = NA_ROWS * rb + qi
            rs = min(max(i - kh // 2, 0), grid_rows - kh)
            row = []
            for a in range(NA_KROWS):
                r = start + a
                row.append(r - i + (WIN_H_MAX - 1) if rs <= r < rs + kh else None)
            table.append(row)
        classes.append(table)
    return classes


def _na_kernel(rpb_ref, q_ref, k_ref, v_ref, o_ref, strip_ref, bias_ref, s_ref,
               *, classes, n_blocks):
    h = pl.program_id(0)
    n_rpb_rows = 2 * WIN_H_MAX - 1
    n_rpb_cols = 2 * WIN_W - 1
    qblk = NA_ROWS * GRID_W
    kblk = NA_KROWS * GRID_W

    jj = lax.broadcasted_iota(jnp.int32, (GRID_W, 2 * GRID_W), 0)
    cc = lax.broadcasted_iota(jnp.int32, (GRID_W, 2 * GRID_W), 1) % GRID_W
    rel = cc - jj + (WIN_W - 1)
    cs = jnp.clip(jj - WIN_W // 2, 0, GRID_W - WIN_W)
    col_ok = (cc >= cs) & (cc < cs + WIN_W)
    for dr in range(n_rpb_rows):
        base = (h * n_rpb_rows + dr) * n_rpb_cols

        def pick(d, t, base=base):
            return jnp.where(rel == d, rpb_ref[base + d], t)

        strip = lax.fori_loop(0, n_rpb_cols, pick,
                              jnp.zeros((GRID_W, 2 * GRID_W), F32))
        strip_ref[dr] = jnp.where(col_ok, strip * LOG2_E, MASK_VALUE)

    left = lax.broadcasted_iota(jnp.int32, (GRID_W, 2 * GRID_W), 1) < GRID_W
    masked = jnp.full((GRID_W, 2 * GRID_W), MASK_VALUE, F32)
    for cls, table in enumerate(classes):
        for qi in range(NA_ROWS):
            for ap in range(NA_KROWS // 2):
                dl, dr_ = table[qi][2 * ap], table[qi][2 * ap + 1]
                lhs = masked if dl is None else strip_ref[dl]
                rhs = masked if dr_ is None else strip_ref[dr_]
                bias_ref[cls, qi * GRID_W:(qi + 1) * GRID_W,
                         ap * 2 * GRID_W:(ap + 1) * 2 * GRID_W] = (
                             jnp.where(left, lhs, rhs))

    def key_start(rb):
        sb = jnp.clip(rb - 1, 0, n_blocks - NA_KROWS // NA_ROWS)
        return pl.multiple_of(sb * qblk, qblk)

    def scores(rb, slot):
        q0 = pl.multiple_of(rb * qblk, qblk)
        s_ref[slot] = lax.dot_general(q_ref[pl.ds(q0, qblk), :],
                                      k_ref[pl.ds(key_start(rb), kblk), :],
                                      (((1,), (1,)), ((), ())),
                                      preferred_element_type=F32)

    def softmax_pv(rb, slot):
        cls = jnp.where(rb == 0, 0, jnp.where(rb == n_blocks - 1, 2, 1))
        q0 = pl.multiple_of(rb * qblk, qblk)
        s = s_ref[slot] + bias_ref[cls]
        m = jnp.max(s, axis=-1, keepdims=True)
        p = jnp.exp2(s - m)
        l = jnp.sum(p, axis=-1, keepdims=True)
        o = jnp.dot(p.astype(BF16), v_ref[pl.ds(key_start(rb), kblk), :],
                    preferred_element_type=F32)
        o_ref[pl.ds(q0, qblk), :] = o / l

    scores(0, 0)

    def body(i, carry):
        rb = NA_UNROLL * i
        for u in range(NA_UNROLL):
            scores(jnp.minimum(rb + u + 1, n_blocks - 1), (u + 1) % NA_UNROLL)
            softmax_pv(rb + u, u)
        return carry

    lax.fori_loop(0, n_blocks // NA_UNROLL, body, 0)


def _na(proj, rpb_flat):
    s = proj.shape[0]
    grid_rows = s // GRID_W
    n_blocks = grid_rows // NA_ROWS
    q_blk = (WIDTH_A + 2 * KV_WIDTH_A) // HEAD_DIM
    k_blk = q_blk + N_HEADS_B
    v_blk = k_blk + N_HEADS_B
    kern = functools.partial(_na_kernel, classes=_na_classes(grid_rows),
                             n_blocks=n_blocks)
    return pl.pallas_call(
        kern,
        grid=(N_HEADS_B,),
        in_specs=[
            pl.BlockSpec(memory_space=pltpu.SMEM),
            pl.BlockSpec((s, HEAD_DIM), lambda h: (0, q_blk + h)),
            pl.BlockSpec((s, HEAD_DIM), lambda h: (0, k_blk + h)),
            pl.BlockSpec((s, HEAD_DIM), lambda h: (0, v_blk + h)),
        ],
        out_specs=pl.BlockSpec((s, HEAD_DIM), lambda h: (0, h)),
        out_shape=jax.ShapeDtypeStruct((s, WIDTH_B), F32),
        scratch_shapes=[
            pltpu.VMEM((2 * WIN_H_MAX - 1, GRID_W, 2 * GRID_W), F32),
            pltpu.VMEM((3, NA_ROWS * GRID_W, NA_KROWS * GRID_W), F32),
            pltpu.VMEM((NA_UNROLL, NA_ROWS * GRID_W, NA_KROWS * GRID_W), F32),
        ],
        compiler_params=_params("arbitrary"),
        name="na",
    )(rpb_flat, proj, proj, proj)


def _out_proj_kernel(oa_hbm, ob_hbm, ga_ref, gb_ref, w_ref, r_ref, o_ref,
                     oa_buf, ob_buf, a_ref, sem):
    def project_rows(r0, rows):
        acc = jnp.dot(a_ref[r0:r0 + rows, :], w_ref[...].astype(BF16),
                      preferred_element_type=F32)
        o_ref[r0:r0 + rows, :] = r_ref[r0:r0 + rows, :] + acc

    _normed_matmul_steps(
        [(oa_hbm, oa_buf, sem.at[0], ga_ref, 0),
         (ob_hbm, ob_buf, sem.at[1], gb_ref, oa_buf.shape[1])], a_ref,
        project_rows)


def _out_proj(oa, ob, ga, gb, w, layer, x):
    s, wa = oa.shape
    wb = ob.shape[1]
    d = w.shape[2]
    return pl.pallas_call(
        _out_proj_kernel,
        grid=(s // TM, d // TN_PROJ),
        in_specs=[
            pl.BlockSpec(memory_space=pl.ANY),
            pl.BlockSpec(memory_space=pl.ANY),
            pl.BlockSpec((1, wa), lambda i, j: (0, 0)),
            pl.BlockSpec((1, wb), lambda i, j: (0, 0)),
            pl.BlockSpec((None, wa + wb, TN_PROJ), lambda i, j: (layer, 0, j)),
            pl.BlockSpec((TM, TN_PROJ), lambda i, j: (i, j)),
        ],
        out_specs=pl.BlockSpec((TM, TN_PROJ), lambda i, j: (i, j)),
        out_shape=jax.ShapeDtypeStruct((s, d), F32),
        scratch_shapes=[pltpu.VMEM((TM, wa), F32), pltpu.VMEM((TM, wb), F32),
                        pltpu.VMEM((TM, wa + wb), BF16),
                        pltpu.SemaphoreType.DMA((2,))],
        compiler_params=_params("arbitrary", "arbitrary"),
        name="out_proj",
    )(oa, ob, ga, gb, w, x)


def _mem_kv_kernel(m_ref, g_ref, w_ref, o_ref, a_ref):
    @pl.when(pl.program_id(0) == 0)
    def _():
        _rms_norm_rows(m_ref, g_ref, a_ref)

    o_ref[...] = jnp.dot(a_ref[...], w_ref[...],
                         preferred_element_type=F32).astype(o_ref.dtype)


def _mem_kv(mem, g, wkv):
    m, d = mem.shape
    n = wkv.shape[1]
    return pl.pallas_call(
        _mem_kv_kernel,
        grid=(n // TN_PROJ,),
        in_specs=[
            pl.BlockSpec((m, d), lambda j: (0, 0)),
            pl.BlockSpec((1, d), lambda j: (0, 0)),
            pl.BlockSpec((d, TN_PROJ), lambda j: (0, j)),
        ],
        out_specs=pl.BlockSpec((m, TN_PROJ), lambda j: (0, j)),
        out_shape=jax.ShapeDtypeStruct((m, n), BF16),
        scratch_shapes=[pltpu.VMEM((m, d), BF16)],
        compiler_params=_params("arbitrary"),
        name="mem_kv",
    )(mem, g, wkv)


def _xattn_kernel(x_ref, g_ref, wq_ref, kv_ref, wo_ref, o_ref, h_ref, oc_ref):
    _rms_norm_rows(x_ref, g_ref, h_ref)
    q = jnp.dot(h_ref[...], wq_ref[...], preferred_element_type=F32)
    for hd in range(N_HEADS_MEM):
        lo, hi = hd * HEAD_DIM, (hd + 1) * HEAD_DIM
        qh = q[:, lo:hi].astype(BF16)
        kh = kv_ref[:, lo:hi]
        vh = kv_ref[:, MEM_WIDTH + lo:MEM_WIDTH + hi]
        s = lax.dot_general(qh, kh, (((1,), (1,)), ((), ())),
                            preferred_element_type=F32) * ATTN_SCALE
        m = jnp.max(s, axis=-1, keepdims=True)
        p = jnp.exp(s - m)
        l = jnp.sum(p, axis=-1, keepdims=True)
        o = jnp.dot(p.astype(BF16), vh, preferred_element_type=F32) / l
        oc_ref[:, lo:hi] = o.astype(oc_ref.dtype)
    o_ref[...] = x_ref[...] + jnp.dot(oc_ref[...], wo_ref[...],
                                      preferred_element_type=F32)


def _xattn(x, g, wq, kv, wo):
    s, d = x.shape
    m = kv.shape[0]
    return pl.pallas_call(
        _xattn_kernel,
        grid=(s // TM_X,),
        in_specs=[
            pl.BlockSpec((TM_X, d), lambda i: (i, 0)),
            pl.BlockSpec((1, d), lambda i: (0, 0)),
            pl.BlockSpec((d, MEM_WIDTH), lambda i: (0, 0)),
            pl.BlockSpec((m, 2 * MEM_WIDTH), lambda i: (0, 0)),
            pl.BlockSpec((MEM_WIDTH, d), lambda i: (0, 0)),
        ],
        out_specs=pl.BlockSpec((TM_X, d), lambda i: (i, 0)),
        out_shape=jax.ShapeDtypeStruct((s, d), F32),
        scratch_shapes=[pltpu.VMEM((TM_X, d), BF16),
                        pltpu.VMEM((TM_X, MEM_WIDTH), BF16)],
        compiler_params=_params("parallel"),
        name="xattn",
    )(x, g, wq, kv, wo)


def _final_norm_kernel(x_ref, g_ref, o_ref):
    _rms_norm_rows(x_ref, g_ref, o_ref)


def _final_norm(x, g):
    s, d = x.shape
    return pl.pallas_call(
        _final_norm_kernel,
        grid=(s // TM_FINAL,),
        in_specs=[pl.BlockSpec((TM_FINAL, d), lambda i: (i, 0)),
                  pl.BlockSpec((1, d), lambda i: (0, 0))],
        out_specs=pl.BlockSpec((TM_FINAL, d), lambda i: (i, 0)),
        out_shape=jax.ShapeDtypeStruct((s, d), F32),
        compiler_params=_params("parallel"),
        name="final_norm",
    )(x, g)


def _rope_tables(seq_len):
    grid_rows = seq_len // GRID_W
    axis_dim = HEAD_DIM // 2
    inv_freq = ROPE_THETA ** (-jnp.arange(0, axis_dim, 2, dtype=F32) / axis_dim)
    ang_r = jnp.arange(grid_rows, dtype=F32)[:, None] * inv_freq
    ang_c = jnp.arange(GRID_W, dtype=F32)[:, None] * inv_freq
    per_row = lambda v: jnp.repeat(v, GRID_W, axis=0)
    per_col = lambda v: jnp.tile(v, (grid_rows, 1))
    cos_r, sin_r = per_row(jnp.cos(ang_r)), per_row(jnp.sin(ang_r))
    cos_c, sin_c = per_col(jnp.cos(ang_c)), per_col(jnp.sin(ang_c))
    cos = jnp.concatenate([cos_r, cos_r, cos_c, cos_c], axis=-1)
    sin = jnp.concatenate([-sin_r, sin_r, -sin_c, sin_c], axis=-1)
    return cos, sin


def kernel(x, mem, ffn1_norm, ffn1_w_gate, ffn1_w_up, ffn1_w_down, mix_norm, w_in, q_norm_a, k_norm_a, rpb_b, out_norm_a, out_norm_b, w_out, xattn_norm, mem_norm, xattn_wq, xattn_wkv, xattn_wo, ffn2_norm, ffn2_w_gate, ffn2_w_up, ffn2_w_down, final_norm):
    batch, seq_len, d_model = x.shape
    depth = w_in.shape[0]
    assert batch == 1 and mem.shape[0] == 1
    assert seq_len % GRID_W == 0
    assert (seq_len // GRID_W) % (NA_UNROLL * NA_ROWS) == 0
    assert seq_len % (GQA_UNROLL * TK) == 0 and seq_len % TM == 0

    cos, sin = _rope_tables(seq_len)
    row = lambda v: v.reshape(1, -1)
    xs = x[0]
    mem2 = mem[0]
    for l in range(depth):
        bf = lambda w: w[l].astype(BF16)

        act = _ffn_up(xs, row(ffn1_norm[l]), ffn1_w_gate, ffn1_w_up, l)
        xs = _ffn_down(act, ffn1_w_down, l, xs)

        proj = _in_proj(xs, row(mix_norm[l]), w_in, l, row(q_norm_a[l]),
                        row(k_norm_a[l]), cos, sin)
        oa = _gqa(proj)
        ob = _na(proj, rpb_b[l].reshape(-1))
        xs = _out_proj(oa, ob, row(out_norm_a[l]), row(out_norm_b[l]),
                       w_out, l, xs)

        kv = _mem_kv(mem2, row(mem_norm[l]), bf(xattn_wkv))
        xs = _xattn(xs, row(xattn_norm[l]), bf(xattn_wq), kv, bf(xattn_wo))

        act = _ffn_up(xs, row(ffn2_norm[l]), ffn2_w_gate, ffn2_w_up, l)
        xs = _ffn_down(act, ffn2_w_down, l, xs)

    return _final_norm(xs, row(final_norm))[None]
```

```python
import functools

import jax
import jax.numpy as jnp
import numpy as np
from jax import lax
from jax.experimental import pallas as pl
from jax.experimental.pallas import tpu as pltpu

F32 = jnp.float32
BF16 = jnp.bfloat16

HEAD_DIM = 128
N_HEADS_A = 16
N_KV_A = 4
GQA_GROUP = N_HEADS_A // N_KV_A
N_HEADS_B = 16
WIDTH_A = N_HEADS_A * HEAD_DIM
WIDTH_B = N_HEADS_B * HEAD_DIM
KV_WIDTH_A = N_KV_A * HEAD_DIM
GRID_W = 64
WIN_H_MAX = 8
WIN_W = 16
N_HEADS_MEM = 4
MEM_WIDTH = N_HEADS_MEM * HEAD_DIM
ROPE_THETA = 10000.0
EPS = 1e-6
ATTN_SCALE = HEAD_DIM ** -0.5
LOG2_E = 1.4426950408889634
MASK_VALUE = -1e30

V7X_VMEM_BYTES = 64 * 1024 * 1024
VMEM_LIMIT_BYTES = V7X_VMEM_BYTES - 6 * 1024 * 1024

TM = 1024
TN_UP = 256
TN_DOWN = 512
KSPLIT_DOWN = 2
TN_PROJ = 512
NORM_COLS = 1024
NORM_CHUNK = 256
TQ = 256
TK = 512
GQA_ONES_ROWS = 16
GQA_UNROLL = 4
NA_ROWS = 4
NA_KROWS = 12
NA_UNROLL = 4
TM_X = 256
TM_FINAL = 512


def _params(*sem):
    return pltpu.CompilerParams(dimension_semantics=sem,
                                vmem_limit_bytes=VMEM_LIMIT_BYTES)


def _rms_norm_chunk(x_ref, g_ref, a_ref, r, chunk, col_off=0):
    d = x_ref.shape[1]
    cb = min(NORM_COLS, d)
    ss = jnp.zeros((chunk, 1), F32)
    for c0 in range(0, d, cb):
        xb = x_ref[pl.ds(r, chunk), c0:c0 + cb].astype(F32)
        ss = ss + jnp.sum(xb * xb, axis=-1, keepdims=True)
    inv = lax.rsqrt(ss * (1.0 / d) + EPS)
    for c0 in range(0, d, cb):
        xb = x_ref[pl.ds(r, chunk), c0:c0 + cb].astype(F32)
        y = (xb * inv) * g_ref[:, c0:c0 + cb]
        a_ref[pl.ds(r, chunk),
              col_off + c0:col_off + c0 + cb] = y.astype(a_ref.dtype)


def _rms_norm_rows(x_ref, g_ref, a_ref, col_off=0):
    tm = x_ref.shape[0]
    chunk = min(NORM_CHUNK, tm)

    def body(c, carry):
        _rms_norm_chunk(x_ref, g_ref, a_ref, pl.multiple_of(c * chunk, chunk),
                        chunk, col_off)
        return carry

    lax.fori_loop(0, tm // chunk, body, 0)


def _row_tile_copy(x_hbm, x_buf, sem, tile):
    rows = x_buf.shape[0]
    r = pl.multiple_of(tile * rows, rows)
    return pltpu.make_async_copy(x_hbm.at[pl.ds(r, rows), :], x_buf, sem)


def _normed_matmul_steps(sources, a_ref, compute_rows, first_step_rows=None):
    i = pl.program_id(0)
    j = pl.program_id(1)
    tm = a_ref.shape[0]
    chunk = min(NORM_CHUNK, tm)

    @pl.when(j == 0)
    def _():
        @pl.when(i == 0)
        def _():
            for x_hbm, x_buf, sem, _, _ in sources:
                _row_tile_copy(x_hbm, x_buf, sem, 0).start()

        for x_hbm, x_buf, sem, _, _ in sources:
            _row_tile_copy(x_hbm, x_buf, sem, i).wait()
        for r0 in range(0, tm, chunk):
            for _, x_buf, _, g_ref, col_off in sources:
                _rms_norm_chunk(x_buf, g_ref, a_ref, r0, chunk, col_off)
            (first_step_rows or compute_rows)(r0, chunk)

    @pl.when((j == 1) & (i + 1 < pl.num_programs(0)))
    def _():
        for x_hbm, x_buf, sem, _, _ in sources:
            _row_tile_copy(x_hbm, x_buf, sem, i + 1).start()

    @pl.when(j > 0)
    def _():
        compute_rows(0, tm)


def _ffn_up_kernel(x_hbm, g_ref, wg_ref, wu_ref, o_ref, x_buf, a_ref, sem):
    def swiglu_rows(r0, rows):
        a = a_ref[r0:r0 + rows, :]
        gate = jnp.dot(a, wg_ref[...].astype(BF16), preferred_element_type=F32)
        up = jnp.dot(a, wu_ref[...].astype(BF16), preferred_element_type=F32)
        o_ref[r0:r0 + rows, :] = (
            (gate * jax.nn.sigmoid(gate)) * up).astype(o_ref.dtype)

    _normed_matmul_steps([(x_hbm, x_buf, sem.at[0], g_ref, 0)], a_ref,
                         swiglu_rows)


def _ffn_up(x, g, wg, wu, layer):
    s, d = x.shape
    dff = wg.shape[2]
    return pl.pallas_call(
        _ffn_up_kernel,
        grid=(s // TM, dff // TN_UP),
        in_specs=[
            pl.BlockSpec(memory_space=pl.ANY),
            pl.BlockSpec((1, d), lambda i, j: (0, 0)),
            pl.BlockSpec((None, d, TN_UP), lambda i, j: (layer, 0, j)),
            pl.BlockSpec((None, d, TN_UP), lambda i, j: (layer, 0, j)),
        ],
        out_specs=pl.BlockSpec((TM, TN_UP), lambda i, j: (i, j)),
        out_shape=jax.ShapeDtypeStruct((s, dff), BF16),
        scratch_shapes=[pltpu.VMEM((TM, d), F32), pltpu.VMEM((TM, d), BF16),
                        pltpu.SemaphoreType.DMA((1,))],
        compiler_params=_params("arbitrary", "arbitrary"),
        name="ffn_up",
    )(x, g, wg, wu)


def _ffn_down_kernel(act_hbm, w_ref, r_ref, o_ref, a0_buf, a1_buf, sem):
    i = pl.program_id(0)
    j = pl.program_id(1)
    k = pl.program_id(2)
    n_i = pl.num_programs(0)
    n_j = pl.num_programs(1)
    tm, tk = a0_buf.shape
    bufs = (a0_buf, a1_buf)

    def half_copy(tile, half):
        r = pl.multiple_of(tile * tm, tm)
        return pltpu.make_async_copy(
            act_hbm.at[pl.ds(r, tm), pl.ds(half * tk, tk)], bufs[half],
            sem.at[half])

    @pl.when((j == 0) & (k == 0))
    def _():
        @pl.when(i == 0)
        def _():
            half_copy(0, 0).start()

        half_copy(i, 1).start()
        half_copy(i, 0).wait()

    @pl.when((j == 0) & (k == 1))
    def _():
        half_copy(i, 1).wait()

    @pl.when((j == n_j - 1) & (k == 1) & (i + 1 < n_i))
    def _():
        half_copy(i + 1, 0).start()

    for half in range(KSPLIT_DOWN):
        @pl.when(k == half)
        def _(half=half):
            part = 0.5 * jnp.dot(bufs[half][...], w_ref[...].astype(BF16),
                                 preferred_element_type=F32)
            if half == 0:
                o_ref[...] = r_ref[...] + part
            else:
                o_ref[...] += part


def _ffn_down(act, wd, layer, x):
    s, dff = act.shape
    d = wd.shape[2]
    assert KSPLIT_DOWN == 2 and d // TN_DOWN >= 2
    tk = dff // KSPLIT_DOWN
    return pl.pallas_call(
        _ffn_down_kernel,
        grid=(s // TM, d // TN_DOWN, KSPLIT_DOWN),
        in_specs=[
            pl.BlockSpec(memory_space=pl.ANY),
            pl.BlockSpec((None, tk, TN_DOWN), lambda i, j, k: (layer, k, j)),
            pl.BlockSpec((TM, TN_DOWN), lambda i, j, k: (i, j)),
        ],
        out_specs=pl.BlockSpec((TM, TN_DOWN), lambda i, j, k: (i, j)),
        out_shape=jax.ShapeDtypeStruct((s, d), F32),
        scratch_shapes=[pltpu.VMEM((TM, tk), BF16), pltpu.VMEM((TM, tk), BF16),
                        pltpu.SemaphoreType.DMA((KSPLIT_DOWN,))],
        compiler_params=_params("arbitrary", "arbitrary", "arbitrary"),
        name="ffn_down",
    )(act, wd, x)


def _head_pair_matrices():
    lane = np.arange(2 * HEAD_DIM)
    head = lane // HEAD_DIM
    ones = (head[:, None] == head[None, :]).astype(np.float32)
    first_half = (lane % (HEAD_DIM // 2)) < (HEAD_DIM // 4)
    partner = np.where(first_half, lane + HEAD_DIM // 4, lane - HEAD_DIM // 4)
    perm = np.zeros((2 * HEAD_DIM, 2 * HEAD_DIM), np.float32)
    perm[partner, lane] = 1.0
    return jnp.asarray(ones, BF16), jnp.asarray(perm, BF16)


def _in_proj_kernel(x_hbm, g_ref, w_ref, qg_ref, kg_ref, cos_ref, sin_ref,
                    ones_ref, perm_ref, o_ref, x_buf, a_ref, sem):
    j = pl.program_id(1)
    n_q_tiles = WIDTH_A // TN_PROJ
    n_qk_tiles = (WIDTH_A + KV_WIDTH_A) // TN_PROJ
    nb_q_first = (WIDTH_A + 2 * KV_WIDTH_A) // TN_PROJ
    def project(r0, rows):
        return jnp.dot(a_ref[r0:r0 + rows, :], w_ref[...].astype(BF16),
                       preferred_element_type=F32)

    def qk_rows(r0, rows):
        acc = project(r0, rows)
        is_q = j < n_q_tiles
        gain = jnp.where(is_q, qg_ref[...], kg_ref[...])
        gain = jnp.concatenate([gain, gain], axis=1)
        scale = jnp.where(is_q, ATTN_SCALE * LOG2_E, 1.0).astype(F32)
        cos = cos_ref[r0:r0 + rows, :]
        sin = sin_ref[r0:r0 + rows, :]
        cos = jnp.concatenate([cos, cos], axis=1)
        sin = jnp.concatenate([sin, sin], axis=1)
        pair = 2 * HEAD_DIM
        for hp in range(TN_PROJ // pair):
            xh = acc[:, hp * pair:(hp + 1) * pair]
            ss = jnp.dot((xh * xh).astype(BF16), ones_ref[...],
                         preferred_element_type=F32)
            y = (xh * lax.rsqrt(ss * (1.0 / HEAD_DIM) + EPS)) * gain
            partner = jnp.dot(y.astype(BF16), perm_ref[...],
                              preferred_element_type=F32)
            out = (y * cos + partner * sin) * scale
            o_ref[r0:r0 + rows, hp * pair:(hp + 1) * pair] = (
                out.astype(o_ref.dtype))

    def later_steps(r0, rows):
        @pl.when(j < n_qk_tiles)
        def _():
            qk_rows(r0, rows)

        @pl.when(j >= n_qk_tiles)
        def _():
            nb_q = (j >= nb_q_first) & (j < nb_q_first + WIDTH_B // TN_PROJ)
            scale = jnp.where(nb_q, ATTN_SCALE * LOG2_E, 1.0).astype(F32)
            o_ref[r0:r0 + rows, :] = (project(r0, rows) * scale).astype(
                o_ref.dtype)

    _normed_matmul_steps([(x_hbm, x_buf, sem.at[0], g_ref, 0)], a_ref,
                         later_steps, first_step_rows=qk_rows)


def _in_proj(x, g, w, layer, qg, kg, cos, sin):
    s, d = x.shape
    n = w.shape[2]
    ones, perm = _head_pair_matrices()
    pair = 2 * HEAD_DIM
    return pl.pallas_call(
        _in_proj_kernel,
        grid=(s // TM, n // TN_PROJ),
        in_specs=[
            pl.BlockSpec(memory_space=pl.ANY),
            pl.BlockSpec((1, d), lambda i, j: (0, 0)),
            pl.BlockSpec((None, d, TN_PROJ), lambda i, j: (layer, 0, j)),
            pl.BlockSpec((1, HEAD_DIM), lambda i, j: (0, 0)),
            pl.BlockSpec((1, HEAD_DIM), lambda i, j: (0, 0)),
            pl.BlockSpec((TM, HEAD_DIM), lambda i, j: (i, 0)),
            pl.BlockSpec((TM, HEAD_DIM), lambda i, j: (i, 0)),
            pl.BlockSpec((pair, pair), lambda i, j: (0, 0)),
            pl.BlockSpec((pair, pair), lambda i, j: (0, 0)),
        ],
        out_specs=pl.BlockSpec((TM, TN_PROJ), lambda i, j: (i, j)),
        out_shape=jax.ShapeDtypeStruct((s, n), BF16),
        scratch_shapes=[pltpu.VMEM((TM, d), F32), pltpu.VMEM((TM, d), BF16),
                        pltpu.SemaphoreType.DMA((1,))],
        compiler_params=_params("arbitrary", "arbitrary"),
        name="in_proj",
    )(x, g, w, qg, kg, cos, sin, ones, perm)


def _gqa_kernel(q_ref, k_ref, v_ref, o_ref, qs_ref, vt_ref, m_ref, acc_ref,
                st_ref):
    tq = o_ref.shape[0]
    s_len = k_ref.shape[0]
    n_chunks = s_len // TK
    i = pl.program_id(1)
    last_block = pl.num_programs(1) - 1

    def stack_queries(blk):
        r = pl.multiple_of(blk * tq, tq)
        for g in range(GQA_GROUP):
            qs_ref[g * tq:(g + 1) * tq, :] = (
                q_ref[pl.ds(r, tq), g * HEAD_DIM:(g + 1) * HEAD_DIM])

    def scores(c, slot):
        r = pl.multiple_of(c * TK, TK)
        st_ref[slot] = lax.dot_general(k_ref[pl.ds(r, TK), :], qs_ref[...],
                                       (((1,), (1,)), ((), ())),
                                       preferred_element_type=F32)

    def softmax_pv(c, slot):
        r = pl.multiple_of(c * TK, TK)
        for g in range(GQA_GROUP):
            cols = slice(g * tq, (g + 1) * tq)
            st = st_ref[slot, :, cols]
            m_old = m_ref[:, cols]
            m_new = jnp.maximum(m_old, jnp.max(st, axis=0, keepdims=True))
            alpha = jnp.exp2(m_old - m_new)
            p = jnp.exp2(st - m_new)
            pv = jnp.dot(vt_ref[:, pl.ds(r, TK)], p.astype(BF16),
                         preferred_element_type=F32)
            acc_ref[:, cols] = alpha * acc_ref[:, cols] + pv
            m_ref[:, cols] = m_new

    @pl.when(i == 0)
    def _():
        def transpose_chunk(c, carry):
            r = pl.multiple_of(c * TK, TK)
            vc = v_ref[pl.ds(r, TK), :].astype(F32)
            vt_ref[:HEAD_DIM, pl.ds(r, TK)] = vc.T.astype(vt_ref.dtype)
            vt_ref[HEAD_DIM:, pl.ds(r, TK)] = jnp.ones((GQA_ONES_ROWS, TK),
                                                       vt_ref.dtype)
            return carry

        lax.fori_loop(0, n_chunks, transpose_chunk, 0)
        stack_queries(0)
        scores(0, 0)

    m_ref[...] = jnp.full(m_ref.shape, MASK_VALUE, F32)
    acc_ref[...] = jnp.zeros(acc_ref.shape, F32)

    def body(it, carry):
        c = GQA_UNROLL * it
        for u in range(GQA_UNROLL):
            scores(c + u + 1, (u + 1) % GQA_UNROLL)
            softmax_pv(c + u, u)
        return carry

    lax.fori_loop(0, n_chunks // GQA_UNROLL - 1, body, 0)
    c = n_chunks - GQA_UNROLL
    for u in range(GQA_UNROLL - 1):
        scores(c + u + 1, u + 1)
        softmax_pv(c + u, u)
    stack_queries(jnp.minimum(i + 1, last_block))
    scores(0, 0)
    softmax_pv(n_chunks - 1, GQA_UNROLL - 1)

    denom = acc_ref[HEAD_DIM:HEAD_DIM + 1, :]
    out = (acc_ref[:HEAD_DIM, :] / denom).T
    for g in range(GQA_GROUP):
        o_ref[:, g * HEAD_DIM:(g + 1) * HEAD_DIM] = out[g * tq:(g + 1) * tq]


def _gqa(proj):
    s = proj.shape[0]
    k_blk = WIDTH_A // HEAD_DIM
    v_blk = (WIDTH_A + KV_WIDTH_A) // HEAD_DIM
    gw = GQA_GROUP * HEAD_DIM
    return pl.pallas_call(
        _gqa_kernel,
        grid=(N_KV_A, s // TQ),
        in_specs=[
            pl.BlockSpec((s, gw), lambda h, i: (0, h)),
            pl.BlockSpec((s, HEAD_DIM), lambda h, i: (0, k_blk + h)),
            pl.BlockSpec((s, HEAD_DIM), lambda h, i: (0, v_blk + h)),
        ],
        out_specs=pl.BlockSpec((TQ, gw), lambda h, i: (i, h)),
        out_shape=jax.ShapeDtypeStruct((s, WIDTH_A), F32),
        scratch_shapes=[
            pltpu.VMEM((GQA_GROUP * TQ, HEAD_DIM), BF16),
            pltpu.VMEM((HEAD_DIM + GQA_ONES_ROWS, s), BF16),
            pltpu.VMEM((1, GQA_GROUP * TQ), F32),
            pltpu.VMEM((HEAD_DIM + GQA_ONES_ROWS, GQA_GROUP * TQ), F32),
            pltpu.VMEM((GQA_UNROLL, TK, GQA_GROUP * TQ), F32),
        ],
        compiler_params=_params("arbitrary", "arbitrary"),
        name="gqa",
    )(proj, proj, proj)


def _na_classes(grid_rows):
    kh = min(WIN_H_MAX, grid_rows)
    n_blocks = grid_rows // NA_ROWS
    classes = []
    for rb in (0, 1, n_blocks - 1):
        start = min(max(NA_ROWS * rb - NA_ROWS, 0), grid_rows - NA_KROWS)
        table = []
        for qi in range(NA_ROWS):
            i = NA_ROWS * rb + qi
            rs = min(max(i - kh // 2, 0), grid_rows - kh)
            row = []
            for a in range(NA_KROWS):
                r = start + a
                row.append(r - i + (WIN_H_MAX - 1) if rs <= r < rs + kh else None)
            table.append(row)
        classes.append(table)
    return classes


def _na_kernel(rpb_ref, q_ref, k_ref, v_ref, o_ref, strip_ref, bias_ref, s_ref,
               *, classes, n_blocks):
    h = pl.program_id(0)
    n_rpb_rows = 2 * WIN_H_MAX - 1
    n_rpb_cols = 2 * WIN_W - 1
    qblk = NA_ROWS * GRID_W
    kblk = NA_KROWS * GRID_W

    jj = lax.broadcasted_iota(jnp.int32, (GRID_W, 2 * GRID_W), 0)
    cc = lax.broadcasted_iota(jnp.int32, (GRID_W, 2 * GRID_W), 1) % GRID_W
    rel = cc - jj + (WIN_W - 1)
    cs = jnp.clip(jj - WIN_W // 2, 0, GRID_W - WIN_W)
    col_ok = (cc >= cs) & (cc < cs + WIN_W)
    for dr in range(n_rpb_rows):
        base = (h * n_rpb_rows + dr) * n_rpb_cols

        def pick(d, t, base=base):
            return jnp.where(rel == d, rpb_ref[base + d], t)

        strip = lax.fori_loop(0, n_rpb_cols, pick,
                              jnp.zeros((GRID_W, 2 * GRID_W), F32))
        strip_ref[dr] = jnp.where(col_ok, strip * LOG2_E, MASK_VALUE)

    left = lax.broadcasted_iota(jnp.int32, (GRID_W, 2 * GRID_W), 1) < GRID_W
    masked = jnp.full((GRID_W, 2 * GRID_W), MASK_VALUE, F32)
    for cls, table in enumerate(classes):
        for qi in range(NA_ROWS):
            for ap in range(NA_KROWS // 2):
                dl, dr_ = table[qi][2 * ap], table[qi][2 * ap + 1]
                lhs = masked if dl is None else strip_ref[dl]
                rhs = masked if dr_ is None else strip_ref[dr_]
                bias_ref[cls, qi * GRID_W:(qi + 1) * GRID_W,
                         ap * 2 * GRID_W:(ap + 1) * 2 * GRID_W] = (
                             jnp.where(left, lhs, rhs))

    def key_start(rb):
        sb = jnp.clip(rb - 1, 0, n_blocks - NA_KROWS // NA_ROWS)
        return pl.multiple_of(sb * qblk, qblk)

    def scores(rb, slot):
        q0 = pl.multiple_of(rb * qblk, qblk)
        s_ref[slot] = lax.dot_general(q_ref[pl.ds(q0, qblk), :],
                                      k_ref[pl.ds(key_start(rb), kblk), :],
                                      (((1,), (1,)), ((), ())),
                                      preferred_element_type=F32)

    def softmax_pv(rb, slot):
        cls = jnp.where(rb == 0, 0, jnp.where(rb == n_blocks - 1, 2, 1))
        q0 = pl.multiple_of(rb * qblk, qblk)
        s = s_ref[slot] + bias_ref[cls]
        m = jnp.max(s, axis=-1, keepdims=True)
        p = jnp.exp2(s - m)
        l = jnp.sum(p, axis=-1, keepdims=True)
        o = jnp.dot(p.astype(BF16), v_ref[pl.ds(key_start(rb), kblk), :],
                    preferred_element_type=F32)
        o_ref[pl.ds(q0, qblk), :] = o / l

    scores(0, 0)

    def body(i, carry):
        rb = NA_UNROLL * i
        for u in range(NA_UNROLL):
            scores(jnp.minimum(rb + u + 1, n_blocks - 1), (u + 1) % NA_UNROLL)
            softmax_pv(rb + u, u)
        return carry

    lax.fori_loop(0, n_blocks // NA_UNROLL, body, 0)


def _na(proj, rpb_flat):
    s = proj.shape[0]
    grid_rows = s // GRID_W
    n_blocks = grid_rows // NA_ROWS
    q_blk = (WIDTH_A + 2 * KV_WIDTH_A) // HEAD_DIM
    k_blk = q_blk + N_HEADS_B
    v_blk = k_blk + N_HEADS_B
    kern = functools.partial(_na_kernel, classes=_na_classes(grid_rows),
                             n_blocks=n_blocks)
    return pl.pallas_call(
        kern,
        grid=(N_HEADS_B,),
        in_specs=[
            pl.BlockSpec(memory_space=pltpu.SMEM),
            pl.BlockSpec((s, HEAD_DIM), lambda h: (0, q_blk + h)),
            pl.BlockSpec((s, HEAD_DIM), lambda h: (0, k_blk + h)),
            pl.BlockSpec((s, HEAD_DIM), lambda h: (0, v_blk + h)),
        ],
        out_specs=pl.BlockSpec((s, HEAD_DIM), lambda h: (0, h)),
        out_shape=jax.ShapeDtypeStruct((s, WIDTH_B), F32),
        scratch_shapes=[
            pltpu.VMEM((2 * WIN_H_MAX - 1, GRID_W, 2 * GRID_W), F32),
            pltpu.VMEM((3, NA_ROWS * GRID_W, NA_KROWS * GRID_W), F32),
            pltpu.VMEM((NA_UNROLL, NA_ROWS * GRID_W, NA_KROWS * GRID_W), F32),
        ],
        compiler_params=_params("arbitrary"),
        name="na",
    )(rpb_flat, proj, proj, proj)


def _out_proj_kernel(oa_hbm, ob_hbm, ga_ref, gb_ref, w_ref, r_ref, o_ref,
                     oa_buf, ob_buf, a_ref, sem):
    def project_rows(r0, rows):
        acc = jnp.dot(a_ref[r0:r0 + rows, :], w_ref[...].astype(BF16),
                      preferred_element_type=F32)
        o_ref[r0:r0 + rows, :] = r_ref[r0:r0 + rows, :] + acc

    _normed_matmul_steps(
        [(oa_hbm, oa_buf, sem.at[0], ga_ref, 0),
         (ob_hbm, ob_buf, sem.at[1], gb_ref, oa_buf.shape[1])], a_ref,
        project_rows)


def _out_proj(oa, ob, ga, gb, w, layer, x):
    s, wa = oa.shape
    wb = ob.shape[1]
    d = w.shape[2]
    return pl.pallas_call(
        _out_proj_kernel,
        grid=(s // TM, d // TN_PROJ),
        in_specs=[
            pl.BlockSpec(memory_space=pl.ANY),
            pl.BlockSpec(memory_space=pl.ANY),
            pl.BlockSpec((1, wa), lambda i, j: (0, 0)),
            pl.BlockSpec((1, wb), lambda i, j: (0, 0)),
            pl.BlockSpec((None, wa + wb, TN_PROJ), lambda i, j: (layer, 0, j)),
            pl.BlockSpec((TM, TN_PROJ), lambda i, j: (i, j)),
        ],
        out_specs=pl.BlockSpec((TM, TN_PROJ), lambda i, j: (i, j)),
        out_shape=jax.ShapeDtypeStruct((s, d), F32),
        scratch_shapes=[pltpu.VMEM((TM, wa), F32), pltpu.VMEM((TM, wb), F32),
                        pltpu.VMEM((TM, wa + wb), BF16),
                        pltpu.SemaphoreType.DMA((2,))],
        compiler_params=_params("arbitrary", "arbitrary"),
        name="out_proj",
    )(oa, ob, ga, gb, w, x)


def _mem_kv_kernel(m_ref, g_ref, w_ref, o_ref, a_ref):
    @pl.when(pl.program_id(0) == 0)
    def _():
        _rms_norm_rows(m_ref, g_ref, a_ref)

    o_ref[...] = jnp.dot(a_ref[...], w_ref[...],
                         preferred_element_type=F32).astype(o_ref.dtype)


def _mem_kv(mem, g, wkv):
    m, d = mem.shape
    n = wkv.shape[1]
    return pl.pallas_call(
        _mem_kv_kernel,
        grid=(n // TN_PROJ,),
        in_specs=[
            pl.BlockSpec((m, d), lambda j: (0, 0)),
            pl.BlockSpec((1, d), lambda j: (0, 0)),
            pl.BlockSpec((d, TN_PROJ), lambda j: (0, j)),
        ],
        out_specs=pl.BlockSpec((m, TN_PROJ), lambda j: (0, j)),
        out_shape=jax.ShapeDtypeStruct((m, n), BF16),
        scratch_shapes=[pltpu.VMEM((m, d), BF16)],
        compiler_params=_params("arbitrary"),
        name="mem_kv",
    )(mem, g, wkv)


def _xattn_kernel(x_ref, g_ref, wq_ref, kv_ref, wo_ref, o_ref, h_ref, oc_ref):
    _rms_norm_rows(x_ref, g_ref, h_ref)
    q = jnp.dot(h_ref[...], wq_ref[...], preferred_element_type=F32)
    for hd in range(N_HEADS_MEM):
        lo, hi = hd * HEAD_DIM, (hd + 1) * HEAD_DIM
        qh = q[:, lo:hi].astype(BF16)
        kh = kv_ref[:, lo:hi]
        vh = kv_ref[:, MEM_WIDTH + lo:MEM_WIDTH + hi]
        s = lax.dot_general(qh, kh, (((1,), (1,)), ((), ())),
                            preferred_element_type=F32) * ATTN_SCALE
        m = jnp.max(s, axis=-1, keepdims=True)
        p = jnp.exp(s - m)
        l = jnp.sum(p, axis=-1, keepdims=True)
        o = jnp.dot(p.astype(BF16), vh, preferred_element_type=F32) / l
        oc_ref[:, lo:hi] = o.astype(oc_ref.dtype)
    o_ref[...] = x_ref[...] + jnp.dot(oc_ref[...], wo_ref[...],
                                      preferred_element_type=F32)


def _xattn(x, g, wq, kv, wo):
    s, d = x.shape
    m = kv.shape[0]
    return pl.pallas_call(
        _xattn_kernel,
        grid=(s // TM_X,),
        in_specs=[
            pl.BlockSpec((TM_X, d), lambda i: (i, 0)),
            pl.BlockSpec((1, d), lambda i: (0, 0)),
            pl.BlockSpec((d, MEM_WIDTH), lambda i: (0, 0)),
            pl.BlockSpec((m, 2 * MEM_WIDTH), lambda i: (0, 0)),
            pl.BlockSpec((MEM_WIDTH, d), lambda i: (0, 0)),
        ],
        out_specs=pl.BlockSpec((TM_X, d), lambda i: (i, 0)),
        out_shape=jax.ShapeDtypeStruct((s, d), F32),
        scratch_shapes=[pltpu.VMEM((TM_X, d), BF16),
                        pltpu.VMEM((TM_X, MEM_WIDTH), BF16)],
        compiler_params=_params("parallel"),
        name="xattn",
    )(x, g, wq, kv, wo)


def _final_norm_kernel(x_ref, g_ref, o_ref):
    _rms_norm_rows(x_ref, g_ref, o_ref)


def _final_norm(x, g):
    s, d = x.shape
    return pl.pallas_call(
        _final_norm_kernel,
        grid=(s // TM_FINAL,),
        in_specs=[pl.BlockSpec((TM_FINAL, d), lambda i: (i, 0)),
                  pl.BlockSpec((1, d), lambda i: (0, 0))],
        out_specs=pl.BlockSpec((TM_FINAL, d), lambda i: (i, 0)),
        out_shape=jax.ShapeDtypeStruct((s, d), F32),
        compiler_params=_params("parallel"),
        name="final_norm",
    )(x, g)


def _rope_tables(seq_len):
    grid_rows = seq_len // GRID_W
    axis_dim = HEAD_DIM // 2
    inv_freq = ROPE_THETA ** (-jnp.arange(0, axis_dim, 2, dtype=F32) / axis_dim)
    ang_r = jnp.arange(grid_rows, dtype=F32)[:, None] * inv_freq
    ang_c = jnp.arange(GRID_W, dtype=F32)[:, None] * inv_freq
    per_row = lambda v: jnp.repeat(v, GRID_W, axis=0)
    per_col = lambda v: jnp.tile(v, (grid_rows, 1))
    cos_r, sin_r = per_row(jnp.cos(ang_r)), per_row(jnp.sin(ang_r))
    cos_c, sin_c = per_col(jnp.cos(ang_c)), per_col(jnp.sin(ang_c))
    cos = jnp.concatenate([cos_r, cos_r, cos_c, cos_c], axis=-1)
    sin = jnp.concatenate([-sin_r, sin_r, -sin_c, sin_c], axis=-1)
    return cos, sin


def kernel(x, mem, ffn1_norm, ffn1_w_gate, ffn1_w_up, ffn1_w_down, mix_norm, w_in, q_norm_a, k_norm_a, rpb_b, out_norm_a, out_norm_b, w_out, xattn_norm, mem_norm, xattn_wq, xattn_wkv, xattn_wo, ffn2_norm, ffn2_w_gate, ffn2_w_up, ffn2_w_down, final_norm):
    batch, seq_len, d_model = x.shape
    depth = w_in.shape[0]
    assert batch == 1 and mem.shape[0] == 1
    assert seq_len % GRID_W == 0
    assert (seq_len // GRID_W) % (NA_UNROLL * NA_ROWS) == 0
    assert seq_len % (GQA_UNROLL * TK) == 0 and seq_len % TM == 0

    cos, sin = _rope_tables(seq_len)
    row = lambda v: v.reshape(1, -1)
    xs = x[0]
    mem2 = mem[0]
    for l in range(depth):
        bf = lambda w: w[l].astype(BF16)

        act = _ffn_up(xs, row(ffn1_norm[l]), ffn1_w_gate, ffn1_w_up, l)
        xs = _ffn_down(act, ffn1_w_down, l, xs)

        proj = _in_proj(xs, row(mix_norm[l]), w_in, l, row(q_norm_a[l]),
                        row(k_norm_a[l]), cos, sin)
        oa = _gqa(proj)
        ob = _na(proj, rpb_b[l].reshape(-1))
        xs = _out_proj(oa, ob, row(out_norm_a[l]), row(out_norm_b[l]),
                       w_out, l, xs)

        kv = _mem_kv(mem2, row(mem_norm[l]), bf(xattn_wkv))
        xs = _xattn(xs, row(xattn_norm[l]), bf(xattn_wq), kv, bf(xattn_wo))

        act = _ffn_up(xs, row(ffn2_norm[l]), ffn2_w_gate, ffn2_w_up, l)
        xs = _ffn_down(act, ffn2_w_down, l, xs)

    return _final_norm(xs, row(final_norm))[None]
```

```python
import functools

import jax
import jax.numpy as jnp
import numpy as np
from jax import lax
from jax.experimental import pallas as pl
from jax.experimental.pallas import tpu as pltpu

F32 = jnp.float32
BF16 = jnp.bfloat16

HEAD_DIM = 128
N_HEADS_A = 16
N_KV_A = 4
GQA_GROUP = N_HEADS_A // N_KV_A
N_HEADS_B = 16
WIDTH_A = N_HEADS_A * HEAD_DIM
WIDTH_B = N_HEADS_B * HEAD_DIM
KV_WIDTH_A = N_KV_A * HEAD_DIM
GRID_W = 64
WIN_H_MAX = 8
WIN_W = 16
N_HEADS_MEM = 4
MEM_WIDTH = N_HEADS_MEM * HEAD_DIM
ROPE_THETA = 10000.0
EPS = 1e-6
ATTN_SCALE = HEAD_DIM ** -0.5
LOG2_E = 1.4426950408889634
MASK_VALUE = -1e30

V7X_VMEM_BYTES = 64 * 1024 * 1024
VMEM_LIMIT_BYTES = V7X_VMEM_BYTES - 6 * 1024 * 1024

TM = 1024
TN_UP = 256
TN_DOWN = 512
KSPLIT_DOWN = 2
TN_PROJ = 512
NORM_COLS = 1024
NORM_CHUNK = 256
TQ = 256
TK = 512
GQA_ONES_ROWS = 16
GQA_UNROLL = 4
NA_ROWS = 4
NA_KROWS = 12
NA_UNROLL = 4
TM_X = 256
TM_FINAL = 512


def _params(*sem):
    return pltpu.CompilerParams(dimension_semantics=sem,
                                vmem_limit_bytes=VMEM_LIMIT_BYTES)


def _rms_norm_chunk(x_ref, g_ref, a_ref, r, chunk, col_off=0):
    d = x_ref.shape[1]
    cb = min(NORM_COLS, d)
    ss = jnp.zeros((chunk, 1), F32)
    for c0 in range(0, d, cb):
        xb = x_ref[pl.ds(r, chunk), c0:c0 + cb].astype(F32)
        ss = ss + jnp.sum(xb * xb, axis=-1, keepdims=True)
    inv = lax.rsqrt(ss * (1.0 / d) + EPS)
    for c0 in range(0, d, cb):
        xb = x_ref[pl.ds(r, chunk), c0:c0 + cb].astype(F32)
        y = (xb * inv) * g_ref[:, c0:c0 + cb]
        a_ref[pl.ds(r, chunk),
              col_off + c0:col_off + c0 + cb] = y.astype(a_ref.dtype)


def _rms_norm_rows(x_ref, g_ref, a_ref, col_off=0):
    tm = x_ref.shape[0]
    chunk = min(NORM_CHUNK, tm)

    def body(c, carry):
        _rms_norm_chunk(x_ref, g_ref, a_ref, pl.multiple_of(c * chunk, chunk),
                        chunk, col_off)
        return carry

    lax.fori_loop(0, tm // chunk, body, 0)


def _row_tile_copy(x_hbm, x_buf, sem, tile):
    rows = x_buf.shape[0]
    r = pl.multiple_of(tile * rows, rows)
    return pltpu.make_async_copy(x_hbm.at[pl.ds(r, rows), :], x_buf, sem)


def _normed_matmul_steps(sources, a_ref, compute_rows, first_step_rows=None):
    i = pl.program_id(0)
    j = pl.program_id(1)
    tm = a_ref.shape[0]
    chunk = min(NORM_CHUNK, tm)

    @pl.when(j == 0)
    def _():
        @pl.when(i == 0)
        def _():
            for x_hbm, x_buf, sem, _, _ in sources:
                _row_tile_copy(x_hbm, x_buf, sem, 0).start()

        for x_hbm, x_buf, sem, _, _ in sources:
            _row_tile_copy(x_hbm, x_buf, sem, i).wait()
        for r0 in range(0, tm, chunk):
            for _, x_buf, _, g_ref, col_off in sources:
                _rms_norm_chunk(x_buf, g_ref, a_ref, r0, chunk, col_off)
            (first_step_rows or compute_rows)(r0, chunk)

    @pl.when((j == 1) & (i + 1 < pl.num_programs(0)))
    def _():
        for x_hbm, x_buf, sem, _, _ in sources:
            _row_tile_copy(x_hbm, x_buf, sem, i + 1).start()

    @pl.when(j > 0)
    def _():
        compute_rows(0, tm)


def _ffn_up_kernel(x_hbm, g_ref, wg_ref, wu_ref, o_ref, x_buf, a_ref, sem):
    def swiglu_rows(r0, rows):
        a = a_ref[r0:r0 + rows, :]
        gate = jnp.dot(a, wg_ref[...].astype(BF16), preferred_element_type=F32)
        up = jnp.dot(a, wu_ref[...].astype(BF16), preferred_element_type=F32)
        o_ref[r0:r0 + rows, :] = (
            (gate * jax.nn.sigmoid(gate)) * up).astype(o_ref.dtype)

    _normed_matmul_steps([(x_hbm, x_buf, sem.at[0], g_ref, 0)], a_ref,
                         swiglu_rows)


def _ffn_up(x, g, wg, wu, layer):
    s, d = x.shape
    dff = wg.shape[2]
    return pl.pallas_call(
        _ffn_up_kernel,
        grid=(s // TM, dff // TN_UP),
        in_specs=[
            pl.BlockSpec(memory_space=pl.ANY),
            pl.BlockSpec((1, d), lambda i, j: (0, 0)),
            pl.BlockSpec((None, d, TN_UP), lambda i, j: (layer, 0, j)),
            pl.BlockSpec((None, d, TN_UP), lambda i, j: (layer, 0, j)),
        ],
        out_specs=pl.BlockSpec((TM, TN_UP), lambda i, j: (i, j)),
        out_shape=jax.ShapeDtypeStruct((s, dff), BF16),
        scratch_shapes=[pltpu.VMEM((TM, d), F32), pltpu.VMEM((TM, d), BF16),
                        pltpu.SemaphoreType.DMA((1,))],
        compiler_params=_params("arbitrary", "arbitrary"),
        name="ffn_up",
    )(x, g, wg, wu)


def _ffn_down_kernel(act_hbm, w_ref, r_ref, o_ref, a0_buf, a1_buf, sem):
    i = pl.program_id(0)
    j = pl.program_id(1)
    k = pl.program_id(2)
    n_i = pl.num_programs(0)
    n_j = pl.num_programs(1)
    tm, tk = a0_buf.shape
    bufs = (a0_buf, a1_buf)

    def half_copy(tile, half):
        r = pl.multiple_of(tile * tm, tm)
        return pltpu.make_async_copy(
            act_hbm.at[pl.ds(r, tm), pl.ds(half * tk, tk)], bufs[half],
            sem.at[half])

    @pl.when((j == 0) & (k == 0))
    def _():
        @pl.when(i == 0)
        def _():
            half_copy(0, 0).start()

        half_copy(i, 1).start()
        half_copy(i, 0).wait()

    @pl.when((j == 0) & (k == 1))
    def _():
        half_copy(i, 1).wait()

    @pl.when((j == n_j - 1) & (k == 1) & (i + 1 < n_i))
    def _():
        half_copy(i + 1, 0).start()

    for half in range(KSPLIT_DOWN):
        @pl.when(k == half)
        def _(half=half):
            part = 0.5 * jnp.dot(bufs[half][...], w_ref[...].astype(BF16),
                                 preferred_element_type=F32)
            if half == 0:
                o_ref[...] = r_ref[...] + part
            else:
                o_ref[...] += part


def _ffn_down(act, wd, layer, x):
    s, dff = act.shape
    d = wd.shape[2]
    assert KSPLIT_DOWN == 2 and d // TN_DOWN >= 2
    tk = dff // KSPLIT_DOWN
    return pl.pallas_call(
        _ffn_down_kernel,
        grid=(s // TM, d // TN_DOWN, KSPLIT_DOWN),
        in_specs=[
            pl.BlockSpec(memory_space=pl.ANY),
            pl.BlockSpec((None, tk, TN_DOWN), lambda i, j, k: (layer, k, j)),
            pl.BlockSpec((TM, TN_DOWN), lambda i, j, k: (i, j)),
        ],
        out_specs=pl.BlockSpec((TM, TN_DOWN), lambda i, j, k: (i, j)),
        out_shape=jax.ShapeDtypeStruct((s, d), F32),
        scratch_shapes=[pltpu.VMEM((TM, tk), BF16), pltpu.VMEM((TM, tk), BF16),
                        pltpu.SemaphoreType.DMA((KSPLIT_DOWN,))],
        compiler_params=_params("arbitrary", "arbitrary", "arbitrary"),
        name="ffn_down",
    )(act, wd, x)


def _head_pair_matrices():
    lane = np.arange(2 * HEAD_DIM)
    head = lane // HEAD_DIM
    ones = (head[:, None] == head[None, :]).astype(np.float32)
    first_half = (lane % (HEAD_DIM // 2)) < (HEAD_DIM // 4)
    partner = np.where(first_half, lane + HEAD_DIM // 4, lane - HEAD_DIM // 4)
    perm = np.zeros((2 * HEAD_DIM, 2 * HEAD_DIM), np.float32)
    perm[partner, lane] = 1.0
    return jnp.asarray(ones, BF16), jnp.asarray(perm, BF16)


def _in_proj_kernel(x_hbm, g_ref, w_ref, qg_ref, kg_ref, cos_ref, sin_ref,
                    ones_ref, perm_ref, o_ref, x_buf, a_ref, sem):
    j = pl.program_id(1)
    n_q_tiles = WIDTH_A // TN_PROJ
    n_qk_tiles = (WIDTH_A + KV_WIDTH_A) // TN_PROJ
    nb_q_first = (WIDTH_A + 2 * KV_WIDTH_A) // TN_PROJ
    def project(r0, rows):
        return jnp.dot(a_ref[r0:r0 + rows, :], w_ref[...].astype(BF16),
                       preferred_element_type=F32)

    def qk_rows(r0, rows):
        acc = project(r0, rows)
        is_q = j < n_q_tiles
        gain = jnp.where(is_q, qg_ref[...], kg_ref[...])
        gain = jnp.concatenate([gain, gain], axis=1)
        scale = jnp.where(is_q, ATTN_SCALE * LOG2_E, 1.0).astype(F32)
        cos = cos_ref[r0:r0 + rows, :]
        sin = sin_ref[r0:r0 + rows, :]
        cos = jnp.concatenate([cos, cos], axis=1)
        sin = jnp.concatenate([sin, sin], axis=1)
        pair = 2 * HEAD_DIM
        for hp in range(TN_PROJ // pair):
            xh = acc[:, hp * pair:(hp + 1) * pair]
            ss = jnp.dot((xh * xh).astype(BF16), ones_ref[...],
                         preferred_element_type=F32)
            y = (xh * lax.rsqrt(ss * (1.0 / HEAD_DIM) + EPS)) * gain
            partner = jnp.dot(y.astype(BF16), perm_ref[...],
                              preferred_element_type=F32)
            out = (y * cos + partner * sin) * scale
            o_ref[r0:r0 + rows, hp * pair:(hp + 1) * pair] = (
                out.astype(o_ref.dtype))

    def later_steps(r0, rows):
        @pl.when(j < n_qk_tiles)
        def _():
            qk_rows(r0, rows)

        @pl.when(j >= n_qk_tiles)
        def _():
            nb_q = (j >= nb_q_first) & (j < nb_q_first + WIDTH_B // TN_PROJ)
            scale = jnp.where(nb_q, ATTN_SCALE * LOG2_E, 1.0).astype(F32)
            o_ref[r0:r0 + rows, :] = (project(r0, rows) * scale).astype(
                o_ref.dtype)

    _normed_matmul_steps([(x_hbm, x_buf, sem.at[0], g_ref, 0)], a_ref,
                         later_steps, first_step_rows=qk_rows)


def _in_proj(x, g, w, layer, qg, kg, cos, sin):
    s, d = x.shape
    n = w.shape[2]
    ones, perm = _head_pair_matrices()
    pair = 2 * HEAD_DIM
    return pl.pallas_call(
        _in_proj_kernel,
        grid=(s // TM, n // TN_PROJ),
        in_specs=[
            pl.BlockSpec(memory_space=pl.ANY),
            pl.BlockSpec((1, d), lambda i, j: (0, 0)),
            pl.BlockSpec((None, d, TN_PROJ), lambda i, j: (layer, 0, j)),
            pl.BlockSpec((1, HEAD_DIM), lambda i, j: (0, 0)),
            pl.BlockSpec((1, HEAD_DIM), lambda i, j: (0, 0)),
            pl.BlockSpec((TM, HEAD_DIM), lambda i, j: (i, 0)),
            pl.BlockSpec((TM, HEAD_DIM), lambda i, j: (i, 0)),
            pl.BlockSpec((pair, pair), lambda i, j: (0, 0)),
            pl.BlockSpec((pair, pair), lambda i, j: (0, 0)),
        ],
        out_specs=pl.BlockSpec((TM, TN_PROJ), lambda i, j: (i, j)),
        out_shape=jax.ShapeDtypeStruct((s, n), BF16),
        scratch_shapes=[pltpu.VMEM((TM, d), F32), pltpu.VMEM((TM, d), BF16),
                        pltpu.SemaphoreType.DMA((1,))],
        compiler_params=_params("arbitrary", "arbitrary"),
        name="in_proj",
    )(x, g, w, qg, kg, cos, sin, ones, perm)


def _gqa_kernel(q_ref, k_ref, v_ref, o_ref, qs_ref, vt_ref, m_ref, acc_ref,
                st_ref):
    tq = o_ref.shape[0]
    s_len = k_ref.shape[0]
    n_chunks = s_len // TK
    i = pl.program_id(1)
    last_block = pl.num_programs(1) - 1

    def stack_queries(blk):
        r = pl.multiple_of(blk * tq, tq)
        for g in range(GQA_GROUP):
            qs_ref[g * tq:(g + 1) * tq, :] = (
                q_ref[pl.ds(r, tq), g * HEAD_DIM:(g + 1) * HEAD_DIM])

    def scores(c, slot):
        r = pl.multiple_of(c * TK, TK)
        st = lax.dot_general(k_ref[pl.ds(r, TK), :], qs_ref[...],
                             (((1,), (1,)), ((), ())),
                             preferred_element_type=F32)
        for g in range(GQA_GROUP):
            st_ref[slot, g] = st[:, g * tq:(g + 1) * tq]

    def softmax_pv(c, slot):
        for g in range(GQA_GROUP):
            st = st_ref[slot, g]
            m_old = m_ref[g]
            m_new = jnp.maximum(m_old, jnp.max(st, axis=0, keepdims=True))
            alpha = jnp.exp2(m_old - m_new)
            p = jnp.exp2(st - m_new)
            pv = jnp.dot(vt_ref[c], p.astype(BF16),
                         preferred_element_type=F32)
            acc_ref[g] = alpha * acc_ref[g] + pv
            m_ref[g] = m_new

    @pl.when(i == 0)
    def _():
        def transpose_chunk(c, carry):
            r = pl.multiple_of(c * TK, TK)
            vc = v_ref[pl.ds(r, TK), :].astype(F32)
            vt_ref[c, :HEAD_DIM, :] = vc.T.astype(vt_ref.dtype)
            vt_ref[c, HEAD_DIM:, :] = jnp.ones((GQA_ONES_ROWS, TK),
                                               vt_ref.dtype)
            return carry

        lax.fori_loop(0, n_chunks, transpose_chunk, 0)
        stack_queries(0)
        scores(0, 0)

    m_ref[...] = jnp.full(m_ref.shape, MASK_VALUE, F32)
    acc_ref[...] = jnp.zeros(acc_ref.shape, F32)

    def body(it, carry):
        c = GQA_UNROLL * it
        for u in range(GQA_UNROLL):
            scores(c + u + 1, (u + 1) % GQA_UNROLL)
            softmax_pv(c + u, u)
        return carry

    lax.fori_loop(0, n_chunks // GQA_UNROLL - 1, body, 0)
    c = n_chunks - GQA_UNROLL
    for u in range(GQA_UNROLL - 1):
        scores(c + u + 1, u + 1)
        softmax_pv(c + u, u)
    stack_queries(jnp.minimum(i + 1, last_block))
    scores(0, 0)
    softmax_pv(n_chunks - 1, GQA_UNROLL - 1)

    for g in range(GQA_GROUP):
        out = acc_ref[g, :HEAD_DIM, :] / acc_ref[g, HEAD_DIM:HEAD_DIM + 1, :]
        o_ref[:, g * HEAD_DIM:(g + 1) * HEAD_DIM] = out.T


def _gqa(proj):
    s = proj.shape[0]
    k_blk = WIDTH_A // HEAD_DIM
    v_blk = (WIDTH_A + KV_WIDTH_A) // HEAD_DIM
    gw = GQA_GROUP * HEAD_DIM
    return pl.pallas_call(
        _gqa_kernel,
        grid=(N_KV_A, s // TQ),
        in_specs=[
            pl.BlockSpec((s, gw), lambda h, i: (0, h)),
            pl.BlockSpec((s, HEAD_DIM), lambda h, i: (0, k_blk + h)),
            pl.BlockSpec((s, HEAD_DIM), lambda h, i: (0, v_blk + h)),
        ],
        out_specs=pl.BlockSpec((TQ, gw), lambda h, i: (i, h)),
        out_shape=jax.ShapeDtypeStruct((s, WIDTH_A), F32),
        scratch_shapes=[
            pltpu.VMEM((GQA_GROUP * TQ, HEAD_DIM), BF16),
            pltpu.VMEM((s // TK, HEAD_DIM + GQA_ONES_ROWS, TK), BF16),
            pltpu.VMEM((GQA_GROUP, 1, TQ), F32),
            pltpu.VMEM((GQA_GROUP, HEAD_DIM + GQA_ONES_ROWS, TQ), F32),
            pltpu.VMEM((GQA_UNROLL, GQA_GROUP, TK, TQ), F32),
        ],
        compiler_params=_params("arbitrary", "arbitrary"),
        name="gqa",
    )(proj, proj, proj)


def _na_classes(grid_rows):
    kh = min(WIN_H_MAX, grid_rows)
    n_blocks = grid_rows // NA_ROWS
    classes = []
    for rb in (0, 1, n_blocks - 1):
        start = min(max(NA_ROWS * rb - NA_ROWS, 0), grid_rows - NA_KROWS)
        table = []
        for qi in range(NA_ROWS):
            i = NA_ROWS * rb + qi
            rs = min(max(i - kh // 2, 0), grid_rows - kh)
            row = []
            for a in range(NA_KROWS):
                r = start + a
                row.append(r - i + (WIN_H_MAX - 1) if rs <= r < rs + kh else None)
            table.append(row)
        classes.append(table)
    return classes


def _na_kernel(rpb_ref, q_ref, k_ref, v_ref, o_ref, strip_ref, bias_ref, s_ref,
               *, classes, n_blocks):
    h = pl.program_id(0)
    n_rpb_rows = 2 * WIN_H_MAX - 1
    n_rpb_cols = 2 * WIN_W - 1
    qblk = NA_ROWS * GRID_W
    kblk = NA_KROWS * GRID_W

    jj = lax.broadcasted_iota(jnp.int32, (GRID_W, 2 * GRID_W), 0)
    cc = lax.broadcasted_iota(jnp.int32, (GRID_W, 2 * GRID_W), 1) % GRID_W
    rel = cc - jj + (WIN_W - 1)
    cs = jnp.clip(jj - WIN_W // 2, 0, GRID_W - WIN_W)
    col_ok = (cc >= cs) & (cc < cs + WIN_W)
    for dr in range(n_rpb_rows):
        base = (h * n_rpb_rows + dr) * n_rpb_cols

        def pick(d, t, base=base):
            return jnp.where(rel == d, rpb_ref[base + d], t)

        strip = lax.fori_loop(0, n_rpb_cols, pick,
                              jnp.zeros((GRID_W, 2 * GRID_W), F32))
        strip_ref[dr] = jnp.where(col_ok, strip * LOG2_E, MASK_VALUE)

    left = lax.broadcasted_iota(jnp.int32, (GRID_W, 2 * GRID_W), 1) < GRID_W
    masked = jnp.full((GRID_W, 2 * GRID_W), MASK_VALUE, F32)
    for cls, table in enumerate(classes):
        for qi in range(NA_ROWS):
            for ap in range(NA_KROWS // 2):
                dl, dr_ = table[qi][2 * ap], table[qi][2 * ap + 1]
                lhs = masked if dl is None else strip_ref[dl]
                rhs = masked if dr_ is None else strip_ref[dr_]
                bias_ref[cls, qi * GRID_W:(qi + 1) * GRID_W,
                         ap * 2 * GRID_W:(ap + 1) * 2 * GRID_W] = (
                             jnp.where(left, lhs, rhs))

    def key_start(rb):
        sb = jnp.clip(rb - 1, 0, n_blocks - NA_KROWS // NA_ROWS)
        return pl.multiple_of(sb * qblk, qblk)

    def scores(rb, slot):
        q0 = pl.multiple_of(rb * qblk, qblk)
        s_ref[slot] = lax.dot_general(q_ref[pl.ds(q0, qblk), :],
                                      k_ref[pl.ds(key_start(rb), kblk), :],
                                      (((1,), (1,)), ((), ())),
                                      preferred_element_type=F32)

    def softmax_pv(rb, slot):
        cls = jnp.where(rb == 0, 0, jnp.where(rb == n_blocks - 1, 2, 1))
        q0 = pl.multiple_of(rb * qblk, qblk)
        s = s_ref[slot] + bias_ref[cls]
        m = jnp.max(s, axis=-1, keepdims=True)
        p = jnp.exp2(s - m)
        l = jnp.sum(p, axis=-1, keepdims=True)
        o = jnp.dot(p.astype(BF16), v_ref[pl.ds(key_start(rb), kblk), :],
                    preferred_element_type=F32)
        o_ref[pl.ds(q0, qblk), :] = o / l

    scores(0, 0)

    def body(i, carry):
        rb = NA_UNROLL * i
        for u in range(NA_UNROLL):
            scores(jnp.minimum(rb + u + 1, n_blocks - 1), (u + 1) % NA_UNROLL)
            softmax_pv(rb + u, u)
        return carry

    lax.fori_loop(0, n_blocks // NA_UNROLL, body, 0)


def _na(proj, rpb_flat):
    s = proj.shape[0]
    grid_rows = s // GRID_W
    n_blocks = grid_rows // NA_ROWS
    q_blk = (WIDTH_A + 2 * KV_WIDTH_A) // HEAD_DIM
    k_blk = q_blk + N_HEADS_B
    v_blk = k_blk + N_HEADS_B
    kern = functools.partial(_na_kernel, classes=_na_classes(grid_rows),
                             n_blocks=n_blocks)
    return pl.pallas_call(
        kern,
        grid=(N_HEADS_B,),
        in_specs=[
            pl.BlockSpec(memory_space=pltpu.SMEM),
            pl.BlockSpec((s, HEAD_DIM), lambda h: (0, q_blk + h)),
            pl.BlockSpec((s, HEAD_DIM), lambda h: (0, k_blk + h)),
            pl.BlockSpec((s, HEAD_DIM), lambda h: (0, v_blk + h)),
        ],
        out_specs=pl.BlockSpec((s, HEAD_DIM), lambda h: (0, h)),
        out_shape=jax.ShapeDtypeStruct((s, WIDTH_B), F32),
        scratch_shapes=[
            pltpu.VMEM((2 * WIN_H_MAX - 1, GRID_W, 2 * GRID_W), F32),
            pltpu.VMEM((3, NA_ROWS * GRID_W, NA_KROWS * GRID_W), F32),
            pltpu.VMEM((NA_UNROLL, NA_ROWS * GRID_W, NA_KROWS * GRID_W), F32),
        ],
        compiler_params=_params("arbitrary"),
        name="na",
    )(rpb_flat, proj, proj, proj)


def _out_proj_kernel(oa_hbm, ob_hbm, ga_ref, gb_ref, w_ref, r_ref, o_ref,
                     oa_buf, ob_buf, a_ref, sem):
    def project_rows(r0, rows):
        acc = jnp.dot(a_ref[r0:r0 + rows, :], w_ref[...].astype(BF16),
                      preferred_element_type=F32)
        o_ref[r0:r0 + rows, :] = r_ref[r0:r0 + rows, :] + acc

    _normed_matmul_steps(
        [(oa_hbm, oa_buf, sem.at[0], ga_ref, 0),
         (ob_hbm, ob_buf, sem.at[1], gb_ref, oa_buf.shape[1])], a_ref,
        project_rows)


def _out_proj(oa, ob, ga, gb, w, layer, x):
    s, wa = oa.shape
    wb = ob.shape[1]
    d = w.shape[2]
    return pl.pallas_call(
        _out_proj_kernel,
        grid=(s // TM, d // TN_PROJ),
        in_specs=[
            pl.BlockSpec(memory_space=pl.ANY),
            pl.BlockSpec(memory_space=pl.ANY),
            pl.BlockSpec((1, wa), lambda i, j: (0, 0)),
            pl.BlockSpec((1, wb), lambda i, j: (0, 0)),
            pl.BlockSpec((None, wa + wb, TN_PROJ), lambda i, j: (layer, 0, j)),
            pl.BlockSpec((TM, TN_PROJ), lambda i, j: (i, j)),
        ],
        out_specs=pl.BlockSpec((TM, TN_PROJ), lambda i, j: (i, j)),
        out_shape=jax.ShapeDtypeStruct((s, d), F32),
        scratch_shapes=[pltpu.VMEM((TM, wa), F32), pltpu.VMEM((TM, wb), F32),
                        pltpu.VMEM((TM, wa + wb), BF16),
                        pltpu.SemaphoreType.DMA((2,))],
        compiler_params=_params("arbitrary", "arbitrary"),
        name="out_proj",
    )(oa, ob, ga, gb, w, x)


def _mem_kv_kernel(m_ref, g_ref, w_ref, o_ref, a_ref):
    @pl.when(pl.program_id(0) == 0)
    def _():
        _rms_norm_rows(m_ref, g_ref, a_ref)

    o_ref[...] = jnp.dot(a_ref[...], w_ref[...],
                         preferred_element_type=F32).astype(o_ref.dtype)


def _mem_kv(mem, g, wkv):
    m, d = mem.shape
    n = wkv.shape[1]
    return pl.pallas_call(
        _mem_kv_kernel,
        grid=(n // TN_PROJ,),
        in_specs=[
            pl.BlockSpec((m, d), lambda j: (0, 0)),
            pl.BlockSpec((1, d), lambda j: (0, 0)),
            pl.BlockSpec((d, TN_PROJ), lambda j: (0, j)),
        ],
        out_specs=pl.BlockSpec((m, TN_PROJ), lambda j: (0, j)),
        out_shape=jax.ShapeDtypeStruct((m, n), BF16),
        scratch_shapes=[pltpu.VMEM((m, d), BF16)],
        compiler_params=_params("arbitrary"),
        name="mem_kv",
    )(mem, g, wkv)


def _xattn_kernel(x_ref, g_ref, wq_ref, kv_ref, wo_ref, o_ref, h_ref, oc_ref):
    _rms_norm_rows(x_ref, g_ref, h_ref)
    q = jnp.dot(h_ref[...], wq_ref[...], preferred_element_type=F32)
    for hd in range(N_HEADS_MEM):
        lo, hi = hd * HEAD_DIM, (hd + 1) * HEAD_DIM
        qh = q[:, lo:hi].astype(BF16)
        kh = kv_ref[:, lo:hi]
        vh = kv_ref[:, MEM_WIDTH + lo:MEM_WIDTH + hi]
        s = lax.dot_general(qh, kh, (((1,), (1,)), ((), ())),
                            preferred_element_type=F32) * ATTN_SCALE
        m = jnp.max(s, axis=-1, keepdims=True)
        p = jnp.exp(s - m)
        l = jnp.sum(p, axis=-1, keepdims=True)
        o = jnp.dot(p.astype(BF16), vh, preferred_element_type=F32) / l
        oc_ref[:, lo:hi] = o.astype(oc_ref.dtype)
    o_ref[...] = x_ref[...] + jnp.dot(oc_ref[...], wo_ref[...],
                                      preferred_element_type=F32)


def _xattn(x, g, wq, kv, wo):
    s, d = x.shape
    m = kv.shape[0]
    return pl.pallas_call(
        _xattn_kernel,
        grid=(s // TM_X,),
        in_specs=[
            pl.BlockSpec((TM_X, d), lambda i: (i, 0)),
            pl.BlockSpec((1, d), lambda i: (0, 0)),
            pl.BlockSpec((d, MEM_WIDTH), lambda i: (0, 0)),
            pl.BlockSpec((m, 2 * MEM_WIDTH), lambda i: (0, 0)),
            pl.BlockSpec((MEM_WIDTH, d), lambda i: (0, 0)),
        ],
        out_specs=pl.BlockSpec((TM_X, d), lambda i: (i, 0)),
        out_shape=jax.ShapeDtypeStruct((s, d), F32),
        scratch_shapes=[pltpu.VMEM((TM_X, d), BF16),
                        pltpu.VMEM((TM_X, MEM_WIDTH), BF16)],
        compiler_params=_params("parallel"),
        name="xattn",
    )(x, g, wq, kv, wo)


def _final_norm_kernel(x_ref, g_ref, o_ref):
    _rms_norm_rows(x_ref, g_ref, o_ref)


def _final_norm(x, g):
    s, d = x.shape
    return pl.pallas_call(
        _final_norm_kernel,
        grid=(s // TM_FINAL,),
        in_specs=[pl.BlockSpec((TM_FINAL, d), lambda i: (i, 0)),
                  pl.BlockSpec((1, d), lambda i: (0, 0))],
        out_specs=pl.BlockSpec((TM_FINAL, d), lambda i: (i, 0)),
        out_shape=jax.ShapeDtypeStruct((s, d), F32),
        compiler_params=_params("parallel"),
        name="final_norm",
    )(x, g)


def _rope_tables(seq_len):
    grid_rows = seq_len // GRID_W
    axis_dim = HEAD_DIM // 2
    inv_freq = ROPE_THETA ** (-jnp.arange(0, axis_dim, 2, dtype=F32) / axis_dim)
    ang_r = jnp.arange(grid_rows, dtype=F32)[:, None] * inv_freq
    ang_c = jnp.arange(GRID_W, dtype=F32)[:, None] * inv_freq
    per_row = lambda v: jnp.repeat(v, GRID_W, axis=0)
    per_col = lambda v: jnp.tile(v, (grid_rows, 1))
    cos_r, sin_r = per_row(jnp.cos(ang_r)), per_row(jnp.sin(ang_r))
    cos_c, sin_c = per_col(jnp.cos(ang_c)), per_col(jnp.sin(ang_c))
    cos = jnp.concatenate([cos_r, cos_r, cos_c, cos_c], axis=-1)
    sin = jnp.concatenate([-sin_r, sin_r, -sin_c, sin_c], axis=-1)
    return cos, sin


def kernel(x, mem, ffn1_norm, ffn1_w_gate, ffn1_w_up, ffn1_w_down, mix_norm, w_in, q_norm_a, k_norm_a, rpb_b, out_norm_a, out_norm_b, w_out, xattn_norm, mem_norm, xattn_wq, xattn_wkv, xattn_wo, ffn2_norm, ffn2_w_gate, ffn2_w_up, ffn2_w_down, final_norm):
    batch, seq_len, d_model = x.shape
    depth = w_in.shape[0]
    assert batch == 1 and mem.shape[0] == 1
    assert seq_len % GRID_W == 0
    assert (seq_len // GRID_W) % (NA_UNROLL * NA_ROWS) == 0
    assert seq_len % (GQA_UNROLL * TK) == 0 and seq_len % TM == 0

    cos, sin = _rope_tables(seq_len)
    row = lambda v: v.reshape(1, -1)
    xs = x[0]
    mem2 = mem[0]
    for l in range(depth):
        bf = lambda w: w[l].astype(BF16)

        act = _ffn_up(xs, row(ffn1_norm[l]), ffn1_w_gate, ffn1_w_up, l)
        xs = _ffn_down(act, ffn1_w_down, l, xs)

        proj = _in_proj(xs, row(mix_norm[l]), w_in, l, row(q_norm_a[l]),
                        row(k_norm_a[l]), cos, sin)
        oa = _gqa(proj)
        ob = _na(proj, rpb_b[l].reshape(-1))
        xs = _out_proj(oa, ob, row(out_norm_a[l]), row(out_norm_b[l]),
                       w_out, l, xs)

        kv = _mem_kv(mem2, row(mem_norm[l]), bf(xattn_wkv))
        xs = _xattn(xs, row(xattn_norm[l]), bf(xattn_wq), kv, bf(xattn_wo))

        act = _ffn_up(xs, row(ffn2_norm[l]), ffn2_w_gate, ffn2_w_up, l)
        xs = _ffn_down(act, ffn2_w_down, l, xs)

    return _final_norm(xs, row(final_norm))[None]
```

```python
import functools

import jax
import jax.numpy as jnp
import numpy as np
from jax import lax
from jax.experimental import pallas as pl
from jax.experimental.pallas import tpu as pltpu

F32 = jnp.float32
BF16 = jnp.bfloat16

HEAD_DIM = 128
N_HEADS_A = 16
N_KV_A = 4
GQA_GROUP = N_HEADS_A // N_KV_A
N_HEADS_B = 16
WIDTH_A = N_HEADS_A * HEAD_DIM
WIDTH_B = N_HEADS_B * HEAD_DIM
KV_WIDTH_A = N_KV_A * HEAD_DIM
GRID_W = 64
WIN_H_MAX = 8
WIN_W = 16
N_HEADS_MEM = 4
MEM_WIDTH = N_HEADS_MEM * HEAD_DIM
ROPE_THETA = 10000.0
EPS = 1e-6
ATTN_SCALE = HEAD_DIM ** -0.5
LOG2_E = 1.4426950408889634
MASK_VALUE = -1e30

V7X_VMEM_BYTES = 64 * 1024 * 1024
VMEM_LIMIT_BYTES = V7X_VMEM_BYTES - 6 * 1024 * 1024

TM = 1024
TN_UP = 256
TN_DOWN = 512
KSPLIT_DOWN = 2
TN_PROJ = 512
NORM_COLS = 1024
NORM_CHUNK = 256
TQ = 256
TK = 512
GQA_ONES_ROWS = 16
GQA_UNROLL = 4
NA_ROWS = 4
NA_KROWS = 12
NA_UNROLL = 4
TM_X = 256
TM_FINAL = 512


def _params(*sem):
    return pltpu.CompilerParams(dimension_semantics=sem,
                                vmem_limit_bytes=VMEM_LIMIT_BYTES)


def _rms_norm_chunk(x_ref, g_ref, a_ref, r, chunk, col_off=0):
    if len(x_ref.shape) == 3:
        cb = x_ref.shape[2]
        d = x_ref.shape[0] * cb
        load = lambda c0: x_ref[c0 // cb, pl.ds(r, chunk), :]
    else:
        d = x_ref.shape[1]
        cb = min(NORM_COLS, d)
        load = lambda c0: x_ref[pl.ds(r, chunk), c0:c0 + cb]
    ss = jnp.zeros((chunk, 1), F32)
    for c0 in range(0, d, cb):
        xb = load(c0).astype(F32)
        ss = ss + jnp.sum(xb * xb, axis=-1, keepdims=True)
    inv = lax.rsqrt(ss * (1.0 / d) + EPS)
    for c0 in range(0, d, cb):
        xb = load(c0).astype(F32)
        y = (xb * inv) * g_ref[:, c0:c0 + cb]
        a_ref[pl.ds(r, chunk),
              col_off + c0:col_off + c0 + cb] = y.astype(a_ref.dtype)


def _rms_norm_rows(x_ref, g_ref, a_ref, col_off=0):
    tm = x_ref.shape[0]
    chunk = min(NORM_CHUNK, tm)

    def body(c, carry):
        _rms_norm_chunk(x_ref, g_ref, a_ref, pl.multiple_of(c * chunk, chunk),
                        chunk, col_off)
        return carry

    lax.fori_loop(0, tm // chunk, body, 0)


class _RowTileCopy:
    def __init__(self, x_hbm, x_buf, sem, tile):
        blocks, rows, cb = x_buf.shape
        r = pl.multiple_of(tile * rows, rows)
        self.copies = [
            pltpu.make_async_copy(x_hbm.at[pl.ds(r, rows), pl.ds(b * cb, cb)],
                                  x_buf.at[b], sem.at[b])
            for b in range(blocks)]

    def start(self):
        for c in self.copies:
            c.start()

    def wait(self):
        for c in self.copies:
            c.wait()


def _row_tile_copy(x_hbm, x_buf, sem, tile):
    return _RowTileCopy(x_hbm, x_buf, sem, tile)


def _normed_matmul_steps(sources, a_ref, compute_rows, first_step_rows=None):
    i = pl.program_id(0)
    j = pl.program_id(1)
    tm = a_ref.shape[0]
    chunk = min(NORM_CHUNK, tm)

    @pl.when(j == 0)
    def _():
        @pl.when(i == 0)
        def _():
            for x_hbm, x_buf, sem, _, _ in sources:
                _row_tile_copy(x_hbm, x_buf, sem, 0).start()

        for x_hbm, x_buf, sem, _, _ in sources:
            _row_tile_copy(x_hbm, x_buf, sem, i).wait()
        for r0 in range(0, tm, chunk):
            for _, x_buf, _, g_ref, col_off in sources:
                _rms_norm_chunk(x_buf, g_ref, a_ref, r0, chunk, col_off)
            (first_step_rows or compute_rows)(r0, chunk)

    @pl.when((j == 1) & (i + 1 < pl.num_programs(0)))
    def _():
        for x_hbm, x_buf, sem, _, _ in sources:
            _row_tile_copy(x_hbm, x_buf, sem, i + 1).start()

    @pl.when(j > 0)
    def _():
        compute_rows(0, tm)


def _ffn_up_kernel(x_hbm, g_ref, wg_ref, wu_ref, o_ref, x_buf, a_ref, sem):
    def swiglu_rows(r0, rows):
        a = a_ref[r0:r0 + rows, :]
        gate = jnp.dot(a, wg_ref[...].astype(BF16), preferred_element_type=F32)
        up = jnp.dot(a, wu_ref[...].astype(BF16), preferred_element_type=F32)
        o_ref[r0:r0 + rows, :] = (
            (gate * jax.nn.sigmoid(gate)) * up).astype(o_ref.dtype)

    _normed_matmul_steps([(x_hbm, x_buf, sem, g_ref, 0)], a_ref,
                         swiglu_rows)


def _ffn_up(x, g, wg, wu, layer):
    s, d = x.shape
    dff = wg.shape[2]
    return pl.pallas_call(
        _ffn_up_kernel,
        grid=(s // TM, dff // TN_UP),
        in_specs=[
            pl.BlockSpec(memory_space=pl.ANY),
            pl.BlockSpec((1, d), lambda i, j: (0, 0)),
            pl.BlockSpec((None, d, TN_UP), lambda i, j: (layer, 0, j)),
            pl.BlockSpec((None, d, TN_UP), lambda i, j: (layer, 0, j)),
        ],
        out_specs=pl.BlockSpec((TM, TN_UP), lambda i, j: (i, j)),
        out_shape=jax.ShapeDtypeStruct((s, dff), BF16),
        scratch_shapes=[pltpu.VMEM((d // NORM_COLS, TM, NORM_COLS), F32),
                        pltpu.VMEM((TM, d), BF16),
                        pltpu.SemaphoreType.DMA((d // NORM_COLS,))],
        compiler_params=_params("arbitrary", "arbitrary"),
        name="ffn_up",
    )(x, g, wg, wu)


def _ffn_down_kernel(act_hbm, w_ref, r_ref, o_ref, a0_buf, a1_buf, sem):
    i = pl.program_id(0)
    j = pl.program_id(1)
    k = pl.program_id(2)
    n_i = pl.num_programs(0)
    n_j = pl.num_programs(1)
    tm, tk = a0_buf.shape
    bufs = (a0_buf, a1_buf)

    def half_copy(tile, half):
        r = pl.multiple_of(tile * tm, tm)
        return pltpu.make_async_copy(
            act_hbm.at[pl.ds(r, tm), pl.ds(half * tk, tk)], bufs[half],
            sem.at[half])

    @pl.when((j == 0) & (k == 0))
    def _():
        @pl.when(i == 0)
        def _():
            half_copy(0, 0).start()

        half_copy(i, 1).start()
        half_copy(i, 0).wait()

    @pl.when((j == 0) & (k == 1))
    def _():
        half_copy(i, 1).wait()

    @pl.when((j == n_j - 1) & (k == 1) & (i + 1 < n_i))
    def _():
        half_copy(i + 1, 0).start()

    for half in range(KSPLIT_DOWN):
        @pl.when(k == half)
        def _(half=half):
            part = 0.5 * jnp.dot(bufs[half][...], w_ref[...].astype(BF16),
                                 preferred_element_type=F32)
            if half == 0:
                o_ref[...] = r_ref[...] + part
            else:
                o_ref[...] += part


def _ffn_down(act, wd, layer, x):
    s, dff = act.shape
    d = wd.shape[2]
    assert KSPLIT_DOWN == 2 and d // TN_DOWN >= 2
    tk = dff // KSPLIT_DOWN
    return pl.pallas_call(
        _ffn_down_kernel,
        grid=(s // TM, d // TN_DOWN, KSPLIT_DOWN),
        in_specs=[
            pl.BlockSpec(memory_space=pl.ANY),
            pl.BlockSpec((None, tk, TN_DOWN), lambda i, j, k: (layer, k, j)),
            pl.BlockSpec((TM, TN_DOWN), lambda i, j, k: (i, j)),
        ],
        out_specs=pl.BlockSpec((TM, TN_DOWN), lambda i, j, k: (i, j)),
        out_shape=jax.ShapeDtypeStruct((s, d), F32),
        scratch_shapes=[pltpu.VMEM((TM, tk), BF16), pltpu.VMEM((TM, tk), BF16),
                        pltpu.SemaphoreType.DMA((KSPLIT_DOWN,))],
        compiler_params=_params("arbitrary", "arbitrary", "arbitrary"),
        name="ffn_down",
    )(act, wd, x)


def _head_pair_matrices():
    lane = np.arange(2 * HEAD_DIM)
    head = lane // HEAD_DIM
    ones = (head[:, None] == head[None, :]).astype(np.float32)
    first_half = (lane % (HEAD_DIM // 2)) < (HEAD_DIM // 4)
    partner = np.where(first_half, lane + HEAD_DIM // 4, lane - HEAD_DIM // 4)
    perm = np.zeros((2 * HEAD_DIM, 2 * HEAD_DIM), np.float32)
    perm[partner, lane] = 1.0
    return jnp.asarray(ones, BF16), jnp.asarray(perm, BF16)


def _in_proj_kernel(x_hbm, g_ref, w_ref, qg_ref, kg_ref, cos_ref, sin_ref,
                    ones_ref, perm_ref, o_ref, x_buf, a_ref, sem):
    j = pl.program_id(1)
    n_q_tiles = WIDTH_A // TN_PROJ
    n_qk_tiles = (WIDTH_A + KV_WIDTH_A) // TN_PROJ
    nb_q_first = (WIDTH_A + 2 * KV_WIDTH_A) // TN_PROJ
    def project(r0, rows):
        return jnp.dot(a_ref[r0:r0 + rows, :], w_ref[...].astype(BF16),
                       preferred_element_type=F32)

    def qk_rows(r0, rows):
        acc = project(r0, rows)
        is_q = j < n_q_tiles
        gain = jnp.where(is_q, qg_ref[...], kg_ref[...])
        gain = jnp.concatenate([gain, gain], axis=1)
        scale = jnp.where(is_q, ATTN_SCALE * LOG2_E, 1.0).astype(F32)
        cos = cos_ref[r0:r0 + rows, :]
        sin = sin_ref[r0:r0 + rows, :]
        cos = jnp.concatenate([cos, cos], axis=1)
        sin = jnp.concatenate([sin, sin], axis=1)
        pair = 2 * HEAD_DIM
        for hp in range(TN_PROJ // pair):
            xh = acc[:, hp * pair:(hp + 1) * pair]
            ss = jnp.dot((xh * xh).astype(BF16), ones_ref[...],
                         preferred_element_type=F32)
            y = (xh * lax.rsqrt(ss * (1.0 / HEAD_DIM) + EPS)) * gain
            partner = jnp.dot(y.astype(BF16), perm_ref[...],
                              preferred_element_type=F32)
            out = (y * cos + partner * sin) * scale
            o_ref[r0:r0 + rows, hp * pair:(hp + 1) * pair] = (
                out.astype(o_ref.dtype))

    def later_steps(r0, rows):
        @pl.when(j < n_qk_tiles)
        def _():
            qk_rows(r0, rows)

        @pl.when(j >= n_qk_tiles)
        def _():
            nb_q = (j >= nb_q_first) & (j < nb_q_first + WIDTH_B // TN_PROJ)
            scale = jnp.where(nb_q, ATTN_SCALE * LOG2_E, 1.0).astype(F32)
            o_ref[r0:r0 + rows, :] = (project(r0, rows) * scale).astype(
                o_ref.dtype)

    _normed_matmul_steps([(x_hbm, x_buf, sem, g_ref, 0)], a_ref,
                         later_steps, first_step_rows=qk_rows)


def _in_proj(x, g, w, layer, qg, kg, cos, sin):
    s, d = x.shape
    n = w.shape[2]
    ones, perm = _head_pair_matrices()
    pair = 2 * HEAD_DIM
    return pl.pallas_call(
        _in_proj_kernel,
        grid=(s // TM, n // TN_PROJ),
        in_specs=[
            pl.BlockSpec(memory_space=pl.ANY),
            pl.BlockSpec((1, d), lambda i, j: (0, 0)),
            pl.BlockSpec((None, d, TN_PROJ), lambda i, j: (layer, 0, j)),
            pl.BlockSpec((1, HEAD_DIM), lambda i, j: (0, 0)),
            pl.BlockSpec((1, HEAD_DIM), lambda i, j: (0, 0)),
            pl.BlockSpec((TM, HEAD_DIM), lambda i, j: (i, 0)),
            pl.BlockSpec((TM, HEAD_DIM), lambda i, j: (i, 0)),
            pl.BlockSpec((pair, pair), lambda i, j: (0, 0)),
            pl.BlockSpec((pair, pair), lambda i, j: (0, 0)),
        ],
        out_specs=pl.BlockSpec((TM, TN_PROJ), lambda i, j: (i, j)),
        out_shape=jax.ShapeDtypeStruct((s, n), BF16),
        scratch_shapes=[pltpu.VMEM((d // NORM_COLS, TM, NORM_COLS), F32),
                        pltpu.VMEM((TM, d), BF16),
                        pltpu.SemaphoreType.DMA((d // NORM_COLS,))],
        compiler_params=_params("arbitrary", "arbitrary"),
        name="in_proj",
    )(x, g, w, qg, kg, cos, sin, ones, perm)


def _gqa_kernel(q_ref, k_ref, v_ref, o_ref, qs_ref, vt_ref, m_ref, acc_ref,
                st_ref):
    tq = o_ref.shape[0]
    s_len = k_ref.shape[0]
    n_chunks = s_len // TK
    i = pl.program_id(1)
    last_block = pl.num_programs(1) - 1

    def stack_queries(blk):
        r = pl.multiple_of(blk * tq, tq)
        for g in range(GQA_GROUP):
            qs_ref[g * tq:(g + 1) * tq, :] = (
                q_ref[pl.ds(r, tq), g * HEAD_DIM:(g + 1) * HEAD_DIM])

    def scores(c, slot):
        r = pl.multiple_of(c * TK, TK)
        st = lax.dot_general(k_ref[pl.ds(r, TK), :], qs_ref[...],
                             (((1,), (1,)), ((), ())),
                             preferred_element_type=F32)
        for g in range(GQA_GROUP):
            st_ref[slot, g] = st[:, g * tq:(g + 1) * tq]

    def softmax_pv(c, slot):
        for g in range(GQA_GROUP):
            st = st_ref[slot, g]
            m_old = m_ref[g]
            m_new = jnp.maximum(m_old, jnp.max(st, axis=0, keepdims=True))
            alpha = jnp.exp2(m_old - m_new)
            p = jnp.exp2(st - m_new)
            pv = jnp.dot(vt_ref[c], p.astype(BF16),
                         preferred_element_type=F32)
            acc_ref[g] = alpha * acc_ref[g] + pv
            m_ref[g] = m_new

    @pl.when(i == 0)
    def _():
        def transpose_chunk(c, carry):
            r = pl.multiple_of(c * TK, TK)
            vc = v_ref[pl.ds(r, TK), :].astype(F32)
            vt_ref[c, :HEAD_DIM, :] = vc.T.astype(vt_ref.dtype)
            vt_ref[c, HEAD_DIM:, :] = jnp.ones((GQA_ONES_ROWS, TK),
                                               vt_ref.dtype)
            return carry

        lax.fori_loop(0, n_chunks, transpose_chunk, 0)
        stack_queries(0)
        scores(0, 0)

    m_ref[...] = jnp.full(m_ref.shape, MASK_VALUE, F32)
    acc_ref[...] = jnp.zeros(acc_ref.shape, F32)

    def body(it, carry):
        c = GQA_UNROLL * it
        for u in range(GQA_UNROLL):
            scores(c + u + 1, (u + 1) % GQA_UNROLL)
            softmax_pv(c + u, u)
        return carry

    lax.fori_loop(0, n_chunks // GQA_UNROLL - 1, body, 0)
    c = n_chunks - GQA_UNROLL
    for u in range(GQA_UNROLL - 1):
        scores(c + u + 1, u + 1)
        softmax_pv(c + u, u)
    stack_queries(jnp.minimum(i + 1, last_block))
    scores(0, 0)
    softmax_pv(n_chunks - 1, GQA_UNROLL - 1)

    for g in range(GQA_GROUP):
        out = acc_ref[g, :HEAD_DIM, :] / acc_ref[g, HEAD_DIM:HEAD_DIM + 1, :]
        o_ref[:, g * HEAD_DIM:(g + 1) * HEAD_DIM] = out.T


def _gqa(proj):
    s = proj.shape[0]
    k_blk = WIDTH_A // HEAD_DIM
    v_blk = (WIDTH_A + KV_WIDTH_A) // HEAD_DIM
    gw = GQA_GROUP * HEAD_DIM
    return pl.pallas_call(
        _gqa_kernel,
        grid=(N_KV_A, s // TQ),
        in_specs=[
            pl.BlockSpec((s, gw), lambda h, i: (0, h)),
            pl.BlockSpec((s, HEAD_DIM), lambda h, i: (0, k_blk + h)),
            pl.BlockSpec((s, HEAD_DIM), lambda h, i: (0, v_blk + h)),
        ],
        out_specs=pl.BlockSpec((TQ, gw), lambda h, i: (i, h)),
        out_shape=jax.ShapeDtypeStruct((s, WIDTH_A), F32),
        scratch_shapes=[
            pltpu.VMEM((GQA_GROUP * TQ, HEAD_DIM), BF16),
            pltpu.VMEM((s // TK, HEAD_DIM + GQA_ONES_ROWS, TK), BF16),
            pltpu.VMEM((GQA_GROUP, 1, TQ), F32),
            pltpu.VMEM((GQA_GROUP, HEAD_DIM + GQA_ONES_ROWS, TQ), F32),
            pltpu.VMEM((GQA_UNROLL, GQA_GROUP, TK, TQ), F32),
        ],
        compiler_params=_params("arbitrary", "arbitrary"),
        name="gqa",
    )(proj, proj, proj)


def _na_classes(grid_rows):
    kh = min(WIN_H_MAX, grid_rows)
    n_blocks = grid_rows // NA_ROWS
    classes = []
    for rb in (0, 1, n_blocks - 1):
        start = min(max(NA_ROWS * rb - NA_ROWS, 0), grid_rows - NA_KROWS)
        table = []
        for qi in range(NA_ROWS):
            i = NA_ROWS * rb + qi
            rs = min(max(i - kh // 2, 0), grid_rows - kh)
            row = []
            for a in range(NA_KROWS):
                r = start + a
                row.append(r - i + (WIN_H_MAX - 1) if rs <= r < rs + kh else None)
            table.append(row)
        classes.append(table)
    return classes


def _na_kernel(rpb_ref, q_ref, k_ref, v_ref, o_ref, strip_ref, bias_ref, s_ref,
               *, classes, n_blocks):
    h = pl.program_id(0)
    n_rpb_rows = 2 * WIN_H_MAX - 1
    n_rpb_cols = 2 * WIN_W - 1
    qblk = NA_ROWS * GRID_W
    kblk = NA_KROWS * GRID_W

    jj = lax.broadcasted_iota(jnp.int32, (GRID_W, 2 * GRID_W), 0)
    cc = lax.broadcasted_iota(jnp.int32, (GRID_W, 2 * GRID_W), 1) % GRID_W
    rel = cc - jj + (WIN_W - 1)
    cs = jnp.clip(jj - WIN_W // 2, 0, GRID_W - WIN_W)
    col_ok = (cc >= cs) & (cc < cs + WIN_W)
    for dr in range(n_rpb_rows):
        base = (h * n_rpb_rows + dr) * n_rpb_cols

        def pick(d, t, base=base):
            return jnp.where(rel == d, rpb_ref[base + d], t)

        strip = lax.fori_loop(0, n_rpb_cols, pick,
                              jnp.zeros((GRID_W, 2 * GRID_W), F32))
        strip_ref[dr] = jnp.where(col_ok, strip * LOG2_E, MASK_VALUE)

    left = lax.broadcasted_iota(jnp.int32, (GRID_W, 2 * GRID_W), 1) < GRID_W
    masked = jnp.full((GRID_W, 2 * GRID_W), MASK_VALUE, F32)
    for cls, table in enumerate(classes):
        for qi in range(NA_ROWS):
            for ap in range(NA_KROWS // 2):
                dl, dr_ = table[qi][2 * ap], table[qi][2 * ap + 1]
                lhs = masked if dl is None else strip_ref[dl]
                rhs = masked if dr_ is None else strip_ref[dr_]
                bias_ref[cls, qi * GRID_W:(qi + 1) * GRID_W,
                         ap * 2 * GRID_W:(ap + 1) * 2 * GRID_W] = (
                             jnp.where(left, lhs, rhs))

    def key_start(rb):
        sb = jnp.clip(rb - 1, 0, n_blocks - NA_KROWS // NA_ROWS)
        return pl.multiple_of(sb * qblk, qblk)

    def scores(rb, slot):
        q0 = pl.multiple_of(rb * qblk, qblk)
        s_ref[slot] = lax.dot_general(q_ref[pl.ds(q0, qblk), :],
                                      k_ref[pl.ds(key_start(rb), kblk), :],
                                      (((1,), (1,)), ((), ())),
                                      preferred_element_type=F32)

    def softmax_pv(rb, slot):
        cls = jnp.where(rb == 0, 0, jnp.where(rb == n_blocks - 1, 2, 1))
        q0 = pl.multiple_of(rb * qblk, qblk)
        s = s_ref[slot] + bias_ref[cls]
        m = jnp.max(s, axis=-1, keepdims=True)
        p = jnp.exp2(s - m)
        l = jnp.sum(p, axis=-1, keepdims=True)
        o = jnp.dot(p.astype(BF16), v_ref[pl.ds(key_start(rb), kblk), :],
                    preferred_element_type=F32)
        o_ref[pl.ds(q0, qblk), :] = o / l

    scores(0, 0)

    def body(i, carry):
        rb = NA_UNROLL * i
        for u in range(NA_UNROLL):
            scores(jnp.minimum(rb + u + 1, n_blocks - 1), (u + 1) % NA_UNROLL)
            softmax_pv(rb + u, u)
        return carry

    lax.fori_loop(0, n_blocks // NA_UNROLL, body, 0)


def _na(proj, rpb_flat):
    s = proj.shape[0]
    grid_rows = s // GRID_W
    n_blocks = grid_rows // NA_ROWS
    q_blk = (WIDTH_A + 2 * KV_WIDTH_A) // HEAD_DIM
    k_blk = q_blk + N_HEADS_B
    v_blk = k_blk + N_HEADS_B
    kern = functools.partial(_na_kernel, classes=_na_classes(grid_rows),
                             n_blocks=n_blocks)
    return pl.pallas_call(
        kern,
        grid=(N_HEADS_B,),
        in_specs=[
            pl.BlockSpec(memory_space=pltpu.SMEM),
            pl.BlockSpec((s, HEAD_DIM), lambda h: (0, q_blk + h)),
            pl.BlockSpec((s, HEAD_DIM), lambda h: (0, k_blk + h)),
            pl.BlockSpec((s, HEAD_DIM), lambda h: (0, v_blk + h)),
        ],
        out_specs=pl.BlockSpec((s, HEAD_DIM), lambda h: (0, h)),
        out_shape=jax.ShapeDtypeStruct((s, WIDTH_B), F32),
        scratch_shapes=[
            pltpu.VMEM((2 * WIN_H_MAX - 1, GRID_W, 2 * GRID_W), F32),
            pltpu.VMEM((3, NA_ROWS * GRID_W, NA_KROWS * GRID_W), F32),
            pltpu.VMEM((NA_UNROLL, NA_ROWS * GRID_W, NA_KROWS * GRID_W), F32),
        ],
        compiler_params=_params("arbitrary"),
        name="na",
    )(rpb_flat, proj, proj, proj)


def _out_proj_kernel(oa_hbm, ob_hbm, ga_ref, gb_ref, w_ref, r_ref, o_ref,
                     oa_buf, ob_buf, a_ref, sem_a, sem_b):
    def project_rows(r0, rows):
        acc = jnp.dot(a_ref[r0:r0 + rows, :], w_ref[...].astype(BF16),
                      preferred_element_type=F32)
        o_ref[r0:r0 + rows, :] = r_ref[r0:r0 + rows, :] + acc

    _normed_matmul_steps(
        [(oa_hbm, oa_buf, sem_a, ga_ref, 0),
         (ob_hbm, ob_buf, sem_b, gb_ref, oa_buf.shape[0] * oa_buf.shape[2])],
        a_ref,
        project_rows)


def _out_proj(oa, ob, ga, gb, w, layer, x):
    s, wa = oa.shape
    wb = ob.shape[1]
    d = w.shape[2]
    return pl.pallas_call(
        _out_proj_kernel,
        grid=(s // TM, d // TN_PROJ),
        in_specs=[
            pl.BlockSpec(memory_space=pl.ANY),
            pl.BlockSpec(memory_space=pl.ANY),
            pl.BlockSpec((1, wa), lambda i, j: (0, 0)),
            pl.BlockSpec((1, wb), lambda i, j: (0, 0)),
            pl.BlockSpec((None, wa + wb, TN_PROJ), lambda i, j: (layer, 0, j)),
            pl.BlockSpec((TM, TN_PROJ), lambda i, j: (i, j)),
        ],
        out_specs=pl.BlockSpec((TM, TN_PROJ), lambda i, j: (i, j)),
        out_shape=jax.ShapeDtypeStruct((s, d), F32),
        scratch_shapes=[pltpu.VMEM((wa // NORM_COLS, TM, NORM_COLS), F32),
                        pltpu.VMEM((wb // NORM_COLS, TM, NORM_COLS), F32),
                        pltpu.VMEM((TM, wa + wb), BF16),
                        pltpu.SemaphoreType.DMA((wa // NORM_COLS,)),
                        pltpu.SemaphoreType.DMA((wb // NORM_COLS,))],
        compiler_params=_params("arbitrary", "arbitrary"),
        name="out_proj",
    )(oa, ob, ga, gb, w, x)


def _mem_kv_kernel(m_ref, g_ref, w_ref, o_ref, a_ref):
    @pl.when(pl.program_id(0) == 0)
    def _():
        _rms_norm_rows(m_ref, g_ref, a_ref)

    o_ref[...] = jnp.dot(a_ref[...], w_ref[...],
                         preferred_element_type=F32).astype(o_ref.dtype)


def _mem_kv(mem, g, wkv):
    m, d = mem.shape
    n = wkv.shape[1]
    return pl.pallas_call(
        _mem_kv_kernel,
        grid=(n // TN_PROJ,),
        in_specs=[
            pl.BlockSpec((m, d), lambda j: (0, 0)),
            pl.BlockSpec((1, d), lambda j: (0, 0)),
            pl.BlockSpec((d, TN_PROJ), lambda j: (0, j)),
        ],
        out_specs=pl.BlockSpec((m, TN_PROJ), lambda j: (0, j)),
        out_shape=jax.ShapeDtypeStruct((m, n), BF16),
        scratch_shapes=[pltpu.VMEM((m, d), BF16)],
        compiler_params=_params("arbitrary"),
        name="mem_kv",
    )(mem, g, wkv)


def _xattn_kernel(x_ref, g_ref, wq_ref, kv_ref, wo_ref, o_ref, h_ref, oc_ref):
    _rms_norm_rows(x_ref, g_ref, h_ref)
    q = jnp.dot(h_ref[...], wq_ref[...], preferred_element_type=F32)
    for hd in range(N_HEADS_MEM):
        lo, hi = hd * HEAD_DIM, (hd + 1) * HEAD_DIM
        qh = q[:, lo:hi].astype(BF16)
        kh = kv_ref[:, lo:hi]
        vh = kv_ref[:, MEM_WIDTH + lo:MEM_WIDTH + hi]
        s = lax.dot_general(qh, kh, (((1,), (1,)), ((), ())),
                            preferred_element_type=F32) * ATTN_SCALE
        m = jnp.max(s, axis=-1, keepdims=True)
        p = jnp.exp(s - m)
        l = jnp.sum(p, axis=-1, keepdims=True)
        o = jnp.dot(p.astype(BF16), vh, preferred_element_type=F32) / l
        oc_ref[:, lo:hi] = o.astype(oc_ref.dtype)
    o_ref[...] = x_ref[...] + jnp.dot(oc_ref[...], wo_ref[...],
                                      preferred_element_type=F32)


def _xattn(x, g, wq, kv, wo):
    s, d = x.shape
    m = kv.shape[0]
    return pl.pallas_call(
        _xattn_kernel,
        grid=(s // TM_X,),
        in_specs=[
            pl.BlockSpec((TM_X, d), lambda i: (i, 0)),
            pl.BlockSpec((1, d), lambda i: (0, 0)),
            pl.BlockSpec((d, MEM_WIDTH), lambda i: (0, 0)),
            pl.BlockSpec((m, 2 * MEM_WIDTH), lambda i: (0, 0)),
            pl.BlockSpec((MEM_WIDTH, d), lambda i: (0, 0)),
        ],
        out_specs=pl.BlockSpec((TM_X, d), lambda i: (i, 0)),
        out_shape=jax.ShapeDtypeStruct((s, d), F32),
        scratch_shapes=[pltpu.VMEM((TM_X, d), BF16),
                        pltpu.VMEM((TM_X, MEM_WIDTH), BF16)],
        compiler_params=_params("parallel"),
        name="xattn",
    )(x, g, wq, kv, wo)


def _final_norm_kernel(x_ref, g_ref, o_ref):
    _rms_norm_rows(x_ref, g_ref, o_ref)


def _final_norm(x, g):
    s, d = x.shape
    return pl.pallas_call(
        _final_norm_kernel,
        grid=(s // TM_FINAL,),
        in_specs=[pl.BlockSpec((TM_FINAL, d), lambda i: (i, 0)),
                  pl.BlockSpec((1, d), lambda i: (0, 0))],
        out_specs=pl.BlockSpec((TM_FINAL, d), lambda i: (i, 0)),
        out_shape=jax.ShapeDtypeStruct((s, d), F32),
        compiler_params=_params("parallel"),
        name="final_norm",
    )(x, g)


def _rope_tables(seq_len):
    grid_rows = seq_len // GRID_W
    axis_dim = HEAD_DIM // 2
    inv_freq = ROPE_THETA ** (-jnp.arange(0, axis_dim, 2, dtype=F32) / axis_dim)
    ang_r = jnp.arange(grid_rows, dtype=F32)[:, None] * inv_freq
    ang_c = jnp.arange(GRID_W, dtype=F32)[:, None] * inv_freq
    per_row = lambda v: jnp.repeat(v, GRID_W, axis=0)
    per_col = lambda v: jnp.tile(v, (grid_rows, 1))
    cos_r, sin_r = per_row(jnp.cos(ang_r)), per_row(jnp.sin(ang_r))
    cos_c, sin_c = per_col(jnp.cos(ang_c)), per_col(jnp.sin(ang_c))
    cos = jnp.concatenate([cos_r, cos_r, cos_c, cos_c], axis=-1)
    sin = jnp.concatenate([-sin_r, sin_r, -sin_c, sin_c], axis=-1)
    return cos, sin


def kernel(x, mem, ffn1_norm, ffn1_w_gate, ffn1_w_up, ffn1_w_down, mix_norm, w_in, q_norm_a, k_norm_a, rpb_b, out_norm_a, out_norm_b, w_out, xattn_norm, mem_norm, xattn_wq, xattn_wkv, xattn_wo, ffn2_norm, ffn2_w_gate, ffn2_w_up, ffn2_w_down, final_norm):
    batch, seq_len, d_model = x.shape
    depth = w_in.shape[0]
    assert batch == 1 and mem.shape[0] == 1
    assert seq_len % GRID_W == 0
    assert (seq_len // GRID_W) % (NA_UNROLL * NA_ROWS) == 0
    assert seq_len % (GQA_UNROLL * TK) == 0 and seq_len % TM == 0

    cos, sin = _rope_tables(seq_len)
    row = lambda v: v.reshape(1, -1)
    xs = x[0]
    mem2 = mem[0]
    for l in range(depth):
        bf = lambda w: w[l].astype(BF16)

        act = _ffn_up(xs, row(ffn1_norm[l]), ffn1_w_gate, ffn1_w_up, l)
        xs = _ffn_down(act, ffn1_w_down, l, xs)

        proj = _in_proj(xs, row(mix_norm[l]), w_in, l, row(q_norm_a[l]),
                        row(k_norm_a[l]), cos, sin)
        oa = _gqa(proj)
        ob = _na(proj, rpb_b[l].reshape(-1))
        xs = _out_proj(oa, ob, row(out_norm_a[l]), row(out_norm_b[l]),
                       w_out, l, xs)

        kv = _mem_kv(mem2, row(mem_norm[l]), bf(xattn_wkv))
        xs = _xattn(xs, row(xattn_norm[l]), bf(xattn_wq), kv, bf(xattn_wo))

        act = _ffn_up(xs, row(ffn2_norm[l]), ffn2_w_gate, ffn2_w_up, l)
        xs = _ffn_down(act, ffn2_w_down, l, xs)

    return _final_norm(xs, row(final_norm))[None]
```

```python
import functools

import jax
import jax.numpy as jnp
import numpy as np
from jax import lax
from jax.experimental import pallas as pl
from jax.experimental.pallas import tpu as pltpu

F32 = jnp.float32
BF16 = jnp.bfloat16

HEAD_DIM = 128
N_HEADS_A = 16
N_KV_A = 4
GQA_GROUP = N_HEADS_A // N_KV_A
N_HEADS_B = 16
WIDTH_A = N_HEADS_A * HEAD_DIM
WIDTH_B = N_HEADS_B * HEAD_DIM
KV_WIDTH_A = N_KV_A * HEAD_DIM
GRID_W = 64
WIN_H_MAX = 8
WIN_W = 16
N_HEADS_MEM = 4
MEM_WIDTH = N_HEADS_MEM * HEAD_DIM
ROPE_THETA = 10000.0
EPS = 1e-6
ATTN_SCALE = HEAD_DIM ** -0.5
LOG2_E = 1.4426950408889634
MASK_VALUE = -1e30

V7X_VMEM_BYTES = 64 * 1024 * 1024
VMEM_LIMIT_BYTES = V7X_VMEM_BYTES - 6 * 1024 * 1024

TM = 1024
TN_UP = 256
TN_DOWN = 512
KSPLIT_DOWN = 2
TN_PROJ = 512
NORM_COLS = 1024
NORM_CHUNK = 256
TQ = 256
TK = 512
GQA_ONES_ROWS = 16
GQA_UNROLL = 4
NA_ROWS = 4
NA_KROWS = 12
NA_UNROLL = 4
TM_X = 256
TM_FINAL = 512


def _params(*sem, fuse_inputs=None):
    return pltpu.CompilerParams(dimension_semantics=sem,
                                vmem_limit_bytes=VMEM_LIMIT_BYTES,
                                allow_input_fusion=fuse_inputs)


def _rms_norm_chunk(x_ref, g_ref, a_ref, r, chunk, col_off=0):
    d = x_ref.shape[1]
    cb = min(NORM_COLS, d)
    ss = jnp.zeros((chunk, 1), F32)
    for c0 in range(0, d, cb):
        xb = x_ref[pl.ds(r, chunk), c0:c0 + cb].astype(F32)
        ss = ss + jnp.sum(xb * xb, axis=-1, keepdims=True)
    inv = lax.rsqrt(ss * (1.0 / d) + EPS)
    for c0 in range(0, d, cb):
        xb = x_ref[pl.ds(r, chunk), c0:c0 + cb].astype(F32)
        y = (xb * inv) * g_ref[:, c0:c0 + cb]
        a_ref[pl.ds(r, chunk),
              col_off + c0:col_off + c0 + cb] = y.astype(a_ref.dtype)


def _rms_norm_rows(x_ref, g_ref, a_ref, col_off=0):
    tm = x_ref.shape[0]
    chunk = min(NORM_CHUNK, tm)

    def body(c, carry):
        _rms_norm_chunk(x_ref, g_ref, a_ref, pl.multiple_of(c * chunk, chunk),
                        chunk, col_off)
        return carry

    lax.fori_loop(0, tm // chunk, body, 0)


def _row_tile_copy(x_hbm, x_buf, sem, tile):
    rows = x_buf.shape[0]
    r = pl.multiple_of(tile * rows, rows)
    return pltpu.make_async_copy(x_hbm.at[pl.ds(r, rows), :], x_buf, sem)


def _normed_matmul_steps(sources, a_ref, compute_rows, first_step_rows=None):
    i = pl.program_id(0)
    j = pl.program_id(1)
    tm = a_ref.shape[0]
    chunk = min(NORM_CHUNK, tm)

    @pl.when(j == 0)
    def _():
        @pl.when(i == 0)
        def _():
            for x_hbm, x_buf, sem, _, _ in sources:
                _row_tile_copy(x_hbm, x_buf, sem, 0).start()

        for x_hbm, x_buf, sem, _, _ in sources:
            _row_tile_copy(x_hbm, x_buf, sem, i).wait()
        for r0 in range(0, tm, chunk):
            for _, x_buf, _, g_ref, col_off in sources:
                _rms_norm_chunk(x_buf, g_ref, a_ref, r0, chunk, col_off)
            (first_step_rows or compute_rows)(r0, chunk)

    @pl.when((j == 1) & (i + 1 < pl.num_programs(0)))
    def _():
        for x_hbm, x_buf, sem, _, _ in sources:
            _row_tile_copy(x_hbm, x_buf, sem, i + 1).start()

    @pl.when(j > 0)
    def _():
        compute_rows(0, tm)


def _ffn_up_kernel(x_hbm, g_ref, wg_ref, wu_ref, o_ref, x_buf, a_ref, sem):
    def swiglu_rows(r0, rows):
        a = a_ref[r0:r0 + rows, :]
        gate = jnp.dot(a, wg_ref[...].astype(BF16), preferred_element_type=F32)
        up = jnp.dot(a, wu_ref[...].astype(BF16), preferred_element_type=F32)
        o_ref[r0:r0 + rows, :] = (
            (gate * jax.nn.sigmoid(gate)) * up).astype(o_ref.dtype)

    _normed_matmul_steps([(x_hbm, x_buf, sem.at[0], g_ref, 0)], a_ref,
                         swiglu_rows)


def _ffn_up(x, g, wg, wu, layer):
    s, d = x.shape
    dff = wg.shape[2]
    return pl.pallas_call(
        _ffn_up_kernel,
        grid=(s // TM, dff // TN_UP),
        in_specs=[
            pl.BlockSpec(memory_space=pl.ANY),
            pl.BlockSpec((1, d), lambda i, j: (0, 0)),
            pl.BlockSpec((None, d, TN_UP), lambda i, j: (layer, 0, j)),
            pl.BlockSpec((None, d, TN_UP), lambda i, j: (layer, 0, j)),
        ],
        out_specs=pl.BlockSpec((TM, TN_UP), lambda i, j: (i, j)),
        out_shape=jax.ShapeDtypeStruct((s, dff), BF16),
        scratch_shapes=[pltpu.VMEM((TM, d), F32), pltpu.VMEM((TM, d), BF16),
                        pltpu.SemaphoreType.DMA((1,))],
        compiler_params=_params("arbitrary", "arbitrary"),
        name="ffn_up",
    )(x, g, wg, wu)


def _ffn_down_kernel(act_hbm, w_ref, r_ref, o_ref, a0_buf, a1_buf, sem):
    i = pl.program_id(0)
    j = pl.program_id(1)
    k = pl.program_id(2)
    n_i = pl.num_programs(0)
    n_j = pl.num_programs(1)
    tm, tk = a0_buf.shape
    bufs = (a0_buf, a1_buf)

    def half_copy(tile, half):
        r = pl.multiple_of(tile * tm, tm)
        return pltpu.make_async_copy(
            act_hbm.at[pl.ds(r, tm), pl.ds(half * tk, tk)], bufs[half],
            sem.at[half])

    @pl.when((j == 0) & (k == 0))
    def _():
        @pl.when(i == 0)
        def _():
            half_copy(0, 0).start()

        half_copy(i, 1).start()
        half_copy(i, 0).wait()

    @pl.when((j == 0) & (k == 1))
    def _():
        half_copy(i, 1).wait()

    @pl.when((j == n_j - 1) & (k == 1) & (i + 1 < n_i))
    def _():
        half_copy(i + 1, 0).start()

    for half in range(KSPLIT_DOWN):
        @pl.when(k == half)
        def _(half=half):
            part = 0.5 * jnp.dot(bufs[half][...], w_ref[...].astype(BF16),
                                 preferred_element_type=F32)
            if half == 0:
                o_ref[...] = r_ref[...] + part
            else:
                o_ref[...] += part


def _ffn_down(act, wd, layer, x):
    s, dff = act.shape
    d = wd.shape[2]
    assert KSPLIT_DOWN == 2 and d // TN_DOWN >= 2
    tk = dff // KSPLIT_DOWN
    return pl.pallas_call(
        _ffn_down_kernel,
        grid=(s // TM, d // TN_DOWN, KSPLIT_DOWN),
        in_specs=[
            pl.BlockSpec(memory_space=pl.ANY),
            pl.BlockSpec((None, tk, TN_DOWN), lambda i, j, k: (layer, k, j)),
            pl.BlockSpec((TM, TN_DOWN), lambda i, j, k: (i, j)),
        ],
        out_specs=pl.BlockSpec((TM, TN_DOWN), lambda i, j, k: (i, j)),
        out_shape=jax.ShapeDtypeStruct((s, d), F32),
        scratch_shapes=[pltpu.VMEM((TM, tk), BF16), pltpu.VMEM((TM, tk), BF16),
                        pltpu.SemaphoreType.DMA((KSPLIT_DOWN,))],
        compiler_params=_params("arbitrary", "arbitrary", "arbitrary"),
        name="ffn_down",
    )(act, wd, x)


def _head_pair_matrices():
    lane = np.arange(2 * HEAD_DIM)
    head = lane // HEAD_DIM
    ones = (head[:, None] == head[None, :]).astype(np.float32)
    first_half = (lane % (HEAD_DIM // 2)) < (HEAD_DIM // 4)
    partner = np.where(first_half, lane + HEAD_DIM // 4, lane - HEAD_DIM // 4)
    perm = np.zeros((2 * HEAD_DIM, 2 * HEAD_DIM), np.float32)
    perm[partner, lane] = 1.0
    return jnp.asarray(ones, BF16), jnp.asarray(perm, BF16)


def _in_proj_kernel(x_hbm, g_ref, w_ref, qg_ref, kg_ref, cos_ref, sin_ref,
                    ones_ref, perm_ref, o_ref, x_buf, a_ref, sem):
    j = pl.program_id(1)
    n_q_tiles = WIDTH_A // TN_PROJ
    n_qk_tiles = (WIDTH_A + KV_WIDTH_A) // TN_PROJ
    nb_q_first = (WIDTH_A + 2 * KV_WIDTH_A) // TN_PROJ
    def project(r0, rows):
        return jnp.dot(a_ref[r0:r0 + rows, :], w_ref[...].astype(BF16),
                       preferred_element_type=F32)

    def qk_rows(r0, rows):
        acc = project(r0, rows)
        is_q = j < n_q_tiles
        gain = jnp.where(is_q, qg_ref[...], kg_ref[...])
        gain = jnp.concatenate([gain, gain], axis=1)
        scale = jnp.where(is_q, ATTN_SCALE * LOG2_E, 1.0).astype(F32)
        cos = cos_ref[r0:r0 + rows, :]
        sin = sin_ref[r0:r0 + rows, :]
        cos = jnp.concatenate([cos, cos], axis=1)
        sin = jnp.concatenate([sin, sin], axis=1)
        pair = 2 * HEAD_DIM
        for hp in range(TN_PROJ // pair):
            xh = acc[:, hp * pair:(hp + 1) * pair]
            ss = jnp.dot((xh * xh).astype(BF16), ones_ref[...],
                         preferred_element_type=F32)
            y = (xh * lax.rsqrt(ss * (1.0 / HEAD_DIM) + EPS)) * gain
            partner = jnp.dot(y.astype(BF16), perm_ref[...],
                              preferred_element_type=F32)
            out = (y * cos + partner * sin) * scale
            o_ref[r0:r0 + rows, hp * pair:(hp + 1) * pair] = (
                out.astype(o_ref.dtype))

    def later_steps(r0, rows):
        @pl.when(j < n_qk_tiles)
        def _():
            qk_rows(r0, rows)

        @pl.when(j >= n_qk_tiles)
        def _():
            nb_q = (j >= nb_q_first) & (j < nb_q_first + WIDTH_B // TN_PROJ)
            scale = jnp.where(nb_q, ATTN_SCALE * LOG2_E, 1.0).astype(F32)
            o_ref[r0:r0 + rows, :] = (project(r0, rows) * scale).astype(
                o_ref.dtype)

    _normed_matmul_steps([(x_hbm, x_buf, sem.at[0], g_ref, 0)], a_ref,
                         later_steps, first_step_rows=qk_rows)


def _in_proj(x, g, w, layer, qg, kg, cos, sin):
    s, d = x.shape
    n = w.shape[2]
    ones, perm = _head_pair_matrices()
    pair = 2 * HEAD_DIM
    return pl.pallas_call(
        _in_proj_kernel,
        grid=(s // TM, n // TN_PROJ),
        in_specs=[
            pl.BlockSpec(memory_space=pl.ANY),
            pl.BlockSpec((1, d), lambda i, j: (0, 0)),
            pl.BlockSpec((None, d, TN_PROJ), lambda i, j: (layer, 0, j)),
            pl.BlockSpec((1, HEAD_DIM), lambda i, j: (0, 0)),
            pl.BlockSpec((1, HEAD_DIM), lambda i, j: (0, 0)),
            pl.BlockSpec((TM, HEAD_DIM), lambda i, j: (i, 0)),
            pl.BlockSpec((TM, HEAD_DIM), lambda i, j: (i, 0)),
            pl.BlockSpec((pair, pair), lambda i, j: (0, 0)),
            pl.BlockSpec((pair, pair), lambda i, j: (0, 0)),
        ],
        out_specs=pl.BlockSpec((TM, TN_PROJ), lambda i, j: (i, j)),
        out_shape=jax.ShapeDtypeStruct((s, n), BF16),
        scratch_shapes=[pltpu.VMEM((TM, d), F32), pltpu.VMEM((TM, d), BF16),
                        pltpu.SemaphoreType.DMA((1,))],
        compiler_params=_params("arbitrary", "arbitrary"),
        name="in_proj",
    )(x, g, w, qg, kg, cos, sin, ones, perm)


def _gqa_kernel(q_ref, k_ref, v_ref, o_ref, qs_ref, vt_ref, m_ref, acc_ref,
                st_ref):
    tq = o_ref.shape[0]
    s_len = k_ref.shape[0]
    n_chunks = s_len // TK
    i = pl.program_id(1)
    last_block = pl.num_programs(1) - 1

    def stack_queries(blk):
        r = pl.multiple_of(blk * tq, tq)
        for g in range(GQA_GROUP):
            qs_ref[g * tq:(g + 1) * tq, :] = (
                q_ref[pl.ds(r, tq), g * HEAD_DIM:(g + 1) * HEAD_DIM])

    def scores(c, slot):
        r = pl.multiple_of(c * TK, TK)
        st = lax.dot_general(k_ref[pl.ds(r, TK), :], qs_ref[...],
                             (((1,), (1,)), ((), ())),
                             preferred_element_type=F32)
        for g in range(GQA_GROUP):
            st_ref[slot, g] = st[:, g * tq:(g + 1) * tq]

    def softmax_pv(c, slot):
        for g in range(GQA_GROUP):
            st = st_ref[slot, g]
            m_old = m_ref[g]
            m_new = jnp.maximum(m_old, jnp.max(st, axis=0, keepdims=True))
            alpha = jnp.exp2(m_old - m_new)
            p = jnp.exp2(st - m_new)
            pv = jnp.dot(vt_ref[c], p.astype(BF16),
                         preferred_element_type=F32)
            acc_ref[g] = alpha * acc_ref[g] + pv
            m_ref[g] = m_new

    @pl.when(i == 0)
    def _():
        def transpose_chunk(c, carry):
            r = pl.multiple_of(c * TK, TK)
            vc = v_ref[pl.ds(r, TK), :].astype(F32)
            vt_ref[c, :HEAD_DIM, :] = vc.T.astype(vt_ref.dtype)
            vt_ref[c, HEAD_DIM:, :] = jnp.ones((GQA_ONES_ROWS, TK),
                                               vt_ref.dtype)
            return carry

        lax.fori_loop(0, n_chunks, transpose_chunk, 0)
        stack_queries(0)
        scores(0, 0)

    m_ref[...] = jnp.full(m_ref.shape, MASK_VALUE, F32)
    acc_ref[...] = jnp.zeros(acc_ref.shape, F32)

    def body(it, carry):
        c = GQA_UNROLL * it
        for u in range(GQA_UNROLL):
            scores(c + u + 1, (u + 1) % GQA_UNROLL)
            softmax_pv(c + u, u)
        return carry

    lax.fori_loop(0, n_chunks // GQA_UNROLL - 1, body, 0)
    c = n_chunks - GQA_UNROLL
    for u in range(GQA_UNROLL - 1):
        scores(c + u + 1, u + 1)
        softmax_pv(c + u, u)
    stack_queries(jnp.minimum(i + 1, last_block))
    scores(0, 0)
    softmax_pv(n_chunks - 1, GQA_UNROLL - 1)

    for g in range(GQA_GROUP):
        out = acc_ref[g, :HEAD_DIM, :] / acc_ref[g, HEAD_DIM:HEAD_DIM + 1, :]
        o_ref[:, g * HEAD_DIM:(g + 1) * HEAD_DIM] = out.T


def _gqa(proj):
    s = proj.shape[0]
    k_blk = WIDTH_A // HEAD_DIM
    v_blk = (WIDTH_A + KV_WIDTH_A) // HEAD_DIM
    gw = GQA_GROUP * HEAD_DIM
    return pl.pallas_call(
        _gqa_kernel,
        grid=(N_KV_A, s // TQ),
        in_specs=[
            pl.BlockSpec((s, gw), lambda h, i: (0, h)),
            pl.BlockSpec((s, HEAD_DIM), lambda h, i: (0, k_blk + h)),
            pl.BlockSpec((s, HEAD_DIM), lambda h, i: (0, v_blk + h)),
        ],
        out_specs=pl.BlockSpec((TQ, gw), lambda h, i: (i, h)),
        out_shape=jax.ShapeDtypeStruct((s, WIDTH_A), F32),
        scratch_shapes=[
            pltpu.VMEM((GQA_GROUP * TQ, HEAD_DIM), BF16),
            pltpu.VMEM((s // TK, HEAD_DIM + GQA_ONES_ROWS, TK), BF16),
            pltpu.VMEM((GQA_GROUP, 1, TQ), F32),
            pltpu.VMEM((GQA_GROUP, HEAD_DIM + GQA_ONES_ROWS, TQ), F32),
            pltpu.VMEM((GQA_UNROLL, GQA_GROUP, TK, TQ), F32),
        ],
        compiler_params=_params("arbitrary", "arbitrary"),
        name="gqa",
    )(proj, proj, proj)


def _na_classes(grid_rows):
    kh = min(WIN_H_MAX, grid_rows)
    n_blocks = grid_rows // NA_ROWS
    classes = []
    for rb in (0, 1, n_blocks - 1):
        start = min(max(NA_ROWS * rb - NA_ROWS, 0), grid_rows - NA_KROWS)
        table = []
        for qi in range(NA_ROWS):
            i = NA_ROWS * rb + qi
            rs = min(max(i - kh // 2, 0), grid_rows - kh)
            row = []
            for a in range(NA_KROWS):
                r = start + a
                row.append(r - i + (WIN_H_MAX - 1) if rs <= r < rs + kh else None)
            table.append(row)
        classes.append(table)
    return classes


def _na_kernel(rpb_ref, q_ref, k_ref, v_ref, o_ref, strip_ref, bias_ref, s_ref,
               *, classes, n_blocks):
    h = pl.program_id(0)
    n_rpb_rows = 2 * WIN_H_MAX - 1
    n_rpb_cols = 2 * WIN_W - 1
    qblk = NA_ROWS * GRID_W
    kblk = NA_KROWS * GRID_W

    jj = lax.broadcasted_iota(jnp.int32, (GRID_W, 2 * GRID_W), 0)
    cc = lax.broadcasted_iota(jnp.int32, (GRID_W, 2 * GRID_W), 1) % GRID_W
    rel = cc - jj + (WIN_W - 1)
    cs = jnp.clip(jj - WIN_W // 2, 0, GRID_W - WIN_W)
    col_ok = (cc >= cs) & (cc < cs + WIN_W)
    for dr in range(n_rpb_rows):
        base = (h * n_rpb_rows + dr) * n_rpb_cols

        def pick(d, t, base=base):
            return jnp.where(rel == d, rpb_ref[base + d], t)

        strip = lax.fori_loop(0, n_rpb_cols, pick,
                              jnp.zeros((GRID_W, 2 * GRID_W), F32))
        strip_ref[dr] = jnp.where(col_ok, strip * LOG2_E, MASK_VALUE)

    left = lax.broadcasted_iota(jnp.int32, (GRID_W, 2 * GRID_W), 1) < GRID_W
    masked = jnp.full((GRID_W, 2 * GRID_W), MASK_VALUE, F32)
    for cls, table in enumerate(classes):
        for qi in range(NA_ROWS):
            for ap in range(NA_KROWS // 2):
                dl, dr_ = table[qi][2 * ap], table[qi][2 * ap + 1]
                lhs = masked if dl is None else strip_ref[dl]
                rhs = masked if dr_ is None else strip_ref[dr_]
                bias_ref[cls, qi * GRID_W:(qi + 1) * GRID_W,
                         ap * 2 * GRID_W:(ap + 1) * 2 * GRID_W] = (
                             jnp.where(left, lhs, rhs))

    def key_start(rb):
        sb = jnp.clip(rb - 1, 0, n_blocks - NA_KROWS // NA_ROWS)
        return pl.multiple_of(sb * qblk, qblk)

    def scores(rb, slot):
        q0 = pl.multiple_of(rb * qblk, qblk)
        s_ref[slot] = lax.dot_general(q_ref[pl.ds(q0, qblk), :],
                                      k_ref[pl.ds(key_start(rb), kblk), :],
                                      (((1,), (1,)), ((), ())),
                                      preferred_element_type=F32)

    def softmax_pv(rb, slot):
        cls = jnp.where(rb == 0, 0, jnp.where(rb == n_blocks - 1, 2, 1))
        q0 = pl.multiple_of(rb * qblk, qblk)
        s = s_ref[slot] + bias_ref[cls]
        m = jnp.max(s, axis=-1, keepdims=True)
        p = jnp.exp2(s - m)
        l = jnp.sum(p, axis=-1, keepdims=True)
        o = jnp.dot(p.astype(BF16), v_ref[pl.ds(key_start(rb), kblk), :],
                    preferred_element_type=F32)
        o_ref[pl.ds(q0, qblk), :] = o / l

    scores(0, 0)

    def body(i, carry):
        rb = NA_UNROLL * i
        for u in range(NA_UNROLL):
            scores(jnp.minimum(rb + u + 1, n_blocks - 1), (u + 1) % NA_UNROLL)
            softmax_pv(rb + u, u)
        return carry

    lax.fori_loop(0, n_blocks // NA_UNROLL, body, 0)


def _na(proj, rpb_flat):
    s = proj.shape[0]
    grid_rows = s // GRID_W
    n_blocks = grid_rows // NA_ROWS
    q_blk = (WIDTH_A + 2 * KV_WIDTH_A) // HEAD_DIM
    k_blk = q_blk + N_HEADS_B
    v_blk = k_blk + N_HEADS_B
    kern = functools.partial(_na_kernel, classes=_na_classes(grid_rows),
                             n_blocks=n_blocks)
    return pl.pallas_call(
        kern,
        grid=(N_HEADS_B,),
        in_specs=[
            pl.BlockSpec(memory_space=pltpu.SMEM),
            pl.BlockSpec((s, HEAD_DIM), lambda h: (0, q_blk + h)),
            pl.BlockSpec((s, HEAD_DIM), lambda h: (0, k_blk + h)),
            pl.BlockSpec((s, HEAD_DIM), lambda h: (0, v_blk + h)),
        ],
        out_specs=pl.BlockSpec((s, HEAD_DIM), lambda h: (0, h)),
        out_shape=jax.ShapeDtypeStruct((s, WIDTH_B), F32),
        scratch_shapes=[
            pltpu.VMEM((2 * WIN_H_MAX - 1, GRID_W, 2 * GRID_W), F32),
            pltpu.VMEM((3, NA_ROWS * GRID_W, NA_KROWS * GRID_W), F32),
            pltpu.VMEM((NA_UNROLL, NA_ROWS * GRID_W, NA_KROWS * GRID_W), F32),
        ],
        compiler_params=_params("arbitrary"),
        name="na",
    )(rpb_flat, proj, proj, proj)


def _out_proj_kernel(oa_hbm, ob_hbm, ga_ref, gb_ref, w_ref, r_ref, o_ref,
                     oa_buf, ob_buf, a_ref, sem):
    def project_rows(r0, rows):
        acc = jnp.dot(a_ref[r0:r0 + rows, :], w_ref[...].astype(BF16),
                      preferred_element_type=F32)
        o_ref[r0:r0 + rows, :] = r_ref[r0:r0 + rows, :] + acc

    _normed_matmul_steps(
        [(oa_hbm, oa_buf, sem.at[0], ga_ref, 0),
         (ob_hbm, ob_buf, sem.at[1], gb_ref, oa_buf.shape[1])], a_ref,
        project_rows)


def _out_proj(oa, ob, ga, gb, w, layer, x):
    s, wa = oa.shape
    wb = ob.shape[1]
    d = w.shape[2]
    return pl.pallas_call(
        _out_proj_kernel,
        grid=(s // TM, d // TN_PROJ),
        in_specs=[
            pl.BlockSpec(memory_space=pl.ANY),
            pl.BlockSpec(memory_space=pl.ANY),
            pl.BlockSpec((1, wa), lambda i, j: (0, 0)),
            pl.BlockSpec((1, wb), lambda i, j: (0, 0)),
            pl.BlockSpec((None, wa + wb, TN_PROJ), lambda i, j: (layer, 0, j)),
            pl.BlockSpec((TM, TN_PROJ), lambda i, j: (i, j)),
        ],
        out_specs=pl.BlockSpec((TM, TN_PROJ), lambda i, j: (i, j)),
        out_shape=jax.ShapeDtypeStruct((s, d), F32),
        scratch_shapes=[pltpu.VMEM((TM, wa), F32), pltpu.VMEM((TM, wb), F32),
                        pltpu.VMEM((TM, wa + wb), BF16),
                        pltpu.SemaphoreType.DMA((2,))],
        compiler_params=_params("arbitrary", "arbitrary"),
        name="out_proj",
    )(oa, ob, ga, gb, w, x)


def _mem_kv_kernel(m_ref, g_ref, w_ref, o_ref, a_ref):
    @pl.when(pl.program_id(0) == 0)
    def _():
        _rms_norm_rows(m_ref, g_ref, a_ref)

    o_ref[...] = jnp.dot(a_ref[...], w_ref[...],
                         preferred_element_type=F32).astype(o_ref.dtype)


def _mem_kv(mem, g, wkv):
    m, d = mem.shape
    n = wkv.shape[1]
    return pl.pallas_call(
        _mem_kv_kernel,
        grid=(n // TN_PROJ,),
        in_specs=[
            pl.BlockSpec((m, d), lambda j: (0, 0)),
            pl.BlockSpec((1, d), lambda j: (0, 0)),
            pl.BlockSpec((d, TN_PROJ), lambda j: (0, j)),
        ],
        out_specs=pl.BlockSpec((m, TN_PROJ), lambda j: (0, j)),
        out_shape=jax.ShapeDtypeStruct((m, n), BF16),
        scratch_shapes=[pltpu.VMEM((m, d), BF16)],
        compiler_params=_params("arbitrary", fuse_inputs=[False, False, True]),
        name="mem_kv",
    )(mem, g, wkv)


def _xattn_kernel(x_ref, g_ref, wq_ref, kv_ref, wo_ref, o_ref, h_ref, oc_ref):
    _rms_norm_rows(x_ref, g_ref, h_ref)
    q = jnp.dot(h_ref[...], wq_ref[...], preferred_element_type=F32)
    for hd in range(N_HEADS_MEM):
        lo, hi = hd * HEAD_DIM, (hd + 1) * HEAD_DIM
        qh = q[:, lo:hi].astype(BF16)
        kh = kv_ref[:, lo:hi]
        vh = kv_ref[:, MEM_WIDTH + lo:MEM_WIDTH + hi]
        s = lax.dot_general(qh, kh, (((1,), (1,)), ((), ())),
                            preferred_element_type=F32) * ATTN_SCALE
        m = jnp.max(s, axis=-1, keepdims=True)
        p = jnp.exp(s - m)
        l = jnp.sum(p, axis=-1, keepdims=True)
        o = jnp.dot(p.astype(BF16), vh, preferred_element_type=F32) / l
        oc_ref[:, lo:hi] = o.astype(oc_ref.dtype)
    o_ref[...] = x_ref[...] + jnp.dot(oc_ref[...], wo_ref[...],
                                      preferred_element_type=F32)


def _xattn(x, g, wq, kv, wo):
    s, d = x.shape
    m = kv.shape[0]
    return pl.pallas_call(
        _xattn_kernel,
        grid=(s // TM_X,),
        in_specs=[
            pl.BlockSpec((TM_X, d), lambda i: (i, 0)),
            pl.BlockSpec((1, d), lambda i: (0, 0)),
            pl.BlockSpec((d, MEM_WIDTH), lambda i: (0, 0)),
            pl.BlockSpec((m, 2 * MEM_WIDTH), lambda i: (0, 0)),
            pl.BlockSpec((MEM_WIDTH, d), lambda i: (0, 0)),
        ],
        out_specs=pl.BlockSpec((TM_X, d), lambda i: (i, 0)),
        out_shape=jax.ShapeDtypeStruct((s, d), F32),
        scratch_shapes=[pltpu.VMEM((TM_X, d), BF16),
                        pltpu.VMEM((TM_X, MEM_WIDTH), BF16)],
        compiler_params=_params("parallel",
                                fuse_inputs=[False, False, True, False, True]),
        name="xattn",
    )(x, g, wq, kv, wo)


def _final_norm_kernel(x_ref, g_ref, o_ref):
    _rms_norm_rows(x_ref, g_ref, o_ref)


def _final_norm(x, g):
    s, d = x.shape
    return pl.pallas_call(
        _final_norm_kernel,
        grid=(s // TM_FINAL,),
        in_specs=[pl.BlockSpec((TM_FINAL, d), lambda i: (i, 0)),
                  pl.BlockSpec((1, d), lambda i: (0, 0))],
        out_specs=pl.BlockSpec((TM_FINAL, d), lambda i: (i, 0)),
        out_shape=jax.ShapeDtypeStruct((s, d), F32),
        compiler_params=_params("parallel"),
        name="final_norm",
    )(x, g)


def _rope_tables(seq_len):
    grid_rows = seq_len // GRID_W
    axis_dim = HEAD_DIM // 2
    inv_freq = ROPE_THETA ** (-jnp.arange(0, axis_dim, 2, dtype=F32) / axis_dim)
    ang_r = jnp.arange(grid_rows, dtype=F32)[:, None] * inv_freq
    ang_c = jnp.arange(GRID_W, dtype=F32)[:, None] * inv_freq
    per_row = lambda v: jnp.repeat(v, GRID_W, axis=0)
    per_col = lambda v: jnp.tile(v, (grid_rows, 1))
    cos_r, sin_r = per_row(jnp.cos(ang_r)), per_row(jnp.sin(ang_r))
    cos_c, sin_c = per_col(jnp.cos(ang_c)), per_col(jnp.sin(ang_c))
    cos = jnp.concatenate([cos_r, cos_r, cos_c, cos_c], axis=-1)
    sin = jnp.concatenate([-sin_r, sin_r, -sin_c, sin_c], axis=-1)
    return cos, sin


def kernel(x, mem, ffn1_norm, ffn1_w_gate, ffn1_w_up, ffn1_w_down, mix_norm, w_in, q_norm_a, k_norm_a, rpb_b, out_norm_a, out_norm_b, w_out, xattn_norm, mem_norm, xattn_wq, xattn_wkv, xattn_wo, ffn2_norm, ffn2_w_gate, ffn2_w_up, ffn2_w_down, final_norm):
    batch, seq_len, d_model = x.shape
    depth = w_in.shape[0]
    assert batch == 1 and mem.shape[0] == 1
    assert seq_len % GRID_W == 0
    assert (seq_len // GRID_W) % (NA_UNROLL * NA_ROWS) == 0
    assert seq_len % (GQA_UNROLL * TK) == 0 and seq_len % TM == 0

    cos, sin = _rope_tables(seq_len)
    row = lambda v: v.reshape(1, -1)
    xs = x[0]
    mem2 = mem[0]
    for l in range(depth):
        bf = lambda w: w[l].astype(BF16)

        act = _ffn_up(xs, row(ffn1_norm[l]), ffn1_w_gate, ffn1_w_up, l)
        xs = _ffn_down(act, ffn1_w_down, l, xs)

        proj = _in_proj(xs, row(mix_norm[l]), w_in, l, row(q_norm_a[l]),
                        row(k_norm_a[l]), cos, sin)
        oa = _gqa(proj)
        ob = _na(proj, rpb_b[l].reshape(-1))
        xs = _out_proj(oa, ob, row(out_norm_a[l]), row(out_norm_b[l]),
                       w_out, l, xs)

        kv = _mem_kv(mem2, row(mem_norm[l]), bf(xattn_wkv))
        xs = _xattn(xs, row(xattn_norm[l]), bf(xattn_wq), kv, bf(xattn_wo))

        act = _ffn_up(xs, row(ffn2_norm[l]), ffn2_w_gate, ffn2_w_up, l)
        xs = _ffn_down(act, ffn2_w_down, l, xs)

    return _final_norm(xs, row(final_norm))[None]
```
